```python
import jax, jax.numpy as jnp
from jax import lax
import numpy as np

D_MODEL = 1024
BATCH = 16
SEQ = 2048
DEPTH = 1

HEAD_DIM = 64
A_HEADS = 8
A_KV_HEADS = 8
B_HEADS = 8
B_KV_HEADS = 2
N_ATTN_HEADS = A_HEADS + B_HEADS
MOBA_BLOCK = 256
MOBA_TOPK = 3
MOBA_QCHUNK = 128
SWA_WINDOW = 128
D_FF = 4 * D_MODEL
EPS = 1e-6
NEG = -1e30
SCALE = HEAD_DIM ** -0.5

W_QA = A_HEADS * HEAD_DIM
W_KA = A_KV_HEADS * HEAD_DIM
W_VA = A_KV_HEADS * HEAD_DIM
W_QB = B_HEADS * HEAD_DIM
W_KB = B_KV_HEADS * HEAD_DIM
W_VB = B_KV_HEADS * HEAD_DIM
W_GATE = D_MODEL
IN_WIDTH = W_QA + W_KA + W_VA + W_QB + W_KB + W_VB + 2 * W_GATE

kernel_name = "hybrid_moba_swa_gated_block"


def _rmsnorm(x, g):
    xf = x.astype(jnp.float32)
    y = xf * lax.rsqrt(jnp.mean(xf * xf, axis=-1, keepdims=True) + EPS)
    return (y * g.astype(jnp.float32)).astype(x.dtype)


def _alibi_slopes(n):
    return jnp.exp2(-(8.0 / n) * jnp.arange(1, n + 1, dtype=jnp.float32))


def _moba(q, k, v, slopes):
    B, S, H, hd = q.shape
    nb = -(-S // MOBA_BLOCK)
    pad = nb * MOBA_BLOCK - S
    padw = ((0, 0), (0, pad), (0, 0), (0, 0))
    kt = jnp.pad(k, padw).reshape(B, nb, MOBA_BLOCK, H, hd).transpose(0, 3, 1, 2, 4)
    vt = jnp.pad(v, padw).reshape(B, nb, MOBA_BLOCK, H, hd).transpose(0, 3, 1, 2, 4)
    qh = q.transpose(0, 2, 1, 3)

    kmean = jnp.mean(kt.astype(jnp.float32), axis=3)
    gs = jnp.einsum('bhsd,bhnd->bhsn', qh.astype(jnp.float32), kmean)
    qblk = jnp.arange(S) // MOBA_BLOCK
    past = jnp.arange(nb)[None, :] < qblk[:, None]
    gs = jnp.where(past, gs, -jnp.inf)
    ksel = min(MOBA_TOPK, nb)
    _, idx = lax.top_k(gs, ksel)

    nc = S // MOBA_QCHUNK
    qc = qh.reshape(B, H, nc, MOBA_QCHUNK, hd).transpose(0, 2, 1, 3, 4).reshape(B * nc, H, MOBA_QCHUNK, hd)
    ic = idx.reshape(B, H, nc, MOBA_QCHUNK, ksel).transpose(0, 2, 1, 3, 4).reshape(B * nc, H, MOBA_QCHUNK, ksel)
    b_ids = jnp.repeat(jnp.arange(B), nc)
    c_ids = jnp.tile(jnp.arange(nc), B)

    def chunk(args):
        qq, ii, b, c = args
        kb_b = kt[b]
        vb_b = vt[b]
        t = c * MOBA_QCHUNK + jnp.arange(MOBA_QCHUNK)
        own = (c * MOBA_QCHUNK) // MOBA_BLOCK
        gather = jax.vmap(lambda kh, ih: kh[ih])
        k_sel = gather(kb_b, ii)
        v_sel = gather(vb_b, ii)
        s_sel = jnp.einsum('hqd,hqkpd->hqkp', qq, k_sel).astype(jnp.float32) * SCALE
        pos_sel = ii[..., None] * MOBA_BLOCK + jnp.arange(MOBA_BLOCK)
        d_sel = (t[None, :, None, None] - pos_sel).astype(jnp.float32)
        slot_ok = (jnp.arange(ksel) < own)[None, None, :, None]
        s_sel = jnp.where(slot_ok, s_sel - slopes[:, None, None, None] * d_sel, NEG)
        k_own = lax.dynamic_index_in_dim(kb_b, own, axis=1, keepdims=False)
        v_own = lax.dynamic_index_in_dim(vb_b, own, axis=1, keepdims=False)
        s_own = jnp.einsum('hqd,hpd->hqp', qq, k_own).astype(jnp.float32) * SCALE
        d_own = t[:, None] - (own * MOBA_BLOCK + jnp.arange(MOBA_BLOCK))[None, :]
        s_own = jnp.where((d_own >= 0)[None], s_own - slopes[:, None, None] * d_own.astype(jnp.float32)[None], NEG)
        H_, QC = qq.shape[0], qq.shape[1]
        scores = jnp.concatenate([s_sel.reshape(H_, QC, ksel * MOBA_BLOCK), s_own], axis=-1)
        p = jax.nn.softmax(scores, axis=-1).astype(qq.dtype)
        p_sel = p[..., :ksel * MOBA_BLOCK].reshape(H_, QC, ksel, MOBA_BLOCK)
        p_own = p[..., ksel * MOBA_BLOCK:]
        return (jnp.einsum('hqkp,hqkpd->hqd', p_sel, v_sel)
                + jnp.einsum('hqp,hpd->hqd', p_own, v_own))

    o = lax.map(chunk, (qc, ic, b_ids, c_ids))
    return o.reshape(B, nc, H, MOBA_QCHUNK, hd).transpose(0, 1, 3, 2, 4).reshape(B, S, H * hd)


def _swa(q, k, v, slopes, sinks):
    B, S, Hq, hd = q.shape
    Hkv = k.shape[2]
    G = Hq // Hkv
    W = SWA_WINDOW
    nblk = S // W
    qb = q.reshape(B, nblk, W, Hkv, G, hd)
    kb = k.reshape(B, nblk, W, Hkv, hd)
    vb = v.reshape(B, nblk, W, Hkv, hd)
    shift = lambda a: jnp.concatenate([jnp.zeros_like(a[:, :1]), a[:, :-1]], axis=1)
    kc = jnp.concatenate([shift(kb), kb], axis=2)
    vc = jnp.concatenate([shift(vb), vb], axis=2)
    s = jnp.einsum('bnqhgd,bnkhd->bhgnqk', qb, kc).astype(jnp.float32) * SCALE
    dist = jnp.arange(W)[:, None] - jnp.arange(2 * W)[None, :] + W
    kpos = jnp.arange(nblk)[:, None] * W - W + jnp.arange(2 * W)[None, :]
    valid = ((dist >= 0) & (dist < W))[None, :, :] & (kpos >= 0)[:, None, :]
    sl = slopes.reshape(Hkv, G)[:, :, None, None, None]
    s = jnp.where(valid, s - sl * dist.astype(jnp.float32), NEG)
    sink = sinks.astype(jnp.float32).reshape(Hkv, G)[None, :, :, None, None, None]
    m = jnp.maximum(jnp.max(s, axis=-1, keepdims=True), sink)
    e = jnp.exp(s - m)
    p = e / (jnp.sum(e, axis=-1, keepdims=True) + jnp.exp(sink - m))
    o = jnp.einsum('bhgnqk,bnkhd->bnqhgd', p.astype(v.dtype), vc)
    return o.reshape(B, S, Hq * hd)


def setup_inputs(seed: int = 0) -> dict:
    key = jax.random.key(seed)
    ks = jax.random.split(key, 16)
    f = jnp.float32
    nrm = lambda k, shape, fan: jax.random.normal(k, shape, f) * fan ** -0.5
    gain = lambda k, shape: 1.0 + 0.1 * jax.random.normal(k, shape, f)
    return {
        "x": jax.random.normal(ks[0], (BATCH, SEQ, D_MODEL), f),
        "norm_attn": gain(ks[1], (DEPTH, D_MODEL)),
        "w_in": nrm(ks[2], (DEPTH, D_MODEL, IN_WIDTH), D_MODEL),
        "q_norm_a": gain(ks[3], (DEPTH, HEAD_DIM)),
        "k_norm_a": gain(ks[4], (DEPTH, HEAD_DIM)),
        "q_norm_b": gain(ks[5], (DEPTH, HEAD_DIM)),
        "k_norm_b": gain(ks[6], (DEPTH, HEAD_DIM)),
        "sinks_b": 0.5 * jax.random.normal(ks[7], (DEPTH, B_HEADS), f),
        "w_branch_a": nrm(ks[8], (DEPTH, A_HEADS * HEAD_DIM, D_MODEL), A_HEADS * HEAD_DIM),
        "w_branch_b": nrm(ks[9], (DEPTH, B_HEADS * HEAD_DIM, D_MODEL), B_HEADS * HEAD_DIM),
        "w_out": nrm(ks[10], (DEPTH, D_MODEL, D_MODEL), D_MODEL),
        "norm_mlp": gain(ks[11], (DEPTH, D_MODEL)),
        "w_up": nrm(ks[12], (DEPTH, D_MODEL, D_FF), D_MODEL),
        "w_down": nrm(ks[13], (DEPTH, D_FF, D_MODEL), D_FF),
    }


def reference(x, norm_attn, w_in, q_norm_a, k_norm_a, q_norm_b, k_norm_b, sinks_b,
              w_branch_a, w_branch_b, w_out, norm_mlp, w_up, w_down):
    B, S, _ = x.shape
    slopes = _alibi_slopes(N_ATTN_HEADS)
    slopes_b = slopes[:B_HEADS]
    slopes_a = slopes[B_HEADS:]
    widths = (W_QA, W_KA, W_VA, W_QB, W_KB, W_VB, W_GATE)
    splits = []
    acc = 0
    for w in widths:
        acc += w
        splits.append(acc)
    for l in range(DEPTH):
        h = _rmsnorm(x, norm_attn[l])
        proj = h @ w_in[l]
        qa, ka, va, qb, kb, vb, ga, gb = jnp.split(proj, splits, axis=-1)
        qa = _rmsnorm(qa.reshape(B, S, A_HEADS, HEAD_DIM), q_norm_a[l])
        ka = _rmsnorm(ka.reshape(B, S, A_KV_HEADS, HEAD_DIM), k_norm_a[l])
        va = va.reshape(B, S, A_KV_HEADS, HEAD_DIM)
        qb = _rmsnorm(qb.reshape(B, S, B_HEADS, HEAD_DIM), q_norm_b[l])
        kb = _rmsnorm(kb.reshape(B, S, B_KV_HEADS, HEAD_DIM), k_norm_b[l])
        vb = vb.reshape(B, S, B_KV_HEADS, HEAD_DIM)
        oa = _moba(qa, ka, va, slopes_a)
        ob = _swa(qb, kb, vb, slopes_b, sinks_b[l])
        mixed = (jax.nn.sigmoid(ga) * (oa @ w_branch_a[l])
                 + jax.nn.sigmoid(gb) * (ob @ w_branch_b[l]))
        x = x + mixed @ w_out[l]
        h2 = _rmsnorm(x, norm_mlp[l])
        x = x + jnp.square(jax.nn.relu(h2 @ w_up[l])) @ w_down[l]
    return x
```

```python
import functools

import jax
import jax.numpy as jnp
from jax import lax
from jax.experimental import pallas as pl
from jax.experimental.pallas import tpu as pltpu

D_MODEL = 1024
HEAD_DIM = 64
A_HEADS = 8
B_HEADS = 8
B_KV_HEADS = 2
B_GROUP = B_HEADS // B_KV_HEADS
N_ATTN_HEADS = A_HEADS + B_HEADS
MOBA_BLOCK = 256
MOBA_TOPK = 3
SWA_WINDOW = 128
D_FF = 4 * D_MODEL
EPS = 1e-6
NEG = -1e30
SCALE = HEAD_DIM ** -0.5

W_A = A_HEADS * HEAD_DIM
W_QB = B_HEADS * HEAD_DIM
W_KB = B_KV_HEADS * HEAD_DIM
W_KB_DUP = 2 * W_KB
W_V = W_A + W_KB
W_GATES = 2 * D_MODEL

C_QA = 0
C_KA = C_QA + W_A
C_QB = C_KA + W_A
C_KB = C_QB + W_QB
C_G = C_KB + W_KB_DUP
C_END = C_G + W_GATES

TOKEN_TILE = 512
FF_CHUNK = 1024
VMEM_LIMIT = 48 * 1024 * 1024

_NT = (((1,), (1,)), ((), ()))
_BF16 = jnp.bfloat16
_F32 = jnp.float32


def _const_spec(shape):
    return pl.BlockSpec(shape, lambda *_: (0,) * len(shape), pipeline_mode=pl.Buffered(1))


def _inproj_kernel(x_ref, gn_ref, w_ref, wvt_ref, bd_ref, gqa_ref, gka_ref, gqb_ref, gkb_ref,
                   qa_ref, ka_ref, qb_ref, kb_ref, g_ref, vat_ref, vbt_ref, km_ref):
    x = x_ref[...]
    ms = jnp.mean(x * x, axis=-1, keepdims=True)
    h = ((x * lax.rsqrt(ms + EPS)) * gn_ref[...]).astype(_BF16)

    def proj(lo, hi):
        return jnp.dot(h, w_ref[:, lo:hi], preferred_element_type=_F32)

    def head_norm(y, gain_ref):
        width = y.shape[-1]
        sq = (y * y).astype(_BF16)
        msq = jnp.dot(sq, bd_ref[:width, :width], preferred_element_type=_F32)
        return (y * lax.rsqrt(msq + EPS)) * gain_ref[...]

    qa_ref[...] = (head_norm(proj(C_QA, C_KA), gqa_ref) * SCALE).astype(_BF16)
    kn = head_norm(proj(C_KA, C_QB), gka_ref)
    ka_ref[...] = kn.astype(_BF16)
    nblk = kn.shape[0] // MOBA_BLOCK
    km_ref[0] = kn.reshape(nblk, MOBA_BLOCK, W_A).sum(axis=1) * (1.0 / MOBA_BLOCK)
    qb_ref[...] = (head_norm(proj(C_QB, C_KB), gqb_ref) * SCALE).astype(_BF16)
    kb_ref[...] = head_norm(proj(C_KB, C_G), gkb_ref).astype(_BF16)
    g_ref[...] = proj(C_G, C_END).astype(_BF16)

    vt = lax.dot_general(wvt_ref[...], h, _NT, preferred_element_type=_F32)
    for c in range(vat_ref.shape[0]):
        vat_ref[c] = vt[:W_A, c * MOBA_BLOCK:(c + 1) * MOBA_BLOCK].astype(_BF16)
    for c in range(vbt_ref.shape[0]):
        vbt_ref[c] = vt[W_A:, c * SWA_WINDOW:(c + 1) * SWA_WINDOW].astype(_BF16)


def _inproj(x2, gn, w_main, w_vt, bd, gqa, gka, gqb, gkb):
    n = x2.shape[0]
    tm = TOKEN_TILE
    row = lambda w: pl.BlockSpec((tm, w), lambda i: (i, 0))
    out_shape = (
        jax.ShapeDtypeStruct((n, W_A), _BF16),
        jax.ShapeDtypeStruct((n, W_A), _BF16),
        jax.ShapeDtypeStruct((n, W_QB), _BF16),
        jax.ShapeDtypeStruct((n, W_KB_DUP), _BF16),
        jax.ShapeDtypeStruct((n, W_GATES), _BF16),
        jax.ShapeDtypeStruct((n // MOBA_BLOCK, W_A, MOBA_BLOCK), _BF16),
        jax.ShapeDtypeStruct((n // SWA_WINDOW, W_KB, SWA_WINDOW), _BF16),
        jax.ShapeDtypeStruct((n // tm, tm // MOBA_BLOCK, W_A), _F32),
    )
    out_specs = (
        row(W_A), row(W_A), row(W_QB), row(W_KB_DUP), row(W_GATES),
        pl.BlockSpec((tm // MOBA_BLOCK, W_A, MOBA_BLOCK), lambda i: (i, 0, 0)),
        pl.BlockSpec((tm // SWA_WINDOW, W_KB, SWA_WINDOW), lambda i: (i, 0, 0)),
        pl.BlockSpec((1, tm // MOBA_BLOCK, W_A), lambda i: (i, 0, 0)),
    )
    in_specs = [row(D_MODEL), _const_spec(gn.shape), _const_spec(w_main.shape), _const_spec(w_vt.shape),
                _const_spec(bd.shape), _const_spec(gqa.shape), _const_spec(gka.shape),
                _const_spec(gqb.shape), _const_spec(gkb.shape)]
    return pl.pallas_call(
        _inproj_kernel, grid=(n // tm,), in_specs=in_specs, out_specs=out_specs, out_shape=out_shape,
        compiler_params=pltpu.CompilerParams(dimension_semantics=("parallel",),
                                             vmem_limit_bytes=VMEM_LIMIT),
        name="inproj",
    )(x2, gn, w_main, w_vt, bd, gqa, gka, gqb, gkb)


def _moba_kernel(slopes_ref, q_ref, k_ref, vt_ref, km_ref, o_ref, relb_sc, relo_sc, r_sc):
    hp = pl.program_id(1)
    blk = MOBA_BLOCK
    nb = q_ref.shape[0] // blk
    kp = lax.broadcasted_iota(jnp.int32, (blk, blk), 0)
    qp = lax.broadcasted_iota(jnp.int32, (blk, blk), 1)
    rel = (kp - qp).astype(_F32)
    lane = lax.broadcasted_iota(jnp.int32, (blk, 2 * HEAD_DIM), 1)
    ridx = lax.broadcasted_iota(jnp.int32, (nb, blk), 0)
    for e in range(2):
        rb = slopes_ref[2 * hp + e] * rel
        relb_sc[e] = rb
        relo_sc[e] = jnp.where(kp <= qp, rb, NEG)
    km = km_ref[0].astype(_BF16)

    def q_tile(i, carry):
        row0 = pl.multiple_of(i * blk, blk)
        q_t = q_ref[pl.ds(row0, blk), :]
        k_own = k_ref[pl.ds(row0, blk), :]
        outs = []
        for e in range(2):
            slope = slopes_ref[2 * hp + e]
            in_head = (lane >= e * HEAD_DIM) & (lane < (e + 1) * HEAD_DIM)
            qm = jnp.where(in_head, q_t, jnp.zeros_like(q_t))
            gs = lax.dot_general(km, qm, _NT, preferred_element_type=_F32)
            past = ridx < i
            r = jnp.full((nb, blk), NEG, _F32)
            for n in range(nb - 1):
                row = gs[n:n + 1, :]
                ahead = ((gs > row) | ((gs == row) & (ridx < n))) & past
                rank = jnp.sum(ahead.astype(_F32), axis=0, keepdims=True)
                r = jnp.where((ridx == n) & (rank < MOBA_TOPK), 0.0, r)
            r_sc[...] = r

            s = lax.dot_general(k_own, qm, _NT, preferred_element_type=_F32) + relo_sc[e]
            m = jnp.max(s, axis=0, keepdims=True)
            p = jnp.exp(s - m)
            l = jnp.sum(p, axis=0, keepdims=True)
            acc = jnp.dot(vt_ref[i, e * HEAD_DIM:(e + 1) * HEAD_DIM, :], p.astype(_BF16),
                          preferred_element_type=_F32)

            def past_block(n, c):
                m, l, acc = c
                k_n = k_ref[pl.ds(pl.multiple_of(n * blk, blk), blk), :]
                t = lax.dot_general(k_n, qm, _NT, preferred_element_type=_F32) + relb_sc[e]
                radj = r_sc[pl.ds(n, 1), :] - slope * (blk * (i - n)).astype(_F32)
                m_new = jnp.maximum(m, jnp.max(t, axis=0, keepdims=True) + radj)
                alpha = jnp.exp(m - m_new)
                p = jnp.exp(t - (m_new - radj))
                l = alpha * l + jnp.sum(p, axis=0, keepdims=True)
                pv = jnp.dot(vt_ref[n, e * HEAD_DIM:(e + 1) * HEAD_DIM, :], p.astype(_BF16),
                             preferred_element_type=_F32)
                return m_new, l, alpha * acc + pv

            m, l, acc = lax.fori_loop(0, i, past_block, (m, l, acc))
            outs.append(acc / l)
        o = jnp.concatenate(outs, axis=0)
        o_ref[pl.ds(row0, blk), :] = o.T.astype(o_ref.dtype)
        return carry

    lax.fori_loop(0, nb, q_tile, 0)


def _moba(slopes, qa, ka, vat, km, batch, seq):
    nb = seq // MOBA_BLOCK
    pair = 2 * HEAD_DIM
    seq_spec = pl.BlockSpec((seq, pair), lambda b, hp: (b, hp))
    return pl.pallas_call(
        _moba_kernel, grid=(batch, A_HEADS // 2),
        in_specs=[pl.BlockSpec(memory_space=pltpu.SMEM), seq_spec, seq_spec,
                  pl.BlockSpec((nb, pair, MOBA_BLOCK), lambda b, hp: (b, hp, 0)),
                  pl.BlockSpec((1, nb, pair), lambda b, hp: (b, 0, hp))],
        out_specs=seq_spec,
        out_shape=jax.ShapeDtypeStruct(qa.shape, _BF16),
        scratch_shapes=[pltpu.VMEM((2, MOBA_BLOCK, MOBA_BLOCK), _F32),
                        pltpu.VMEM((2, MOBA_BLOCK, MOBA_BLOCK), _F32),
                        pltpu.VMEM((nb, MOBA_BLOCK), _F32)],
        compiler_params=pltpu.CompilerParams(dimension_semantics=("parallel", "parallel"),
                                             vmem_limit_bytes=VMEM_LIMIT),
        name="moba",
    )(slopes, qa, ka, vat, km)


def _swa_kernel(slopes_ref, sinks_ref, q_ref, k_ref, vt_ref, o_ref, bias_sc):
    hk = pl.program_id(1)
    w = SWA_WINDOW
    nblk = q_ref.shape[0] // w
    kp = lax.broadcasted_iota(jnp.int32, (2 * w, 2 * w), 0)
    col = lax.broadcasted_iota(jnp.int32, (2 * w, 2 * w), 1)
    dist = (col & (w - 1)) + w - kp
    in_window = (dist >= 0) & (dist < w)
    first_head = col < w
    lane = lax.broadcasted_iota(jnp.int32, (w, 2 * HEAD_DIM), 1)
    col_row = lax.broadcasted_iota(jnp.int32, (1, 2 * w), 1)
    for pr in range(2):
        ha = hk * B_GROUP + 2 * pr
        slope = jnp.where(first_head, slopes_ref[ha], slopes_ref[ha + 1])
        bias_sc[pr] = jnp.where(in_window, -slope * dist.astype(_F32), NEG)

    def block(j, first):
        row0 = 0 if first else pl.multiple_of(j * w, w)
        for pr in range(2):
            ha = hk * B_GROUP + 2 * pr
            q_t = q_ref[pl.ds(row0, w), pr * 2 * HEAD_DIM:(pr + 1) * 2 * HEAD_DIM]
            zero = jnp.zeros_like(q_t)
            qs = jnp.concatenate([jnp.where(lane < HEAD_DIM, q_t, zero),
                                  jnp.where(lane >= HEAD_DIM, q_t, zero)], axis=0)
            if first:
                keys = k_ref[0:w, :]
                bias = bias_sc[pr, w:2 * w, :]
            else:
                keys = k_ref[pl.ds(pl.multiple_of(row0 - w, w), 2 * w), :]
                bias = bias_sc[pr]
            s = lax.dot_general(keys, qs, _NT, preferred_element_type=_F32) + bias
            sink = jnp.where(col_row < w, sinks_ref[ha], sinks_ref[ha + 1])
            m = jnp.maximum(jnp.max(s, axis=0, keepdims=True), sink)
            p = jnp.exp(s - m)
            den = jnp.sum(p, axis=0, keepdims=True) + jnp.exp(sink - m)
            pb = p.astype(_BF16)
            if first:
                ot = jnp.dot(vt_ref[0], pb, preferred_element_type=_F32)
            else:
                ot = (jnp.dot(vt_ref[j - 1], pb[:w], preferred_element_type=_F32)
                      + jnp.dot(vt_ref[j], pb[w:], preferred_element_type=_F32))
            ot = ot / den
            o2 = jnp.concatenate([ot[:, :w], ot[:, w:]], axis=0)
            o_ref[pl.ds(row0, w), pr * 2 * HEAD_DIM:(pr + 1) * 2 * HEAD_DIM] = o2.T.astype(o_ref.dtype)

    block(0, True)

    def body(j, carry):
        block(j, False)
        return carry

    lax.fori_loop(1, nblk, body, 0)


def _swa(slopes, sinks, qb, kb, vbt, batch, seq):
    nblk = seq // SWA_WINDOW
    grp = B_GROUP * HEAD_DIM
    q_spec = pl.BlockSpec((seq, grp), lambda b, hk: (b, hk))
    return pl.pallas_call(
        _swa_kernel, grid=(batch, B_KV_HEADS),
        in_specs=[pl.BlockSpec(memory_space=pltpu.SMEM), pl.BlockSpec(memory_space=pltpu.SMEM), q_spec,
                  pl.BlockSpec((seq, 2 * HEAD_DIM), lambda b, hk: (b, hk)),
                  pl.BlockSpec((nblk, HEAD_DIM, SWA_WINDOW), lambda b, hk: (b, hk, 0))],
        out_specs=q_spec,
        out_shape=jax.ShapeDtypeStruct(qb.shape, _BF16),
        scratch_shapes=[pltpu.VMEM((2, 2 * SWA_WINDOW, 2 * SWA_WINDOW), _F32)],
        compiler_params=pltpu.CompilerParams(dimension_semantics=("parallel", "parallel"),
                                             vmem_limit_bytes=VMEM_LIMIT),
        name="swa",
    )(slopes, sinks, qb, kb, vbt)


def _merge_mlp_kernel(x_ref, oa_ref, ob_ref, g_ref, wa_ref, wb_ref, wo_ref, gm_ref, wup_ref, wdn_ref, o_ref):
    a = jnp.dot(oa_ref[...], wa_ref[...], preferred_element_type=_F32)
    b = jnp.dot(ob_ref[...], wb_ref[...], preferred_element_type=_F32)
    ga = g_ref[:, :D_MODEL].astype(_F32)
    gb = g_ref[:, D_MODEL:].astype(_F32)
    mixed = jax.nn.sigmoid(ga) * a + jax.nn.sigmoid(gb) * b
    x1 = x_ref[...] + jnp.dot(mixed.astype(_BF16), wo_ref[...], preferred_element_type=_F32)
    ms = jnp.mean(x1 * x1, axis=-1, keepdims=True)
    h2 = ((x1 * lax.rsqrt(ms + EPS)) * gm_ref[...]).astype(_BF16)
    acc = x1
    for c in range(D_FF // FF_CHUNK):
        u = jnp.dot(h2, wup_ref[:, c * FF_CHUNK:(c + 1) * FF_CHUNK], preferred_element_type=_F32)
        u = jnp.square(jnp.maximum(u, 0.0)).astype(_BF16)
        acc = acc + jnp.dot(u, wdn_ref[c * FF_CHUNK:(c + 1) * FF_CHUNK, :], preferred_element_type=_F32)
    o_ref[...] = acc


def _merge_mlp(x2, oa, ob, g, wa, wb, wo, gm, wup, wdn):
    n = x2.shape[0]
    tm = TOKEN_TILE
    row = lambda w: pl.BlockSpec((tm, w), lambda i: (i, 0))
    return pl.pallas_call(
        _merge_mlp_kernel, grid=(n // tm,),
        in_specs=[row(D_MODEL), row(W_A), row(W_QB), row(W_GATES), _const_spec(wa.shape), _const_spec(wb.shape),
                  _const_spec(wo.shape), _const_spec(gm.shape), _const_spec(wup.shape), _const_spec(wdn.shape)],
        out_specs=row(D_MODEL),
        out_shape=jax.ShapeDtypeStruct(x2.shape, x2.dtype),
        compiler_params=pltpu.CompilerParams(dimension_semantics=("parallel",),
                                             vmem_limit_bytes=VMEM_LIMIT),
        name="merge_mlp",
    )(x2, oa, ob, g, wa, wb, wo, gm, wup, wdn)


def _alibi_slopes(n):
    return jnp.exp2(-(8.0 / n) * jnp.arange(1, n + 1, dtype=_F32))


def kernel(x, norm_attn, w_in, q_norm_a, k_norm_a, q_norm_b, k_norm_b, sinks_b, w_branch_a, w_branch_b, w_out,
           norm_mlp, w_up, w_down):
    batch, seq, d = x.shape
    assert d == D_MODEL and seq % TOKEN_TILE == 0 and TOKEN_TILE % MOBA_BLOCK == 0
    slopes = _alibi_slopes(N_ATTN_HEADS)
    slopes_b, slopes_a = slopes[:B_HEADS], slopes[B_HEADS:]
    head_of = jnp.arange(W_A) // HEAD_DIM
    bd = jnp.where(head_of[:, None] == head_of[None, :], 1.0 / HEAD_DIM, 0.0).astype(_BF16)

    x2 = x.reshape(batch * seq, d)
    for l in range(norm_attn.shape[0]):
        w = w_in[l]
        o = 0
        cols = []
        for width in (W_A, W_A, W_A, W_QB, W_KB, W_KB, W_GATES):
            cols.append(w[:, o:o + width])
            o += width
        w_qa, w_ka, w_va, w_qb, w_kb, w_vb, w_g = cols
        kb0, kb1 = w_kb[:, :HEAD_DIM], w_kb[:, HEAD_DIM:]
        w_main = jnp.concatenate([w_qa, w_ka, w_qb, kb0, kb0, kb1, kb1, w_g], axis=1).astype(_BF16)
        w_vt = jnp.concatenate([w_va, w_vb], axis=1).T.astype(_BF16)
        tile_gain = lambda g, reps: jnp.tile(g, reps)[None, :]
        qa, ka, qb, kb, g, vat, vbt, km = _inproj(
            x2, norm_attn[l][None, :], w_main, w_vt, bd,
            tile_gain(q_norm_a[l], A_HEADS), tile_gain(k_norm_a[l], A_HEADS),
            tile_gain(q_norm_b[l], B_HEADS), tile_gain(k_norm_b[l], 2 * B_KV_HEADS))
        km = km.reshape(batch, seq // MOBA_BLOCK, W_A)
        oa = _moba(slopes_a, qa, ka, vat, km, batch, seq)
        ob = _swa(slopes_b, sinks_b[l], qb, kb, vbt, batch, seq)
        x2 = _merge_mlp(x2, oa, ob, g, w_branch_a[l].astype(_BF16), w_branch_b[l].astype(_BF16),
                        w_out[l].astype(_BF16), norm_mlp[l][None, :], w_up[l].astype(_BF16),
                        w_down[l].astype(_BF16))
    return x2.reshape(batch, seq, d)
```

```python
import functools

import jax
import jax.numpy as jnp
from jax import lax
from jax.experimental import pallas as pl
from jax.experimental.pallas import tpu as pltpu

D_MODEL = 1024
HEAD_DIM = 64
A_HEADS = 8
B_HEADS = 8
B_KV_HEADS = 2
B_GROUP = B_HEADS // B_KV_HEADS
N_ATTN_HEADS = A_HEADS + B_HEADS
MOBA_BLOCK = 256
MOBA_TOPK = 3
SWA_WINDOW = 128
D_FF = 4 * D_MODEL
EPS = 1e-6
NEG = -1e30
SCALE = HEAD_DIM ** -0.5

W_A = A_HEADS * HEAD_DIM
W_QB = B_HEADS * HEAD_DIM
W_KB = B_KV_HEADS * HEAD_DIM
W_KB_DUP = 2 * W_KB
W_V = W_A + W_KB
W_GATES = 2 * D_MODEL

C_QA = 0
C_KA = C_QA + W_A
C_QB = C_KA + W_A
C_KB = C_QB + W_QB
C_G = C_KB + W_KB_DUP
C_END = C_G + W_GATES

TOKEN_TILE = 512
FF_CHUNK = 1024
VMEM_LIMIT = 48 * 1024 * 1024

_NT = (((1,), (1,)), ((), ()))
_BF16 = jnp.bfloat16
_F32 = jnp.float32


def _const_spec(shape):
    return pl.BlockSpec(shape, lambda *_: (0,) * len(shape), pipeline_mode=pl.Buffered(1))


def _inproj_kernel(x_ref, gn_ref, w_ref, wvt_ref, bd_ref, gqa_ref, gka_ref, gqb_ref, gkb_ref,
                   qa_ref, ka_ref, qb_ref, kb_ref, g_ref, vat_ref, vbt_ref, km_ref):
    x = x_ref[...]
    ms = jnp.mean(x * x, axis=-1, keepdims=True)
    h = ((x * lax.rsqrt(ms + EPS)) * gn_ref[...]).astype(_BF16)

    def proj(lo, hi):
        return jnp.dot(h, w_ref[:, lo:hi], preferred_element_type=_F32)

    def head_norm(y, gain_ref):
        width = y.shape[-1]
        sq = (y * y).astype(_BF16)
        msq = jnp.dot(sq, bd_ref[:width, :width], preferred_element_type=_F32)
        return (y * lax.rsqrt(msq + EPS)) * gain_ref[...]

    qa_ref[...] = (head_norm(proj(C_QA, C_KA), gqa_ref) * SCALE).astype(_BF16)
    kn = head_norm(proj(C_KA, C_QB), gka_ref)
    ka_ref[...] = kn.astype(_BF16)
    nblk = kn.shape[0] // MOBA_BLOCK
    km_ref[0] = kn.reshape(nblk, MOBA_BLOCK, W_A).sum(axis=1) * (1.0 / MOBA_BLOCK)
    qb_ref[...] = (head_norm(proj(C_QB, C_KB), gqb_ref) * SCALE).astype(_BF16)
    kb_ref[...] = head_norm(proj(C_KB, C_G), gkb_ref).astype(_BF16)
    g_ref[...] = proj(C_G, C_END).astype(_BF16)

    vt = lax.dot_general(wvt_ref[...], h, _NT, preferred_element_type=_F32)
    for c in range(vat_ref.shape[0]):
        vat_ref[c] = vt[:W_A, c * MOBA_BLOCK:(c + 1) * MOBA_BLOCK].astype(_BF16)
    for c in range(vbt_ref.shape[0]):
        vbt_ref[c] = vt[W_A:, c * SWA_WINDOW:(c + 1) * SWA_WINDOW].astype(_BF16)


def _inproj(x2, gn, w_main, w_vt, bd, gqa, gka, gqb, gkb):
    n = x2.shape[0]
    tm = TOKEN_TILE
    row = lambda w: pl.BlockSpec((tm, w), lambda i: (i, 0))
    out_shape = (
        jax.ShapeDtypeStruct((n, W_A), _BF16),
        jax.ShapeDtypeStruct((n, W_A), _BF16),
        jax.ShapeDtypeStruct((n, W_QB), _BF16),
        jax.ShapeDtypeStruct((n, W_KB_DUP), _BF16),
        jax.ShapeDtypeStruct((n, W_GATES), _BF16),
        jax.ShapeDtypeStruct((n // MOBA_BLOCK, W_A, MOBA_BLOCK), _BF16),
        jax.ShapeDtypeStruct((n // SWA_WINDOW, W_KB, SWA_WINDOW), _BF16),
        jax.ShapeDtypeStruct((n // tm, tm // MOBA_BLOCK, W_A), _F32),
    )
    out_specs = (
        row(W_A), row(W_A), row(W_QB), row(W_KB_DUP), row(W_GATES),
        pl.BlockSpec((tm // MOBA_BLOCK, W_A, MOBA_BLOCK), lambda i: (i, 0, 0)),
        pl.BlockSpec((tm // SWA_WINDOW, W_KB, SWA_WINDOW), lambda i: (i, 0, 0)),
        pl.BlockSpec((1, tm // MOBA_BLOCK, W_A), lambda i: (i, 0, 0)),
    )
    in_specs = [row(D_MODEL), _const_spec(gn.shape), _const_spec(w_main.shape), _const_spec(w_vt.shape),
                _const_spec(bd.shape), _const_spec(gqa.shape), _const_spec(gka.shape),
                _const_spec(gqb.shape), _const_spec(gkb.shape)]
    return pl.pallas_call(
        _inproj_kernel, grid=(n // tm,), in_specs=in_specs, out_specs=out_specs, out_shape=out_shape,
        compiler_params=pltpu.CompilerParams(dimension_semantics=("parallel",),
                                             vmem_limit_bytes=VMEM_LIMIT),
        name="inproj",
    )(x2, gn, w_main, w_vt, bd, gqa, gka, gqb, gkb)


def _moba_kernel(slopes_ref, q_ref, k_ref, vt_ref, km_ref, o_ref, relb_sc, relo_sc, s_sc):
    hp = pl.program_id(1)
    blk = MOBA_BLOCK
    nb = q_ref.shape[0] // blk
    kp = lax.broadcasted_iota(jnp.int32, (blk, blk), 0)
    qp = lax.broadcasted_iota(jnp.int32, (blk, blk), 1)
    rel = (kp - qp).astype(_F32)
    lane = lax.broadcasted_iota(jnp.int32, (blk, 2 * HEAD_DIM), 1)
    ridx = lax.broadcasted_iota(jnp.int32, (nb, blk), 0)
    for e in range(2):
        rb = slopes_ref[2 * hp + e] * rel
        relb_sc[e] = rb
        relo_sc[e] = jnp.where(kp <= qp, rb, NEG)
    km = km_ref[0].astype(_BF16)

    for i in range(nb):
        q_t = q_ref[i * blk:(i + 1) * blk, :]
        outs = []
        for e in range(2):
            slope = slopes_ref[2 * hp + e]
            v_rows = slice(e * HEAD_DIM, (e + 1) * HEAD_DIM)
            in_head = (lane >= e * HEAD_DIM) & (lane < (e + 1) * HEAD_DIM)
            qm = jnp.where(in_head, q_t, jnp.zeros_like(q_t))
            gs = lax.dot_general(km, qm, _NT, preferred_element_type=_F32)
            radj = []
            for n in range(i):
                row = gs[n:n + 1, :]
                ahead = ((gs > row) | ((gs == row) & (ridx < n))) & (ridx < i)
                rank = jnp.sum(ahead.astype(_F32), axis=0, keepdims=True)
                radj.append(jnp.where(rank < MOBA_TOPK, 0.0, NEG) - slope * float(blk * (i - n)))

            m = None
            for n in range(i + 1):
                bias = relo_sc[e] if n == i else relb_sc[e]
                t = lax.dot_general(k_ref[n * blk:(n + 1) * blk, :], qm, _NT,
                                    preferred_element_type=_F32) + bias
                s_sc[e, n] = t
                bm = jnp.max(t, axis=0, keepdims=True)
                if n < i:
                    bm = bm + radj[n]
                m = bm if m is None else jnp.maximum(m, bm)

            l = None
            acc = None
            for n in range(i + 1):
                shift = m - radj[n] if n < i else m
                p = jnp.exp(s_sc[e, n] - shift)
                ps = jnp.sum(p, axis=0, keepdims=True)
                pv = jnp.dot(vt_ref[n, v_rows, :], p.astype(_BF16), preferred_element_type=_F32)
                l = ps if l is None else l + ps
                acc = pv if acc is None else acc + pv
            outs.append(acc / l)
        o = jnp.concatenate(outs, axis=0)
        o_ref[i * blk:(i + 1) * blk, :] = o.T.astype(o_ref.dtype)


def _moba(slopes, qa, ka, vat, km, batch, seq):
    nb = seq // MOBA_BLOCK
    pair = 2 * HEAD_DIM
    seq_spec = pl.BlockSpec((seq, pair), lambda b, hp: (b, hp))
    return pl.pallas_call(
        _moba_kernel, grid=(batch, A_HEADS // 2),
        in_specs=[pl.BlockSpec(memory_space=pltpu.SMEM), seq_spec, seq_spec,
                  pl.BlockSpec((nb, pair, MOBA_BLOCK), lambda b, hp: (b, hp, 0)),
                  pl.BlockSpec((1, nb, pair), lambda b, hp: (b, 0, hp))],
        out_specs=seq_spec,
        out_shape=jax.ShapeDtypeStruct(qa.shape, _BF16),
        scratch_shapes=[pltpu.VMEM((2, MOBA_BLOCK, MOBA_BLOCK), _F32),
                        pltpu.VMEM((2, MOBA_BLOCK, MOBA_BLOCK), _F32),
                        pltpu.VMEM((2, nb, MOBA_BLOCK, MOBA_BLOCK), _F32)],
        compiler_params=pltpu.CompilerParams(dimension_semantics=("parallel", "parallel"),
                                             vmem_limit_bytes=VMEM_LIMIT),
        name="moba",
    )(slopes, qa, ka, vat, km)


def _swa_kernel(slopes_ref, sinks_ref, q_ref, k_ref, vt_ref, o_ref, bias_sc):
    hk = pl.program_id(1)
    w = SWA_WINDOW
    nblk = q_ref.shape[0] // w
    kp = lax.broadcasted_iota(jnp.int32, (2 * w, 2 * w), 0)
    col = lax.broadcasted_iota(jnp.int32, (2 * w, 2 * w), 1)
    dist = (col & (w - 1)) + w - kp
    in_window = (dist >= 0) & (dist < w)
    first_head = col < w
    lane = lax.broadcasted_iota(jnp.int32, (w, 2 * HEAD_DIM), 1)
    col_row = lax.broadcasted_iota(jnp.int32, (1, 2 * w), 1)
    for pr in range(2):
        ha = hk * B_GROUP + 2 * pr
        slope = jnp.where(first_head, slopes_ref[ha], slopes_ref[ha + 1])
        bias_sc[pr] = jnp.where(in_window, -slope * dist.astype(_F32), NEG)

    for j in range(nblk):
        first = j == 0
        row0 = j * w
        for pr in range(2):
            ha = hk * B_GROUP + 2 * pr
            lanes = slice(pr * 2 * HEAD_DIM, (pr + 1) * 2 * HEAD_DIM)
            q_t = q_ref[row0:row0 + w, lanes]
            zero = jnp.zeros_like(q_t)
            qs = jnp.concatenate([jnp.where(lane < HEAD_DIM, q_t, zero),
                                  jnp.where(lane >= HEAD_DIM, q_t, zero)], axis=0)
            if first:
                keys = k_ref[0:w, :]
                bias = bias_sc[pr, w:2 * w, :]
            else:
                keys = k_ref[row0 - w:row0 + w, :]
                bias = bias_sc[pr]
            s = lax.dot_general(keys, qs, _NT, preferred_element_type=_F32) + bias
            sink = jnp.where(col_row < w, sinks_ref[ha], sinks_ref[ha + 1])
            m = jnp.maximum(jnp.max(s, axis=0, keepdims=True), sink)
            p = jnp.exp(s - m)
            den = jnp.sum(p, axis=0, keepdims=True) + jnp.exp(sink - m)
            pb = p.astype(_BF16)
            if first:
                ot = jnp.dot(vt_ref[0], pb, preferred_element_type=_F32)
            else:
                ot = (jnp.dot(vt_ref[j - 1], pb[:w], preferred_element_type=_F32)
                      + jnp.dot(vt_ref[j], pb[w:], preferred_element_type=_F32))
            ot = ot / den
            o2 = jnp.concatenate([ot[:, :w], ot[:, w:]], axis=0)
            o_ref[row0:row0 + w, lanes] = o2.T.astype(o_ref.dtype)


def _swa(slopes, sinks, qb, kb, vbt, batch, seq):
    nblk = seq // SWA_WINDOW
    grp = B_GROUP * HEAD_DIM
    q_spec = pl.BlockSpec((seq, grp), lambda b, hk: (b, hk))
    return pl.pallas_call(
        _swa_kernel, grid=(batch, B_KV_HEADS),
        in_specs=[pl.BlockSpec(memory_space=pltpu.SMEM), pl.BlockSpec(memory_space=pltpu.SMEM), q_spec,
                  pl.BlockSpec((seq, 2 * HEAD_DIM), lambda b, hk: (b, hk)),
                  pl.BlockSpec((nblk, HEAD_DIM, SWA_WINDOW), lambda b, hk: (b, hk, 0))],
        out_specs=q_spec,
        out_shape=jax.ShapeDtypeStruct(qb.shape, _BF16),
        scratch_shapes=[pltpu.VMEM((2, 2 * SWA_WINDOW, 2 * SWA_WINDOW), _F32)],
        compiler_params=pltpu.CompilerParams(dimension_semantics=("parallel", "parallel"),
                                             vmem_limit_bytes=VMEM_LIMIT),
        name="swa",
    )(slopes, sinks, qb, kb, vbt)


def _merge_mlp_kernel(x_ref, oa_ref, ob_ref, g_ref, wa_ref, wb_ref, wo_ref, gm_ref, wup_ref, wdn_ref, o_ref):
    a = jnp.dot(oa_ref[...], wa_ref[...], preferred_element_type=_F32)
    b = jnp.dot(ob_ref[...], wb_ref[...], preferred_element_type=_F32)
    ga = g_ref[:, :D_MODEL].astype(_F32)
    gb = g_ref[:, D_MODEL:].astype(_F32)
    mixed = jax.nn.sigmoid(ga) * a + jax.nn.sigmoid(gb) * b
    x1 = x_ref[...] + jnp.dot(mixed.astype(_BF16), wo_ref[...], preferred_element_type=_F32)
    ms = jnp.mean(x1 * x1, axis=-1, keepdims=True)
    h2 = ((x1 * lax.rsqrt(ms + EPS)) * gm_ref[...]).astype(_BF16)
    acc = x1
    for c in range(D_FF // FF_CHUNK):
        u = jnp.dot(h2, wup_ref[:, c * FF_CHUNK:(c + 1) * FF_CHUNK], preferred_element_type=_F32)
        u = jnp.square(jnp.maximum(u, 0.0)).astype(_BF16)
        acc = acc + jnp.dot(u, wdn_ref[c * FF_CHUNK:(c + 1) * FF_CHUNK, :], preferred_element_type=_F32)
    o_ref[...] = acc


def _merge_mlp(x2, oa, ob, g, wa, wb, wo, gm, wup, wdn):
    n = x2.shape[0]
    tm = TOKEN_TILE
    row = lambda w: pl.BlockSpec((tm, w), lambda i: (i, 0))
    return pl.pallas_call(
        _merge_mlp_kernel, grid=(n // tm,),
        in_specs=[row(D_MODEL), row(W_A), row(W_QB), row(W_GATES), _const_spec(wa.shape), _const_spec(wb.shape),
                  _const_spec(wo.shape), _const_spec(gm.shape), _const_spec(wup.shape), _const_spec(wdn.shape)],
        out_specs=row(D_MODEL),
        out_shape=jax.ShapeDtypeStruct(x2.shape, x2.dtype),
        compiler_params=pltpu.CompilerParams(dimension_semantics=("parallel",),
                                             vmem_limit_bytes=VMEM_LIMIT),
        name="merge_mlp",
    )(x2, oa, ob, g, wa, wb, wo, gm, wup, wdn)


def _alibi_slopes(n):
    return jnp.exp2(-(8.0 / n) * jnp.arange(1, n + 1, dtype=_F32))


def kernel(x, norm_attn, w_in, q_norm_a, k_norm_a, q_norm_b, k_norm_b, sinks_b, w_branch_a, w_branch_b, w_out,
           norm_mlp, w_up, w_down):
    batch, seq, d = x.shape
    assert d == D_MODEL and seq % TOKEN_TILE == 0 and TOKEN_TILE % MOBA_BLOCK == 0
    slopes = _alibi_slopes(N_ATTN_HEADS)
    slopes_b, slopes_a = slopes[:B_HEADS], slopes[B_HEADS:]
    head_of = jnp.arange(W_A) // HEAD_DIM
    bd = jnp.where(head_of[:, None] == head_of[None, :], 1.0 / HEAD_DIM, 0.0).astype(_BF16)

    x2 = x.reshape(batch * seq, d)
    for l in range(norm_attn.shape[0]):
        w = w_in[l]
        o = 0
        cols = []
        for width in (W_A, W_A, W_A, W_QB, W_KB, W_KB, W_GATES):
            cols.append(w[:, o:o + width])
            o += width
        w_qa, w_ka, w_va, w_qb, w_kb, w_vb, w_g = cols
        kb0, kb1 = w_kb[:, :HEAD_DIM], w_kb[:, HEAD_DIM:]
        w_main = jnp.concatenate([w_qa, w_ka, w_qb, kb0, kb0, kb1, kb1, w_g], axis=1).astype(_BF16)
        w_vt = jnp.concatenate([w_va, w_vb], axis=1).T.astype(_BF16)
        tile_gain = lambda g, reps: jnp.tile(g, reps)[None, :]
        qa, ka, qb, kb, g, vat, vbt, km = _inproj(
            x2, norm_attn[l][None, :], w_main, w_vt, bd,
            tile_gain(q_norm_a[l], A_HEADS), tile_gain(k_norm_a[l], A_HEADS),
            tile_gain(q_norm_b[l], B_HEADS), tile_gain(k_norm_b[l], 2 * B_KV_HEADS))
        km = km.reshape(batch, seq // MOBA_BLOCK, W_A)
        oa = _moba(slopes_a, qa, ka, vat, km, batch, seq)
        ob = _swa(slopes_b, sinks_b[l], qb, kb, vbt, batch, seq)
        x2 = _merge_mlp(x2, oa, ob, g, w_branch_a[l].astype(_BF16), w_branch_b[l].astype(_BF16),
                        w_out[l].astype(_BF16), norm_mlp[l][None, :], w_up[l].astype(_BF16),
                        w_down[l].astype(_BF16))
    return x2.reshape(batch, seq, d)
```

```python
import functools

import jax
import jax.numpy as jnp
from jax import lax
from jax.experimental import pallas as pl
from jax.experimental.pallas import tpu as pltpu

D_MODEL = 1024
HEAD_DIM = 64
A_HEADS = 8
B_HEADS = 8
B_KV_HEADS = 2
B_GROUP = B_HEADS // B_KV_HEADS
N_ATTN_HEADS = A_HEADS + B_HEADS
MOBA_BLOCK = 256
MOBA_TOPK = 3
SWA_WINDOW = 128
D_FF = 4 * D_MODEL
EPS = 1e-6
NEG = -1e30
SCALE = HEAD_DIM ** -0.5

W_A = A_HEADS * HEAD_DIM
W_QB = B_HEADS * HEAD_DIM
W_KB = B_KV_HEADS * HEAD_DIM
W_KB_DUP = 2 * W_KB
W_V = W_A + W_KB
W_GATES = 2 * D_MODEL

C_QA = 0
C_KA = C_QA + W_A
C_QB = C_KA + W_A
C_KB = C_QB + W_QB
C_G = C_KB + W_KB_DUP
C_END = C_G + W_GATES

TOKEN_TILE = 512
FF_CHUNK = 1024
ONES_ROWS = 16
VMEM_LIMIT = 48 * 1024 * 1024

_NT = (((1,), (1,)), ((), ()))
_BF16 = jnp.bfloat16
_F32 = jnp.float32


def _const_spec(shape):
    return pl.BlockSpec(shape, lambda *_: (0,) * len(shape), pipeline_mode=pl.Buffered(1))


def _inproj_kernel(x_ref, gn_ref, w_ref, wvt_ref, bd_ref, gqa_ref, gka_ref, gqb_ref, gkb_ref,
                   qa_ref, ka_ref, qb_ref, kb_ref, g_ref, vat_ref, vbt_ref, km_ref):
    x = x_ref[...]
    ms = jnp.mean(x * x, axis=-1, keepdims=True)
    h = ((x * lax.rsqrt(ms + EPS)) * gn_ref[...]).astype(_BF16)

    def proj(lo, hi):
        return jnp.dot(h, w_ref[:, lo:hi], preferred_element_type=_F32)

    def head_norm(y, gain_ref):
        width = y.shape[-1]
        sq = (y * y).astype(_BF16)
        msq = jnp.dot(sq, bd_ref[:width, :width], preferred_element_type=_F32)
        return (y * lax.rsqrt(msq + EPS)) * gain_ref[...]

    qa_ref[...] = (head_norm(proj(C_QA, C_KA), gqa_ref) * SCALE).astype(_BF16)
    kn = head_norm(proj(C_KA, C_QB), gka_ref)
    ka_ref[...] = kn.astype(_BF16)
    nblk = kn.shape[0] // MOBA_BLOCK
    km_ref[0] = kn.reshape(nblk, MOBA_BLOCK, W_A).sum(axis=1) * (1.0 / MOBA_BLOCK)
    qb_ref[...] = (head_norm(proj(C_QB, C_KB), gqb_ref) * SCALE).astype(_BF16)
    kb_ref[...] = head_norm(proj(C_KB, C_G), gkb_ref).astype(_BF16)
    g_ref[...] = proj(C_G, C_END).astype(_BF16)

    vt = lax.dot_general(wvt_ref[...], h, _NT, preferred_element_type=_F32)
    for c in range(vat_ref.shape[0]):
        vat_ref[c] = vt[:W_A, c * MOBA_BLOCK:(c + 1) * MOBA_BLOCK].astype(_BF16)
    for c in range(vbt_ref.shape[0]):
        vbt_ref[c] = vt[W_A:, c * SWA_WINDOW:(c + 1) * SWA_WINDOW].astype(_BF16)


def _inproj(x2, gn, w_main, w_vt, bd, gqa, gka, gqb, gkb):
    n = x2.shape[0]
    tm = TOKEN_TILE
    row = lambda w: pl.BlockSpec((tm, w), lambda i: (i, 0))
    out_shape = (
        jax.ShapeDtypeStruct((n, W_A), _BF16),
        jax.ShapeDtypeStruct((n, W_A), _BF16),
        jax.ShapeDtypeStruct((n, W_QB), _BF16),
        jax.ShapeDtypeStruct((n, W_KB_DUP), _BF16),
        jax.ShapeDtypeStruct((n, W_GATES), _BF16),
        jax.ShapeDtypeStruct((n // MOBA_BLOCK, W_A, MOBA_BLOCK), _BF16),
        jax.ShapeDtypeStruct((n // SWA_WINDOW, W_KB, SWA_WINDOW), _BF16),
        jax.ShapeDtypeStruct((n // tm, tm // MOBA_BLOCK, W_A), _F32),
    )
    out_specs = (
        row(W_A), row(W_A), row(W_QB), row(W_KB_DUP), row(W_GATES),
        pl.BlockSpec((tm // MOBA_BLOCK, W_A, MOBA_BLOCK), lambda i: (i, 0, 0)),
        pl.BlockSpec((tm // SWA_WINDOW, W_KB, SWA_WINDOW), lambda i: (i, 0, 0)),
        pl.BlockSpec((1, tm // MOBA_BLOCK, W_A), lambda i: (i, 0, 0)),
    )
    in_specs = [row(D_MODEL), _const_spec(gn.shape), _const_spec(w_main.shape), _const_spec(w_vt.shape),
                _const_spec(bd.shape), _const_spec(gqa.shape), _const_spec(gka.shape),
                _const_spec(gqb.shape), _const_spec(gkb.shape)]
    return pl.pallas_call(
        _inproj_kernel, grid=(n // tm,), in_specs=in_specs, out_specs=out_specs, out_shape=out_shape,
        compiler_params=pltpu.CompilerParams(dimension_semantics=("parallel",),
                                             vmem_limit_bytes=VMEM_LIMIT),
        name="inproj",
    )(x2, gn, w_main, w_vt, bd, gqa, gka, gqb, gkb)


def _moba_kernel(slopes_ref, q_ref, k_ref, vt_ref, km_ref, o_ref, kaug_sc, vaug_sc, causal_sc, s_sc):
    hp = pl.program_id(1)
    blk = MOBA_BLOCK
    nb = q_ref.shape[0] // blk
    pair = 2 * HEAD_DIM
    kp = lax.broadcasted_iota(jnp.int32, (blk, blk), 0)
    qp = lax.broadcasted_iota(jnp.int32, (blk, blk), 1)
    causal_sc[...] = jnp.where(kp <= qp, 0.0, NEG)
    lane = lax.broadcasted_iota(jnp.int32, (blk, pair), 1)
    prow = lax.broadcasted_iota(jnp.int32, (blk, pair), 0).astype(_F32)
    ridx = lax.broadcasted_iota(jnp.int32, (nb, blk), 0)
    km_lane = lax.broadcasted_iota(jnp.int32, (nb, pair), 1)
    in_head, q_aug, km_head = [], [], []
    for e in range(2):
        head = (lane >= e * HEAD_DIM) & (lane < (e + 1) * HEAD_DIM)
        a = (1 - e) * HEAD_DIM
        sv = jnp.full((blk, pair), slopes_ref[2 * hp + e], _F32)
        hi = sv.astype(_BF16).astype(_F32)
        mid = (sv - hi).astype(_BF16).astype(_F32)
        lo = sv - hi - mid
        pieces = jnp.where(lane == a, hi, jnp.where(lane == a + 1, mid, jnp.where(lane == a + 2, lo, 0.0)))
        q_aug.append(pieces.astype(_BF16))
        k_aug = jnp.where((lane >= a) & (lane < a + 3), prow, 0.0).astype(_BF16)
        for n in range(nb):
            rows = slice(n * blk, (n + 1) * blk)
            kaug_sc[e, rows, :] = jnp.where(head, k_ref[rows, :], k_aug)
            vaug_sc[e, n, :HEAD_DIM, :] = vt_ref[n, e * HEAD_DIM:(e + 1) * HEAD_DIM, :]
            vaug_sc[e, n, HEAD_DIM:, :] = jnp.ones((vaug_sc.shape[2] - HEAD_DIM, blk), _BF16)
        in_head.append(head)
        km_e = (km_lane >= e * HEAD_DIM) & (km_lane < (e + 1) * HEAD_DIM)
        km_head.append(jnp.where(km_e, km_ref[0], 0.0).astype(_BF16))

    def scores(i, e):
        slope = slopes_ref[2 * hp + e]
        qm = jnp.where(in_head[e], q_ref[i * blk:(i + 1) * blk, :], q_aug[e])
        gs = lax.dot_general(km_head[e], qm, _NT, preferred_element_type=_F32)
        radj = []
        for n in range(i):
            row = gs[n:n + 1, :]
            ahead = ((gs > row) | ((gs == row) & (ridx < n))) & (ridx < i)
            rank = jnp.sum(ahead.astype(_F32), axis=0, keepdims=True)
            radj.append(jnp.where(rank < MOBA_TOPK, 0.0, NEG) - slope * float(blk * (i - n)))
        m = None
        for n in range(i + 1):
            t = lax.dot_general(kaug_sc[e, n * blk:(n + 1) * blk, :], qm, _NT, preferred_element_type=_F32)
            if n == i:
                t = t + causal_sc[...]
            s_sc[e, n] = t
            bm = jnp.max(t, axis=0, keepdims=True)
            if n < i:
                bm = bm + radj[n]
            m = bm if m is None else jnp.maximum(m, bm)
        return [m - radj[n] if n < i else m for n in range(i + 1)]

    def weighted_values(i, e, shifts):
        acc = None
        for n in range(i + 1):
            p = jnp.exp(s_sc[e, n] - shifts[n])
            pv = jnp.dot(vaug_sc[e, n], p.astype(_BF16), preferred_element_type=_F32)
            acc = pv if acc is None else acc + pv
        return acc[:HEAD_DIM] / acc[HEAD_DIM:HEAD_DIM + 1]

    units = [(i, e) for i in range(nb) for e in range(2)]
    shifts = scores(*units[0])
    outs = []
    for u, (i, e) in enumerate(units):
        next_shifts = scores(*units[u + 1]) if u + 1 < len(units) else None
        outs.append(weighted_values(i, e, shifts))
        shifts = next_shifts
        if e == 1:
            o = jnp.concatenate(outs, axis=0)
            o_ref[i * blk:(i + 1) * blk, :] = o.T.astype(o_ref.dtype)
            outs = []


def _moba(slopes, qa, ka, vat, km, batch, seq):
    nb = seq // MOBA_BLOCK
    pair = 2 * HEAD_DIM
    seq_spec = pl.BlockSpec((seq, pair), lambda b, hp: (b, hp))
    return pl.pallas_call(
        _moba_kernel, grid=(batch, A_HEADS // 2),
        in_specs=[pl.BlockSpec(memory_space=pltpu.SMEM), seq_spec, seq_spec,
                  pl.BlockSpec((nb, pair, MOBA_BLOCK), lambda b, hp: (b, hp, 0)),
                  pl.BlockSpec((1, nb, pair), lambda b, hp: (b, 0, hp))],
        out_specs=seq_spec,
        out_shape=jax.ShapeDtypeStruct(qa.shape, _BF16),
        scratch_shapes=[pltpu.VMEM((2, seq, pair), _BF16),
                        pltpu.VMEM((2, nb, HEAD_DIM + ONES_ROWS, MOBA_BLOCK), _BF16),
                        pltpu.VMEM((MOBA_BLOCK, MOBA_BLOCK), _F32),
                        pltpu.VMEM((2, nb, MOBA_BLOCK, MOBA_BLOCK), _F32)],
        compiler_params=pltpu.CompilerParams(dimension_semantics=("parallel", "parallel"),
                                             vmem_limit_bytes=VMEM_LIMIT),
        name="moba",
    )(slopes, qa, ka, vat, km)


def _swa_kernel(slopes_ref, sinks_ref, q_ref, k_ref, vt_ref, o_ref, bias_sc, vaug_sc, s_sc):
    hk = pl.program_id(1)
    w = SWA_WINDOW
    nblk = q_ref.shape[0] // w
    kp = lax.broadcasted_iota(jnp.int32, (2 * w, 2 * w), 0)
    col = lax.broadcasted_iota(jnp.int32, (2 * w, 2 * w), 1)
    dist = (col & (w - 1)) + w - kp
    in_window = (dist >= 0) & (dist < w)
    first_head = col < w
    lane = lax.broadcasted_iota(jnp.int32, (w, 2 * HEAD_DIM), 1)
    col_row = lax.broadcasted_iota(jnp.int32, (1, 2 * w), 1)
    for pr in range(2):
        ha = hk * B_GROUP + 2 * pr
        slope = jnp.where(first_head, slopes_ref[ha], slopes_ref[ha + 1])
        bias_sc[pr] = jnp.where(in_window, -slope * dist.astype(_F32), NEG)

    for j in range(nblk):
        vaug_sc[j, :HEAD_DIM, :] = vt_ref[j]
        vaug_sc[j, HEAD_DIM:, :] = jnp.ones((vaug_sc.shape[1] - HEAD_DIM, w), _BF16)

    def scores(j, pr, slot):
        ha = hk * B_GROUP + 2 * pr
        q_t = q_ref[j * w:(j + 1) * w, pr * 2 * HEAD_DIM:(pr + 1) * 2 * HEAD_DIM]
        zero = jnp.zeros_like(q_t)
        qs = jnp.concatenate([jnp.where(lane < HEAD_DIM, q_t, zero),
                              jnp.where(lane >= HEAD_DIM, q_t, zero)], axis=0)
        k0 = max(j - 1, 0) * w
        nk = (j + 1) * w - k0
        s = lax.dot_general(k_ref[k0:k0 + nk, :], qs, _NT, preferred_element_type=_F32)
        s = s + bias_sc[pr, 2 * w - nk:, :]
        s_sc[slot, :nk, :] = s
        sink = jnp.where(col_row < w, sinks_ref[ha], sinks_ref[ha + 1])
        return jnp.maximum(jnp.max(s, axis=0, keepdims=True), sink), sink

    def weighted_values(j, pr, slot, m, sink):
        nk = min(j + 1, 2) * w
        pb = jnp.exp(s_sc[slot, :nk, :] - m).astype(_BF16)
        ot = jnp.dot(vaug_sc[j], pb[nk - w:], preferred_element_type=_F32)
        if j > 0:
            ot = ot + jnp.dot(vaug_sc[j - 1], pb[:w], preferred_element_type=_F32)
        den = ot[HEAD_DIM:HEAD_DIM + 1] + jnp.exp(sink - m)
        ot = ot[:HEAD_DIM] / den
        o2 = jnp.concatenate([ot[:, :w], ot[:, w:]], axis=0)
        o_ref[j * w:(j + 1) * w, pr * 2 * HEAD_DIM:(pr + 1) * 2 * HEAD_DIM] = o2.T.astype(o_ref.dtype)

    tiles = [(j, pr) for j in range(nblk) for pr in range(2)]
    stats = scores(*tiles[0], 0)
    for t, (j, pr) in enumerate(tiles):
        next_stats = scores(*tiles[t + 1], (t + 1) % 2) if t + 1 < len(tiles) else None
        weighted_values(j, pr, t % 2, *stats)
        stats = next_stats


def _swa(slopes, sinks, qb, kb, vbt, batch, seq):
    nblk = seq // SWA_WINDOW
    grp = B_GROUP * HEAD_DIM
    q_spec = pl.BlockSpec((seq, grp), lambda b, hk: (b, hk))
    return pl.pallas_call(
        _swa_kernel, grid=(batch, B_KV_HEADS),
        in_specs=[pl.BlockSpec(memory_space=pltpu.SMEM), pl.BlockSpec(memory_space=pltpu.SMEM), q_spec,
                  pl.BlockSpec((seq, 2 * HEAD_DIM), lambda b, hk: (b, hk)),
                  pl.BlockSpec((nblk, HEAD_DIM, SWA_WINDOW), lambda b, hk: (b, hk, 0))],
        out_specs=q_spec,
        out_shape=jax.ShapeDtypeStruct(qb.shape, _BF16),
        scratch_shapes=[pltpu.VMEM((2, 2 * SWA_WINDOW, 2 * SWA_WINDOW), _F32),
                        pltpu.VMEM((nblk, HEAD_DIM + ONES_ROWS, SWA_WINDOW), _BF16),
                        pltpu.VMEM((2, 2 * SWA_WINDOW, 2 * SWA_WINDOW), _F32)],
        compiler_params=pltpu.CompilerParams(dimension_semantics=("parallel", "parallel"),
                                             vmem_limit_bytes=VMEM_LIMIT),
        name="swa",
    )(slopes, sinks, qb, kb, vbt)


def _merge_mlp_kernel(x_ref, oa_ref, ob_ref, g_ref, wa_ref, wb_ref, wo_ref, gm_ref, wup_ref, wdn_ref, o_ref):
    a = jnp.dot(oa_ref[...], wa_ref[...], preferred_element_type=_F32)
    b = jnp.dot(ob_ref[...], wb_ref[...], preferred_element_type=_F32)
    ga = g_ref[:, :D_MODEL].astype(_F32)
    gb = g_ref[:, D_MODEL:].astype(_F32)
    mixed = jax.nn.sigmoid(ga) * a + jax.nn.sigmoid(gb) * b
    x1 = x_ref[...] + jnp.dot(mixed.astype(_BF16), wo_ref[...], preferred_element_type=_F32)
    ms = jnp.mean(x1 * x1, axis=-1, keepdims=True)
    h2 = ((x1 * lax.rsqrt(ms + EPS)) * gm_ref[...]).astype(_BF16)
    acc = x1
    for c in range(D_FF // FF_CHUNK):
        u = jnp.dot(h2, wup_ref[:, c * FF_CHUNK:(c + 1) * FF_CHUNK], preferred_element_type=_F32)
        u = jnp.square(jnp.maximum(u, 0.0)).astype(_BF16)
        acc = acc + jnp.dot(u, wdn_ref[c * FF_CHUNK:(c + 1) * FF_CHUNK, :], preferred_element_type=_F32)
    o_ref[...] = acc


def _merge_mlp(x2, oa, ob, g, wa, wb, wo, gm, wup, wdn):
    n = x2.shape[0]
    tm = TOKEN_TILE
    row = lambda w: pl.BlockSpec((tm, w), lambda i: (i, 0))
    return pl.pallas_call(
        _merge_mlp_kernel, grid=(n // tm,),
        in_specs=[row(D_MODEL), row(W_A), row(W_QB), row(W_GATES), _const_spec(wa.shape), _const_spec(wb.shape),
                  _const_spec(wo.shape), _const_spec(gm.shape), _const_spec(wup.shape), _const_spec(wdn.shape)],
        out_specs=row(D_MODEL),
        out_shape=jax.ShapeDtypeStruct(x2.shape, x2.dtype),
        compiler_params=pltpu.CompilerParams(dimension_semantics=("parallel",),
                                             vmem_limit_bytes=VMEM_LIMIT),
        name="merge_mlp",
    )(x2, oa, ob, g, wa, wb, wo, gm, wup, wdn)


def _alibi_slopes(n):
    return jnp.exp2(-(8.0 / n) * jnp.arange(1, n + 1, dtype=_F32))


def kernel(x, norm_attn, w_in, q_norm_a, k_norm_a, q_norm_b, k_norm_b, sinks_b, w_branch_a, w_branch_b, w_out,
           norm_mlp, w_up, w_down):
    batch, seq, d = x.shape
    assert d == D_MODEL and seq % TOKEN_TILE == 0 and TOKEN_TILE % MOBA_BLOCK == 0
    slopes = _alibi_slopes(N_ATTN_HEADS)
    slopes_b, slopes_a = slopes[:B_HEADS], slopes[B_HEADS:]
    head_of = jnp.arange(W_A) // HEAD_DIM
    bd = jnp.where(head_of[:, None] == head_of[None, :], 1.0 / HEAD_DIM, 0.0).astype(_BF16)

    x2 = x.reshape(batch * seq, d)
    for l in range(norm_attn.shape[0]):
        w = w_in[l]
        o = 0
        cols = []
        for width in (W_A, W_A, W_A, W_QB, W_KB, W_KB, W_GATES):
            cols.append(w[:, o:o + width])
            o += width
        w_qa, w_ka, w_va, w_qb, w_kb, w_vb, w_g = cols
        kb0, kb1 = w_kb[:, :HEAD_DIM], w_kb[:, HEAD_DIM:]
        w_main = jnp.concatenate([w_qa, w_ka, w_qb, kb0, kb0, kb1, kb1, w_g], axis=1).astype(_BF16)
        w_vt = jnp.concatenate([w_va, w_vb], axis=1).T.astype(_BF16)
        tile_gain = lambda g, reps: jnp.tile(g, reps)[None, :]
        qa, ka, qb, kb, g, vat, vbt, km = _inproj(
            x2, norm_attn[l][None, :], w_main, w_vt, bd,
            tile_gain(q_norm_a[l], A_HEADS), tile_gain(k_norm_a[l], A_HEADS),
            tile_gain(q_norm_b[l], B_HEADS), tile_gain(k_norm_b[l], 2 * B_KV_HEADS))
        km = km.reshape(batch, seq // MOBA_BLOCK, W_A)
        oa = _moba(slopes_a, qa, ka, vat, km, batch, seq)
        ob = _swa(slopes_b, sinks_b[l], qb, kb, vbt, batch, seq)
        x2 = _merge_mlp(x2, oa, ob, g, w_branch_a[l].astype(_BF16), w_branch_b[l].astype(_BF16),
                        w_out[l].astype(_BF16), norm_mlp[l][None, :], w_up[l].astype(_BF16),
                        w_down[l].astype(_BF16))
    return x2.reshape(batch, seq, d)
```

```python
import functools

import jax
import jax.numpy as jnp
from jax import lax
from jax.experimental import pallas as pl
from jax.experimental.pallas import tpu as pltpu

D_MODEL = 1024
HEAD_DIM = 64
A_HEADS = 8
B_HEADS = 8
B_KV_HEADS = 2
B_GROUP = B_HEADS // B_KV_HEADS
N_ATTN_HEADS = A_HEADS + B_HEADS
MOBA_BLOCK = 256
MOBA_TOPK = 3
SWA_WINDOW = 128
D_FF = 4 * D_MODEL
EPS = 1e-6
NEG = -1e30
SCALE = HEAD_DIM ** -0.5
LOG2E = 1.4426950408889634

W_A = A_HEADS * HEAD_DIM
W_QB = B_HEADS * HEAD_DIM
W_KB = B_KV_HEADS * HEAD_DIM
W_KB_DUP = 2 * W_KB
W_V = W_A + W_KB
W_GATES = 2 * D_MODEL

C_QA = 0
C_KA = C_QA + W_A
C_QB = C_KA + W_A
C_KB = C_QB + W_QB
C_G = C_KB + W_KB_DUP
C_END = C_G + W_GATES

TOKEN_TILE = 512
FF_CHUNK = 1024
ONES_ROWS = 16
MOBA_LOOKAHEAD = 3
SWA_LOOKAHEAD = 5
VMEM_LIMIT = 48 * 1024 * 1024

_NT = (((1,), (1,)), ((), ()))
_BF16 = jnp.bfloat16
_F32 = jnp.float32


def _const_spec(shape):
    return pl.BlockSpec(shape, lambda *_: (0,) * len(shape), pipeline_mode=pl.Buffered(1))


def _inproj_kernel(x_ref, gn_ref, w_ref, wvt_ref, bd_ref, gqa_ref, gka_ref, gqb_ref, gkb_ref,
                   qa_ref, ka_ref, qb_ref, kb_ref, g_ref, vat_ref, vbt_ref, km_ref):
    x = x_ref[...]
    ms = jnp.mean(x * x, axis=-1, keepdims=True)
    h = ((x * lax.rsqrt(ms + EPS)) * gn_ref[...]).astype(_BF16)

    def proj(lo, hi):
        return jnp.dot(h, w_ref[:, lo:hi], preferred_element_type=_F32)

    def head_norm(y, gain_ref):
        width = y.shape[-1]
        sq = (y * y).astype(_BF16)
        msq = jnp.dot(sq, bd_ref[:width, :width], preferred_element_type=_F32)
        return (y * lax.rsqrt(msq + EPS)) * gain_ref[...]

    qa_ref[...] = (head_norm(proj(C_QA, C_KA), gqa_ref) * (SCALE * LOG2E)).astype(_BF16)
    kn = head_norm(proj(C_KA, C_QB), gka_ref)
    ka_ref[...] = kn.astype(_BF16)
    nblk = kn.shape[0] // MOBA_BLOCK
    km_ref[0] = kn.reshape(nblk, MOBA_BLOCK, W_A).sum(axis=1) * (1.0 / MOBA_BLOCK)
    qb_ref[...] = (head_norm(proj(C_QB, C_KB), gqb_ref) * (SCALE * LOG2E)).astype(_BF16)
    kb_ref[...] = head_norm(proj(C_KB, C_G), gkb_ref).astype(_BF16)
    g_ref[...] = proj(C_G, C_END).astype(_BF16)

    vt = lax.dot_general(wvt_ref[...], h, _NT, preferred_element_type=_F32)
    for c in range(vat_ref.shape[0]):
        vat_ref[c] = vt[:W_A, c * MOBA_BLOCK:(c + 1) * MOBA_BLOCK].astype(_BF16)
    for c in range(vbt_ref.shape[0]):
        vbt_ref[c] = vt[W_A:, c * SWA_WINDOW:(c + 1) * SWA_WINDOW].astype(_BF16)


def _inproj(x2, gn, w_main, w_vt, bd, gqa, gka, gqb, gkb):
    n = x2.shape[0]
    tm = TOKEN_TILE
    row = lambda w: pl.BlockSpec((tm, w), lambda i: (i, 0))
    out_shape = (
        jax.ShapeDtypeStruct((n, W_A), _BF16),
        jax.ShapeDtypeStruct((n, W_A), _BF16),
        jax.ShapeDtypeStruct((n, W_QB), _BF16),
        jax.ShapeDtypeStruct((n, W_KB_DUP), _BF16),
        jax.ShapeDtypeStruct((n, W_GATES), _BF16),
        jax.ShapeDtypeStruct((n // MOBA_BLOCK, W_A, MOBA_BLOCK), _BF16),
        jax.ShapeDtypeStruct((n // SWA_WINDOW, W_KB, SWA_WINDOW), _BF16),
        jax.ShapeDtypeStruct((n // tm, tm // MOBA_BLOCK, W_A), _F32),
    )
    out_specs = (
        row(W_A), row(W_A), row(W_QB), row(W_KB_DUP), row(W_GATES),
        pl.BlockSpec((tm // MOBA_BLOCK, W_A, MOBA_BLOCK), lambda i: (i, 0, 0)),
        pl.BlockSpec((tm // SWA_WINDOW, W_KB, SWA_WINDOW), lambda i: (i, 0, 0)),
        pl.BlockSpec((1, tm // MOBA_BLOCK, W_A), lambda i: (i, 0, 0)),
    )
    in_specs = [row(D_MODEL), _const_spec(gn.shape), _const_spec(w_main.shape), _const_spec(w_vt.shape),
                _const_spec(bd.shape), _const_spec(gqa.shape), _const_spec(gka.shape),
                _const_spec(gqb.shape), _const_spec(gkb.shape)]
    return pl.pallas_call(
        _inproj_kernel, grid=(n // tm,), in_specs=in_specs, out_specs=out_specs, out_shape=out_shape,
        compiler_params=pltpu.CompilerParams(dimension_semantics=("parallel",),
                                             vmem_limit_bytes=VMEM_LIMIT),
        name="inproj",
    )(x2, gn, w_main, w_vt, bd, gqa, gka, gqb, gkb)


def _moba_kernel(slopes_ref, q_ref, k_ref, vt_ref, km_ref, o_ref, kaug_sc, vaug_sc, causal_sc, s_sc):
    hp = pl.program_id(1)
    blk = MOBA_BLOCK
    nb = q_ref.shape[0] // blk
    pair = 2 * HEAD_DIM
    kp = lax.broadcasted_iota(jnp.int32, (blk, blk), 0)
    qp = lax.broadcasted_iota(jnp.int32, (blk, blk), 1)
    causal_sc[...] = jnp.where(kp <= qp, 0.0, NEG)
    lane = lax.broadcasted_iota(jnp.int32, (blk, pair), 1)
    prow = lax.broadcasted_iota(jnp.int32, (blk, pair), 0).astype(_F32)
    ridx = lax.broadcasted_iota(jnp.int32, (nb, blk), 0)
    km_lane = lax.broadcasted_iota(jnp.int32, (nb, pair), 1)
    in_head, q_aug, km_head = [], [], []
    for e in range(2):
        head = (lane >= e * HEAD_DIM) & (lane < (e + 1) * HEAD_DIM)
        a = (1 - e) * HEAD_DIM
        sv = jnp.full((blk, pair), slopes_ref[2 * hp + e], _F32)
        hi = sv.astype(_BF16).astype(_F32)
        mid = (sv - hi).astype(_BF16).astype(_F32)
        lo = sv - hi - mid
        pieces = jnp.where(lane == a, hi, jnp.where(lane == a + 1, mid, jnp.where(lane == a + 2, lo, 0.0)))
        q_aug.append(pieces.astype(_BF16))
        k_aug = jnp.where((lane >= a) & (lane < a + 3), prow, 0.0).astype(_BF16)
        for n in range(nb):
            rows = slice(n * blk, (n + 1) * blk)
            kaug_sc[e, rows, :] = jnp.where(head, k_ref[rows, :], k_aug)
            vaug_sc[e, n, :HEAD_DIM, :] = vt_ref[n, e * HEAD_DIM:(e + 1) * HEAD_DIM, :]
            vaug_sc[e, n, HEAD_DIM:, :] = jnp.ones((vaug_sc.shape[2] - HEAD_DIM, blk), _BF16)
        in_head.append(head)
        km_e = (km_lane >= e * HEAD_DIM) & (km_lane < (e + 1) * HEAD_DIM)
        km_head.append(jnp.where(km_e, km_ref[0], 0.0).astype(_BF16))

    def scores(i, e, slot):
        slope = slopes_ref[2 * hp + e]
        qm = jnp.where(in_head[e], q_ref[i * blk:(i + 1) * blk, :], q_aug[e])
        gs = lax.dot_general(km_head[e], qm, _NT, preferred_element_type=_F32)
        radj = []
        for n in range(i):
            row = gs[n:n + 1, :]
            ahead = ((gs > row) | ((gs == row) & (ridx < n))) & (ridx < i)
            rank = jnp.sum(ahead.astype(_F32), axis=0, keepdims=True)
            radj.append(jnp.where(rank < MOBA_TOPK, 0.0, NEG) - slope * float(blk * (i - n)))
        m = None
        for n in range(i + 1):
            t = lax.dot_general(kaug_sc[e, n * blk:(n + 1) * blk, :], qm, _NT, preferred_element_type=_F32)
            if n == i:
                t = t + causal_sc[...]
            s_sc[slot, n] = t
            bm = jnp.max(t, axis=0, keepdims=True)
            if n < i:
                bm = bm + radj[n]
            m = bm if m is None else jnp.maximum(m, bm)
        return [m - radj[n] if n < i else m for n in range(i + 1)]

    def weighted_values(i, e, slot, shifts):
        acc = None
        for n in range(i + 1):
            p = jnp.exp2(s_sc[slot, n] - shifts[n])
            pv = jnp.dot(vaug_sc[e, n], p.astype(_BF16), preferred_element_type=_F32)
            acc = pv if acc is None else acc + pv
        return acc[:HEAD_DIM] / acc[HEAD_DIM:HEAD_DIM + 1]

    units = [(i, e, u % s_sc.shape[0]) for u, (i, e) in enumerate((i, e) for i in range(nb) for e in range(2))]
    shifts = {u: scores(*units[u]) for u in range(MOBA_LOOKAHEAD)}
    outs = []
    for u, (i, e, slot) in enumerate(units):
        ahead = u + MOBA_LOOKAHEAD
        if ahead < len(units):
            shifts[ahead] = scores(*units[ahead])
        outs.append(weighted_values(i, e, slot, shifts.pop(u)))
        if e == 1:
            o = jnp.concatenate(outs, axis=0)
            o_ref[i * blk:(i + 1) * blk, :] = o.T.astype(o_ref.dtype)
            outs = []


def _moba(slopes, qa, ka, vat, km, batch, seq):
    nb = seq // MOBA_BLOCK
    pair = 2 * HEAD_DIM
    seq_spec = pl.BlockSpec((seq, pair), lambda b, hp: (b, hp))
    return pl.pallas_call(
        _moba_kernel, grid=(batch, A_HEADS // 2),
        in_specs=[pl.BlockSpec(memory_space=pltpu.SMEM), seq_spec, seq_spec,
                  pl.BlockSpec((nb, pair, MOBA_BLOCK), lambda b, hp: (b, hp, 0)),
                  pl.BlockSpec((1, nb, pair), lambda b, hp: (b, 0, hp))],
        out_specs=seq_spec,
        out_shape=jax.ShapeDtypeStruct(qa.shape, _BF16),
        scratch_shapes=[pltpu.VMEM((2, seq, pair), _BF16),
                        pltpu.VMEM((2, nb, HEAD_DIM + ONES_ROWS, MOBA_BLOCK), _BF16),
                        pltpu.VMEM((MOBA_BLOCK, MOBA_BLOCK), _F32),
                        pltpu.VMEM((MOBA_LOOKAHEAD + 1, nb, MOBA_BLOCK, MOBA_BLOCK), _F32)],
        compiler_params=pltpu.CompilerParams(dimension_semantics=("parallel", "parallel"),
                                             vmem_limit_bytes=VMEM_LIMIT),
        name="moba",
    )(slopes, qa, ka, vat, km)


def _swa_kernel(slopes_ref, sinks_ref, q_ref, k_ref, vt_ref, o_ref, bias_sc, vaug_sc, s_sc):
    hk = pl.program_id(1)
    w = SWA_WINDOW
    nblk = q_ref.shape[0] // w
    kp = lax.broadcasted_iota(jnp.int32, (2 * w, 2 * w), 0)
    col = lax.broadcasted_iota(jnp.int32, (2 * w, 2 * w), 1)
    dist = (col & (w - 1)) + w - kp
    in_window = (dist >= 0) & (dist < w)
    first_head = col < w
    lane = lax.broadcasted_iota(jnp.int32, (w, 2 * HEAD_DIM), 1)
    col_row = lax.broadcasted_iota(jnp.int32, (1, 2 * w), 1)
    for pr in range(2):
        ha = hk * B_GROUP + 2 * pr
        slope = jnp.where(first_head, slopes_ref[ha], slopes_ref[ha + 1])
        bias_sc[pr] = jnp.where(in_window, -slope * dist.astype(_F32), NEG)

    for j in range(nblk):
        vaug_sc[j, :HEAD_DIM, :] = vt_ref[j]
        vaug_sc[j, HEAD_DIM:, :] = jnp.ones((vaug_sc.shape[1] - HEAD_DIM, w), _BF16)

    def scores(j, pr, slot):
        ha = hk * B_GROUP + 2 * pr
        q_t = q_ref[j * w:(j + 1) * w, pr * 2 * HEAD_DIM:(pr + 1) * 2 * HEAD_DIM]
        zero = jnp.zeros_like(q_t)
        qs = jnp.concatenate([jnp.where(lane < HEAD_DIM, q_t, zero),
                              jnp.where(lane >= HEAD_DIM, q_t, zero)], axis=0)
        k0 = max(j - 1, 0) * w
        nk = (j + 1) * w - k0
        s = lax.dot_general(k_ref[k0:k0 + nk, :], qs, _NT, preferred_element_type=_F32)
        s = s + bias_sc[pr, 2 * w - nk:, :]
        s_sc[slot, :nk, :] = s
        sink = jnp.where(col_row < w, sinks_ref[ha], sinks_ref[ha + 1])
        return jnp.maximum(jnp.max(s, axis=0, keepdims=True), sink), sink

    def weighted_values(j, pr, slot, m, sink):
        nk = min(j + 1, 2) * w
        pb = jnp.exp2(s_sc[slot, :nk, :] - m).astype(_BF16)
        ot = jnp.dot(vaug_sc[j], pb[nk - w:], preferred_element_type=_F32)
        if j > 0:
            ot = ot + jnp.dot(vaug_sc[j - 1], pb[:w], preferred_element_type=_F32)
        den = ot[HEAD_DIM:HEAD_DIM + 1] + jnp.exp2(sink - m)
        ot = ot[:HEAD_DIM] / den
        o2 = jnp.concatenate([ot[:, :w], ot[:, w:]], axis=0)
        o_ref[j * w:(j + 1) * w, pr * 2 * HEAD_DIM:(pr + 1) * 2 * HEAD_DIM] = o2.T.astype(o_ref.dtype)

    tiles = [(j, pr) for j in range(nblk) for pr in range(2)]
    nslot = s_sc.shape[0]
    stats = {t: scores(*tiles[t], t % nslot) for t in range(SWA_LOOKAHEAD)}
    for t, (j, pr) in enumerate(tiles):
        ahead = t + SWA_LOOKAHEAD
        if ahead < len(tiles):
            stats[ahead] = scores(*tiles[ahead], ahead % nslot)
        weighted_values(j, pr, t % nslot, *stats.pop(t))


def _swa(slopes, sinks, qb, kb, vbt, batch, seq):
    nblk = seq // SWA_WINDOW
    grp = B_GROUP * HEAD_DIM
    q_spec = pl.BlockSpec((seq, grp), lambda b, hk: (b, hk))
    return pl.pallas_call(
        _swa_kernel, grid=(batch, B_KV_HEADS),
        in_specs=[pl.BlockSpec(memory_space=pltpu.SMEM), pl.BlockSpec(memory_space=pltpu.SMEM), q_spec,
                  pl.BlockSpec((seq, 2 * HEAD_DIM), lambda b, hk: (b, hk)),
                  pl.BlockSpec((nblk, HEAD_DIM, SWA_WINDOW), lambda b, hk: (b, hk, 0))],
        out_specs=q_spec,
        out_shape=jax.ShapeDtypeStruct(qb.shape, _BF16),
        scratch_shapes=[pltpu.VMEM((2, 2 * SWA_WINDOW, 2 * SWA_WINDOW), _F32),
                        pltpu.VMEM((nblk, HEAD_DIM + ONES_ROWS, SWA_WINDOW), _BF16),
                        pltpu.VMEM((SWA_LOOKAHEAD + 1, 2 * SWA_WINDOW, 2 * SWA_WINDOW), _F32)],
        compiler_params=pltpu.CompilerParams(dimension_semantics=("parallel", "parallel"),
                                             vmem_limit_bytes=VMEM_LIMIT),
        name="swa",
    )(slopes, sinks, qb, kb, vbt)


def _merge_mlp_kernel(x_ref, oa_ref, ob_ref, g_ref, wa_ref, wb_ref, wo_ref, gm_ref, wup_ref, wdn_ref, o_ref):
    a = jnp.dot(oa_ref[...], wa_ref[...], preferred_element_type=_F32)
    b = jnp.dot(ob_ref[...], wb_ref[...], preferred_element_type=_F32)
    ga = g_ref[:, :D_MODEL].astype(_F32)
    gb = g_ref[:, D_MODEL:].astype(_F32)
    mixed = jax.nn.sigmoid(ga) * a + jax.nn.sigmoid(gb) * b
    x1 = x_ref[...] + jnp.dot(mixed.astype(_BF16), wo_ref[...], preferred_element_type=_F32)
    ms = jnp.mean(x1 * x1, axis=-1, keepdims=True)
    h2 = ((x1 * lax.rsqrt(ms + EPS)) * gm_ref[...]).astype(_BF16)
    acc = x1
    for c in range(D_FF // FF_CHUNK):
        u = jnp.dot(h2, wup_ref[:, c * FF_CHUNK:(c + 1) * FF_CHUNK], preferred_element_type=_F32)
        u = jnp.square(jnp.maximum(u, 0.0)).astype(_BF16)
        acc = acc + jnp.dot(u, wdn_ref[c * FF_CHUNK:(c + 1) * FF_CHUNK, :], preferred_element_type=_F32)
    o_ref[...] = acc


def _merge_mlp(x2, oa, ob, g, wa, wb, wo, gm, wup, wdn):
    n = x2.shape[0]
    tm = TOKEN_TILE
    row = lambda w: pl.BlockSpec((tm, w), lambda i: (i, 0))
    return pl.pallas_call(
        _merge_mlp_kernel, grid=(n // tm,),
        in_specs=[row(D_MODEL), row(W_A), row(W_QB), row(W_GATES), _const_spec(wa.shape), _const_spec(wb.shape),
                  _const_spec(wo.shape), _const_spec(gm.shape), _const_spec(wup.shape), _const_spec(wdn.shape)],
        out_specs=row(D_MODEL),
        out_shape=jax.ShapeDtypeStruct(x2.shape, x2.dtype),
        compiler_params=pltpu.CompilerParams(dimension_semantics=("parallel",),
                                             vmem_limit_bytes=VMEM_LIMIT),
        name="merge_mlp",
    )(x2, oa, ob, g, wa, wb, wo, gm, wup, wdn)


def _alibi_slopes(n):
    return jnp.exp2(-(8.0 / n) * jnp.arange(1, n + 1, dtype=_F32))


def kernel(x, norm_attn, w_in, q_norm_a, k_norm_a, q_norm_b, k_norm_b, sinks_b, w_branch_a, w_branch_b, w_out,
           norm_mlp, w_up, w_down):
    batch, seq, d = x.shape
    assert d == D_MODEL and seq % TOKEN_TILE == 0 and TOKEN_TILE % MOBA_BLOCK == 0
    slopes = _alibi_slopes(N_ATTN_HEADS)
    slopes_b, slopes_a = slopes[:B_HEADS], slopes[B_HEADS:]
    head_of = jnp.arange(W_A) // HEAD_DIM
    bd = jnp.where(head_of[:, None] == head_of[None, :], 1.0 / HEAD_DIM, 0.0).astype(_BF16)

    x2 = x.reshape(batch * seq, d)
    for l in range(norm_attn.shape[0]):
        w = w_in[l]
        o = 0
        cols = []
        for width in (W_A, W_A, W_A, W_QB, W_KB, W_KB, W_GATES):
            cols.append(w[:, o:o + width])
            o += width
        w_qa, w_ka, w_va, w_qb, w_kb, w_vb, w_g = cols
        kb0, kb1 = w_kb[:, :HEAD_DIM], w_kb[:, HEAD_DIM:]
        w_main = jnp.concatenate([w_qa, w_ka, w_qb, kb0, kb0, kb1, kb1, w_g], axis=1).astype(_BF16)
        w_vt = jnp.concatenate([w_va, w_vb], axis=1).T.astype(_BF16)
        tile_gain = lambda g, reps: jnp.tile(g, reps)[None, :]
        qa, ka, qb, kb, g, vat, vbt, km = _inproj(
            x2, norm_attn[l][None, :], w_main, w_vt, bd,
            tile_gain(q_norm_a[l], A_HEADS), tile_gain(k_norm_a[l], A_HEADS),
            tile_gain(q_norm_b[l], B_HEADS), tile_gain(k_norm_b[l], 2 * B_KV_HEADS))
        km = km.reshape(batch, seq // MOBA_BLOCK, W_A)
        oa = _moba(slopes_a * LOG2E, qa, ka, vat, km, batch, seq)
        ob = _swa(slopes_b * LOG2E, sinks_b[l] * LOG2E, qb, kb, vbt, batch, seq)
        x2 = _merge_mlp(x2, oa, ob, g, w_branch_a[l].astype(_BF16), w_branch_b[l].astype(_BF16),
                        w_out[l].astype(_BF16), norm_mlp[l][None, :], w_up[l].astype(_BF16),
                        w_down[l].astype(_BF16))
    return x2.reshape(batch, seq, d)
```

```python
import functools

import jax
import jax.numpy as jnp
from jax import lax
from jax.experimental import pallas as pl
from jax.experimental.pallas import tpu as pltpu

D_MODEL = 1024
HEAD_DIM = 64
A_HEADS = 8
B_HEADS = 8
B_KV_HEADS = 2
B_GROUP = B_HEADS // B_KV_HEADS
N_ATTN_HEADS = A_HEADS + B_HEADS
MOBA_BLOCK = 256
MOBA_TOPK = 3
SWA_WINDOW = 128
D_FF = 4 * D_MODEL
EPS = 1e-6
NEG = -1e30
SCALE = HEAD_DIM ** -0.5
LOG2E = 1.4426950408889634

W_A = A_HEADS * HEAD_DIM
W_QB = B_HEADS * HEAD_DIM
W_KB = B_KV_HEADS * HEAD_DIM
W_KB_DUP = 2 * W_KB
W_GATES = 2 * D_MODEL

C_QA = 0
C_KA = C_QA + W_A
C_VA = C_KA + W_A
C_QB = C_VA + W_A
C_KB = C_QB + W_QB
C_VB = C_KB + W_KB
C_G = C_VB + W_KB
C_END = C_G + W_GATES

TOKEN_TILE = 512
FF_CHUNK = 1024
ONES_ROWS = 16
MOBA_LOOKAHEAD = 3
SWA_LOOKAHEAD = 5
VMEM_LIMIT = 48 * 1024 * 1024
MXU_TILE = 256

_NT = (((1,), (1,)), ((), ()))
_BF16 = jnp.bfloat16
_F32 = jnp.float32


def _const_spec(shape):
    return pl.BlockSpec(shape, lambda *_: (0,) * len(shape), pipeline_mode=pl.Buffered(1))


def _inproj_kernel(x_ref, gn_ref, w_ref, bd_ref, gqa_ref, gka_ref, gqb_ref, gkb_ref,
                   qa_ref, ka_ref, qb_ref, kb_ref, g_ref, vat_ref, vbt_ref, km_ref):
    x = x_ref[...]
    ms = jnp.mean(x * x, axis=-1, keepdims=True)
    h = ((x * lax.rsqrt(ms + EPS)) * gn_ref[...]).astype(_BF16)

    def proj(lo, hi):
        return jnp.dot(h, w_ref[:, lo:hi], preferred_element_type=_F32)

    def head_norm(y, gain_ref):
        sq = (y * y).astype(_BF16)
        step = min(MXU_TILE, y.shape[-1])
        msq = jnp.concatenate(
            [jnp.dot(sq[:, c:c + step], bd_ref[:step, :step], preferred_element_type=_F32)
             for c in range(0, y.shape[-1], step)], axis=1)
        return (y * lax.rsqrt(msq + EPS)) * gain_ref[...]

    qa_ref[...] = (head_norm(proj(C_QA, C_KA), gqa_ref) * (SCALE * LOG2E)).astype(_BF16)
    kn = head_norm(proj(C_KA, C_VA), gka_ref)
    ka_ref[...] = kn.astype(_BF16)
    nblk = kn.shape[0] // MOBA_BLOCK
    km_ref[0] = kn.reshape(nblk, MOBA_BLOCK, W_A).sum(axis=1) * (1.0 / MOBA_BLOCK)
    qb_ref[...] = (head_norm(proj(C_QB, C_KB), gqb_ref) * (SCALE * LOG2E)).astype(_BF16)
    g_ref[...] = proj(C_G, C_END).astype(_BF16)

    kv = proj(C_KB, C_G)
    kb = head_norm(kv[:, :W_KB], gkb_ref)
    swapped = pltpu.roll(kb, HEAD_DIM, 1)
    lane = lax.broadcasted_iota(jnp.int32, kb.shape, 1)
    first = lane < HEAD_DIM
    kb_ref[...] = jnp.concatenate([jnp.where(first, kb, swapped), jnp.where(first, swapped, kb)],
                                  axis=1).astype(_BF16)

    vat = proj(C_VA, C_QB).T
    for c in range(vat_ref.shape[0]):
        vat_ref[c] = vat[:, c * MOBA_BLOCK:(c + 1) * MOBA_BLOCK].astype(_BF16)
    vbt = kv[:, W_KB:].T
    for c in range(vbt_ref.shape[0]):
        vbt_ref[c] = vbt[:, c * SWA_WINDOW:(c + 1) * SWA_WINDOW].astype(_BF16)


def _inproj(x2, gn, w_in, bd, gqa, gka, gqb, gkb):
    n = x2.shape[0]
    tm = TOKEN_TILE
    row = lambda w: pl.BlockSpec((tm, w), lambda i: (i, 0))
    out_shape = (
        jax.ShapeDtypeStruct((n, W_A), _BF16),
        jax.ShapeDtypeStruct((n, W_A), _BF16),
        jax.ShapeDtypeStruct((n, W_QB), _BF16),
        jax.ShapeDtypeStruct((n, W_KB_DUP), _BF16),
        jax.ShapeDtypeStruct((n, W_GATES), _BF16),
        jax.ShapeDtypeStruct((n // MOBA_BLOCK, W_A, MOBA_BLOCK), _BF16),
        jax.ShapeDtypeStruct((n // SWA_WINDOW, W_KB, SWA_WINDOW), _BF16),
        jax.ShapeDtypeStruct((n // tm, tm // MOBA_BLOCK, W_A), _F32),
    )
    out_specs = (
        row(W_A), row(W_A), row(W_QB), row(W_KB_DUP), row(W_GATES),
        pl.BlockSpec((tm // MOBA_BLOCK, W_A, MOBA_BLOCK), lambda i: (i, 0, 0)),
        pl.BlockSpec((tm // SWA_WINDOW, W_KB, SWA_WINDOW), lambda i: (i, 0, 0)),
        pl.BlockSpec((1, tm // MOBA_BLOCK, W_A), lambda i: (i, 0, 0)),
    )
    in_specs = [row(D_MODEL), _const_spec(gn.shape), _const_spec(w_in.shape),
                _const_spec(bd.shape), _const_spec(gqa.shape), _const_spec(gka.shape),
                _const_spec(gqb.shape), _const_spec(gkb.shape)]
    return pl.pallas_call(
        _inproj_kernel, grid=(n // tm,), in_specs=in_specs, out_specs=out_specs, out_shape=out_shape,
        compiler_params=pltpu.CompilerParams(dimension_semantics=("parallel",),
                                             vmem_limit_bytes=VMEM_LIMIT),
        name="inproj",
    )(x2, gn, w_in, bd, gqa, gka, gqb, gkb)


def _moba_kernel(slopes_ref, q_ref, k_ref, vt_ref, km_ref, o_ref, kaug_sc, vaug_sc, causal_sc, s_sc):
    hp = pl.program_id(1)
    blk = MOBA_BLOCK
    nb = q_ref.shape[0] // blk
    pair = 2 * HEAD_DIM
    kp = lax.broadcasted_iota(jnp.int32, (blk, blk), 0)
    qp = lax.broadcasted_iota(jnp.int32, (blk, blk), 1)
    causal_sc[...] = jnp.where(kp <= qp, 0.0, NEG)
    lane = lax.broadcasted_iota(jnp.int32, (blk, pair), 1)
    prow = lax.broadcasted_iota(jnp.int32, (blk, pair), 0).astype(_F32)
    ridx = lax.broadcasted_iota(jnp.int32, (nb, blk), 0)
    km_lane = lax.broadcasted_iota(jnp.int32, (nb, pair), 1)
    in_head, q_aug, km_head = [], [], []
    for e in range(2):
        head = (lane >= e * HEAD_DIM) & (lane < (e + 1) * HEAD_DIM)
        a = (1 - e) * HEAD_DIM
        sv = jnp.full((blk, pair), slopes_ref[2 * hp + e], _F32)
        hi = sv.astype(_BF16).astype(_F32)
        mid = (sv - hi).astype(_BF16).astype(_F32)
        lo = sv - hi - mid
        pieces = jnp.where(lane == a, hi, jnp.where(lane == a + 1, mid, jnp.where(lane == a + 2, lo, 0.0)))
        q_aug.append(pieces.astype(_BF16))
        k_aug = jnp.where((lane >= a) & (lane < a + 3), prow, 0.0).astype(_BF16)
        for n in range(nb):
            rows = slice(n * blk, (n + 1) * blk)
            kaug_sc[e, rows, :] = jnp.where(head, k_ref[rows, :], k_aug)
            vaug_sc[e, n, :HEAD_DIM, :] = vt_ref[n, e * HEAD_DIM:(e + 1) * HEAD_DIM, :]
            vaug_sc[e, n, HEAD_DIM:, :] = jnp.ones((vaug_sc.shape[2] - HEAD_DIM, blk), _BF16)
        in_head.append(head)
        km_e = (km_lane >= e * HEAD_DIM) & (km_lane < (e + 1) * HEAD_DIM)
        km_head.append(jnp.where(km_e, km_ref[0], 0.0).astype(_BF16))

    def scores(i, e, slot):
        slope = slopes_ref[2 * hp + e]
        qm = jnp.where(in_head[e], q_ref[i * blk:(i + 1) * blk, :], q_aug[e])
        gs = lax.dot_general(km_head[e], qm, _NT, preferred_element_type=_F32)
        radj = []
        for n in range(i):
            row = gs[n:n + 1, :]
            ahead = ((gs > row) | ((gs == row) & (ridx < n))) & (ridx < i)
            rank = jnp.sum(ahead.astype(_F32), axis=0, keepdims=True)
            radj.append(jnp.where(rank < MOBA_TOPK, 0.0, NEG) - slope * float(blk * (i - n)))
        m = None
        for n in range(i + 1):
            t = lax.dot_general(kaug_sc[e, n * blk:(n + 1) * blk, :], qm, _NT, preferred_element_type=_F32)
            if n == i:
                t = t + causal_sc[...]
            s_sc[slot, n] = t
            bm = jnp.max(t, axis=0, keepdims=True)
            if n < i:
                bm = bm + radj[n]
            m = bm if m is None else jnp.maximum(m, bm)
        return [m - radj[n] if n < i else m for n in range(i + 1)]

    def weighted_values(i, e, slot, shifts):
        acc = None
        for n in range(i + 1):
            p = jnp.exp2(s_sc[slot, n] - shifts[n])
            pv = jnp.dot(vaug_sc[e, n], p.astype(_BF16), preferred_element_type=_F32)
            acc = pv if acc is None else acc + pv
        return acc[:HEAD_DIM] / acc[HEAD_DIM:HEAD_DIM + 1]

    units = [(i, e, u % s_sc.shape[0]) for u, (i, e) in enumerate((i, e) for i in range(nb) for e in range(2))]
    shifts = {u: scores(*units[u]) for u in range(MOBA_LOOKAHEAD)}
    outs = []
    for u, (i, e, slot) in enumerate(units):
        ahead = u + MOBA_LOOKAHEAD
        if ahead < len(units):
            shifts[ahead] = scores(*units[ahead])
        outs.append(weighted_values(i, e, slot, shifts.pop(u)))
        if e == 1:
            o = jnp.concatenate(outs, axis=0)
            o_ref[i * blk:(i + 1) * blk, :] = o.T.astype(o_ref.dtype)
            outs = []


def _moba(slopes, qa, ka, vat, km, batch, seq):
    nb = seq // MOBA_BLOCK
    pair = 2 * HEAD_DIM
    seq_spec = pl.BlockSpec((seq, pair), lambda b, hp: (b, hp))
    return pl.pallas_call(
        _moba_kernel, grid=(batch, A_HEADS // 2),
        in_specs=[pl.BlockSpec(memory_space=pltpu.SMEM), seq_spec, seq_spec,
                  pl.BlockSpec((nb, pair, MOBA_BLOCK), lambda b, hp: (b, hp, 0)),
                  pl.BlockSpec((1, nb, pair), lambda b, hp: (b, 0, hp))],
        out_specs=seq_spec,
        out_shape=jax.ShapeDtypeStruct(qa.shape, _BF16),
        scratch_shapes=[pltpu.VMEM((2, seq, pair), _BF16),
                        pltpu.VMEM((2, nb, HEAD_DIM + ONES_ROWS, MOBA_BLOCK), _BF16),
                        pltpu.VMEM((MOBA_BLOCK, MOBA_BLOCK), _F32),
                        pltpu.VMEM((MOBA_LOOKAHEAD + 1, nb, MOBA_BLOCK, MOBA_BLOCK), _F32)],
        compiler_params=pltpu.CompilerParams(dimension_semantics=("parallel", "parallel"),
                                             vmem_limit_bytes=VMEM_LIMIT),
        name="moba",
    )(slopes, qa, ka, vat, km)


def _swa_kernel(slopes_ref, sinks_ref, q_ref, k_ref, vt_ref, o_ref, bias_sc, vaug_sc, s_sc):
    hk = pl.program_id(1)
    w = SWA_WINDOW
    nblk = q_ref.shape[0] // w
    kp = lax.broadcasted_iota(jnp.int32, (2 * w, 2 * w), 0)
    col = lax.broadcasted_iota(jnp.int32, (2 * w, 2 * w), 1)
    dist = (col & (w - 1)) + w - kp
    in_window = (dist >= 0) & (dist < w)
    first_head = col < w
    lane = lax.broadcasted_iota(jnp.int32, (w, 2 * HEAD_DIM), 1)
    col_row = lax.broadcasted_iota(jnp.int32, (1, 2 * w), 1)
    for pr in range(2):
        ha = hk * B_GROUP + 2 * pr
        slope = jnp.where(first_head, slopes_ref[ha], slopes_ref[ha + 1])
        bias_sc[pr] = jnp.where(in_window, -slope * dist.astype(_F32), NEG)

    for j in range(nblk):
        vaug_sc[j, :HEAD_DIM, :] = vt_ref[j]
        vaug_sc[j, HEAD_DIM:, :] = jnp.ones((vaug_sc.shape[1] - HEAD_DIM, w), _BF16)

    def scores(j, pr, slot):
        ha = hk * B_GROUP + 2 * pr
        q_t = q_ref[j * w:(j + 1) * w, pr * 2 * HEAD_DIM:(pr + 1) * 2 * HEAD_DIM]
        zero = jnp.zeros_like(q_t)
        qs = jnp.concatenate([jnp.where(lane < HEAD_DIM, q_t, zero),
                              jnp.where(lane >= HEAD_DIM, q_t, zero)], axis=0)
        k0 = max(j - 1, 0) * w
        nk = (j + 1) * w - k0
        s = lax.dot_general(k_ref[k0:k0 + nk, :], qs, _NT, preferred_element_type=_F32)
        s = s + bias_sc[pr, 2 * w - nk:, :]
        s_sc[slot, :nk, :] = s
        sink = jnp.where(col_row < w, sinks_ref[ha], sinks_ref[ha + 1])
        return jnp.maximum(jnp.max(s, axis=0, keepdims=True), sink), sink

    def weighted_values(j, pr, slot, m, sink):
        nk = min(j + 1, 2) * w
        pb = jnp.exp2(s_sc[slot, :nk, :] - m).astype(_BF16)
        ot = jnp.dot(vaug_sc[j], pb[nk - w:], preferred_element_type=_F32)
        if j > 0:
            ot = ot + jnp.dot(vaug_sc[j - 1], pb[:w], preferred_element_type=_F32)
        den = ot[HEAD_DIM:HEAD_DIM + 1] + jnp.exp2(sink - m)
        ot = ot[:HEAD_DIM] / den
        o2 = jnp.concatenate([ot[:, :w], ot[:, w:]], axis=0)
        o_ref[j * w:(j + 1) * w, pr * 2 * HEAD_DIM:(pr + 1) * 2 * HEAD_DIM] = o2.T.astype(o_ref.dtype)

    tiles = [(j, pr) for j in range(nblk) for pr in range(2)]
    nslot = s_sc.shape[0]
    stats = {t: scores(*tiles[t], t % nslot) for t in range(SWA_LOOKAHEAD)}
    for t, (j, pr) in enumerate(tiles):
        ahead = t + SWA_LOOKAHEAD
        if ahead < len(tiles):
            stats[ahead] = scores(*tiles[ahead], ahead % nslot)
        weighted_values(j, pr, t % nslot, *stats.pop(t))


def _swa(slopes, sinks, qb, kb, vbt, batch, seq):
    nblk = seq // SWA_WINDOW
    grp = B_GROUP * HEAD_DIM
    q_spec = pl.BlockSpec((seq, grp), lambda b, hk: (b, hk))
    return pl.pallas_call(
        _swa_kernel, grid=(batch, B_KV_HEADS),
        in_specs=[pl.BlockSpec(memory_space=pltpu.SMEM), pl.BlockSpec(memory_space=pltpu.SMEM), q_spec,
                  pl.BlockSpec((seq, 2 * HEAD_DIM), lambda b, hk: (b, hk)),
                  pl.BlockSpec((nblk, HEAD_DIM, SWA_WINDOW), lambda b, hk: (b, hk, 0))],
        out_specs=q_spec,
        out_shape=jax.ShapeDtypeStruct(qb.shape, _BF16),
        scratch_shapes=[pltpu.VMEM((2, 2 * SWA_WINDOW, 2 * SWA_WINDOW), _F32),
                        pltpu.VMEM((nblk, HEAD_DIM + ONES_ROWS, SWA_WINDOW), _BF16),
                        pltpu.VMEM((SWA_LOOKAHEAD + 1, 2 * SWA_WINDOW, 2 * SWA_WINDOW), _F32)],
        compiler_params=pltpu.CompilerParams(dimension_semantics=("parallel", "parallel"),
                                             vmem_limit_bytes=VMEM_LIMIT),
        name="swa",
    )(slopes, sinks, qb, kb, vbt)


def _merge_mlp_kernel(x_ref, oa_ref, ob_ref, g_ref, wa_ref, wb_ref, wo_ref, gm_ref, wup_ref, wdn_ref, o_ref):
    a = jnp.dot(oa_ref[...], wa_ref[...], preferred_element_type=_F32)
    b = jnp.dot(ob_ref[...], wb_ref[...], preferred_element_type=_F32)
    ga = g_ref[:, :D_MODEL].astype(_F32)
    gb = g_ref[:, D_MODEL:].astype(_F32)
    mixed = jax.nn.sigmoid(ga) * a + jax.nn.sigmoid(gb) * b
    x1 = x_ref[...] + jnp.dot(mixed.astype(_BF16), wo_ref[...], preferred_element_type=_F32)
    ms = jnp.mean(x1 * x1, axis=-1, keepdims=True)
    h2 = ((x1 * lax.rsqrt(ms + EPS)) * gm_ref[...]).astype(_BF16)
    acc = x1
    for c in range(D_FF // FF_CHUNK):
        u = jnp.dot(h2, wup_ref[:, c * FF_CHUNK:(c + 1) * FF_CHUNK], preferred_element_type=_F32)
        u = jnp.square(jnp.maximum(u, 0.0)).astype(_BF16)
        acc = acc + jnp.dot(u, wdn_ref[c * FF_CHUNK:(c + 1) * FF_CHUNK, :], preferred_element_type=_F32)
    o_ref[...] = acc


def _merge_mlp(x2, oa, ob, g, wa, wb, wo, gm, wup, wdn):
    n = x2.shape[0]
    tm = TOKEN_TILE
    row = lambda w: pl.BlockSpec((tm, w), lambda i: (i, 0))
    return pl.pallas_call(
        _merge_mlp_kernel, grid=(n // tm,),
        in_specs=[row(D_MODEL), row(W_A), row(W_QB), row(W_GATES), _const_spec(wa.shape), _const_spec(wb.shape),
                  _const_spec(wo.shape), _const_spec(gm.shape), _const_spec(wup.shape), _const_spec(wdn.shape)],
        out_specs=row(D_MODEL),
        out_shape=jax.ShapeDtypeStruct(x2.shape, x2.dtype),
        compiler_params=pltpu.CompilerParams(dimension_semantics=("parallel",),
                                             vmem_limit_bytes=VMEM_LIMIT),
        name="merge_mlp",
    )(x2, oa, ob, g, wa, wb, wo, gm, wup, wdn)


def _alibi_slopes(n):
    return jnp.exp2(-(8.0 / n) * jnp.arange(1, n + 1, dtype=_F32))


def kernel(x, norm_attn, w_in, q_norm_a, k_norm_a, q_norm_b, k_norm_b, sinks_b, w_branch_a, w_branch_b, w_out,
           norm_mlp, w_up, w_down):
    batch, seq, d = x.shape
    assert d == D_MODEL and seq % TOKEN_TILE == 0 and TOKEN_TILE % MOBA_BLOCK == 0
    slopes = _alibi_slopes(N_ATTN_HEADS)
    slopes_b, slopes_a = slopes[:B_HEADS], slopes[B_HEADS:]
    head_of = jnp.arange(MXU_TILE) // HEAD_DIM
    bd = jnp.where(head_of[:, None] == head_of[None, :], 1.0 / HEAD_DIM, 0.0).astype(_BF16)

    x2 = x.reshape(batch * seq, d)
    for l in range(norm_attn.shape[0]):
        tile_gain = lambda g, reps: jnp.tile(g, reps)[None, :]
        qa, ka, qb, kb, g, vat, vbt, km = _inproj(
            x2, norm_attn[l][None, :], w_in[l].astype(_BF16), bd,
            tile_gain(q_norm_a[l], A_HEADS), tile_gain(k_norm_a[l], A_HEADS),
            tile_gain(q_norm_b[l], B_HEADS), tile_gain(k_norm_b[l], B_KV_HEADS))
        km = km.reshape(batch, seq // MOBA_BLOCK, W_A)
        oa = _moba(slopes_a * LOG2E, qa, ka, vat, km, batch, seq)
        ob = _swa(slopes_b * LOG2E, sinks_b[l] * LOG2E, qb, kb, vbt, batch, seq)
        x2 = _merge_mlp(x2, oa, ob, g, w_branch_a[l].astype(_BF16), w_branch_b[l].astype(_BF16),
                        w_out[l].astype(_BF16), norm_mlp[l][None, :], w_up[l].astype(_BF16),
                        w_down[l].astype(_BF16))
    return x2.reshape(batch, seq, d)
```

```python
import functools

import jax
import jax.numpy as jnp
from jax import lax
from jax.experimental import pallas as pl
from jax.experimental.pallas import tpu as pltpu

D_MODEL = 1024
HEAD_DIM = 64
A_HEADS = 8
B_HEADS = 8
B_KV_HEADS = 2
B_GROUP = B_HEADS // B_KV_HEADS
N_ATTN_HEADS = A_HEADS + B_HEADS
MOBA_BLOCK = 256
MOBA_TOPK = 3
SWA_WINDOW = 128
D_FF = 4 * D_MODEL
EPS = 1e-6
NEG = -1e30
SCALE = HEAD_DIM ** -0.5
LOG2E = 1.4426950408889634

W_A = A_HEADS * HEAD_DIM
W_QB = B_HEADS * HEAD_DIM
W_KB = B_KV_HEADS * HEAD_DIM
W_KB_DUP = 2 * W_KB
W_GATES = 2 * D_MODEL

C_QA = 0
C_KA = C_QA + W_A
C_VA = C_KA + W_A
C_QB = C_VA + W_A
C_KB = C_QB + W_QB
C_VB = C_KB + W_KB
C_G = C_VB + W_KB
C_END = C_G + W_GATES

TOKEN_TILE = 512
FF_CHUNK = 1024
ONES_ROWS = 16
MOBA_LOOKAHEAD = 3
SWA_LOOKAHEAD = 5
VMEM_LIMIT = 48 * 1024 * 1024
MXU_TILE = 256

_NT = (((1,), (1,)), ((), ()))
_BF16 = jnp.bfloat16
_F32 = jnp.float32


def _const_spec(shape):
    return pl.BlockSpec(shape, lambda *_: (0,) * len(shape), pipeline_mode=pl.Buffered(1))


def _dynamic_zero():
    return jnp.minimum(pl.program_id(0), 0)


def _inproj_kernel(x_ref, gn_ref, w_ref, bd_ref, gqa_ref, gka_ref, gqb_ref, gkb_ref,
                   qa_ref, ka_ref, qb_ref, kb_ref, g_ref, vat_ref, vbt_ref, km_ref):
    x = x_ref[...]
    ms = jnp.mean(x * x, axis=-1, keepdims=True)
    h = ((x * lax.rsqrt(ms + EPS)) * gn_ref[...]).astype(_BF16)

    def proj(lo, hi):
        return jnp.dot(h, w_ref[:, lo:hi], preferred_element_type=_F32)

    def head_norm(y, gain_ref):
        sq = (y * y).astype(_BF16)
        step = min(MXU_TILE, y.shape[-1])
        msq = jnp.concatenate(
            [jnp.dot(sq[:, c:c + step], bd_ref[:step, :step], preferred_element_type=_F32)
             for c in range(0, y.shape[-1], step)], axis=1)
        return (y * lax.rsqrt(msq + EPS)) * gain_ref[...]

    qa_ref[...] = (head_norm(proj(C_QA, C_KA), gqa_ref) * (SCALE * LOG2E)).astype(_BF16)
    kn = head_norm(proj(C_KA, C_VA), gka_ref)
    ka_ref[...] = kn.astype(_BF16)
    nblk = kn.shape[0] // MOBA_BLOCK
    km_ref[0] = kn.reshape(nblk, MOBA_BLOCK, W_A).sum(axis=1) * (1.0 / MOBA_BLOCK)
    qb_ref[...] = (head_norm(proj(C_QB, C_KB), gqb_ref) * (SCALE * LOG2E)).astype(_BF16)
    g_ref[...] = proj(C_G, C_END).astype(_BF16)

    kv = proj(C_KB, C_G)
    kb = head_norm(kv[:, :W_KB], gkb_ref)
    swapped = pltpu.roll(kb, HEAD_DIM, 1)
    lane = lax.broadcasted_iota(jnp.int32, kb.shape, 1)
    first = lane < HEAD_DIM
    kb_ref[...] = jnp.concatenate([jnp.where(first, kb, swapped), jnp.where(first, swapped, kb)],
                                  axis=1).astype(_BF16)

    vat = proj(C_VA, C_QB).T
    for c in range(vat_ref.shape[0]):
        vat_ref[c] = vat[:, c * MOBA_BLOCK:(c + 1) * MOBA_BLOCK].astype(_BF16)
    vbt = kv[:, W_KB:].T
    for c in range(vbt_ref.shape[0]):
        vbt_ref[c] = vbt[:, c * SWA_WINDOW:(c + 1) * SWA_WINDOW].astype(_BF16)


def _inproj(x2, gn, w_in, bd, gqa, gka, gqb, gkb):
    n = x2.shape[0]
    tm = TOKEN_TILE
    row = lambda w: pl.BlockSpec((tm, w), lambda i: (i, 0))
    out_shape = (
        jax.ShapeDtypeStruct((n, W_A), _BF16),
        jax.ShapeDtypeStruct((n, W_A), _BF16),
        jax.ShapeDtypeStruct((n, W_QB), _BF16),
        jax.ShapeDtypeStruct((n, W_KB_DUP), _BF16),
        jax.ShapeDtypeStruct((n, W_GATES), _BF16),
        jax.ShapeDtypeStruct((n // MOBA_BLOCK, W_A, MOBA_BLOCK), _BF16),
        jax.ShapeDtypeStruct((n // SWA_WINDOW, W_KB, SWA_WINDOW), _BF16),
        jax.ShapeDtypeStruct((n // tm, tm // MOBA_BLOCK, W_A), _F32),
    )
    out_specs = (
        row(W_A), row(W_A), row(W_QB), row(W_KB_DUP), row(W_GATES),
        pl.BlockSpec((tm // MOBA_BLOCK, W_A, MOBA_BLOCK), lambda i: (i, 0, 0)),
        pl.BlockSpec((tm // SWA_WINDOW, W_KB, SWA_WINDOW), lambda i: (i, 0, 0)),
        pl.BlockSpec((1, tm // MOBA_BLOCK, W_A), lambda i: (i, 0, 0)),
    )
    in_specs = [row(D_MODEL), _const_spec(gn.shape), _const_spec(w_in.shape),
                _const_spec(bd.shape), _const_spec(gqa.shape), _const_spec(gka.shape),
                _const_spec(gqb.shape), _const_spec(gkb.shape)]
    return pl.pallas_call(
        _inproj_kernel, grid=(n // tm,), in_specs=in_specs, out_specs=out_specs, out_shape=out_shape,
        compiler_params=pltpu.CompilerParams(dimension_semantics=("parallel",),
                                             vmem_limit_bytes=VMEM_LIMIT),
        name="inproj",
    )(x2, gn, w_in, bd, gqa, gka, gqb, gkb)


def _moba_kernel(slopes_ref, q_ref, k_ref, vt_ref, km_ref, o_ref, kaug_sc, vaug_sc, causal_sc, s_sc):
    hp = pl.program_id(1)
    dyn0 = _dynamic_zero()
    blk = MOBA_BLOCK
    nb = q_ref.shape[0] // blk
    pair = 2 * HEAD_DIM
    kp = lax.broadcasted_iota(jnp.int32, (blk, blk), 0)
    qp = lax.broadcasted_iota(jnp.int32, (blk, blk), 1)
    causal_sc[...] = jnp.where(kp <= qp, 0.0, NEG)
    lane = lax.broadcasted_iota(jnp.int32, (blk, pair), 1)
    prow = lax.broadcasted_iota(jnp.int32, (blk, pair), 0).astype(_F32)
    ridx = lax.broadcasted_iota(jnp.int32, (nb, blk), 0)
    km_lane = lax.broadcasted_iota(jnp.int32, (nb, pair), 1)
    in_head, q_aug, km_head = [], [], []
    for e in range(2):
        head = (lane >= e * HEAD_DIM) & (lane < (e + 1) * HEAD_DIM)
        a = (1 - e) * HEAD_DIM
        sv = jnp.full((blk, pair), slopes_ref[2 * hp + e], _F32)
        hi = sv.astype(_BF16).astype(_F32)
        mid = (sv - hi).astype(_BF16).astype(_F32)
        lo = sv - hi - mid
        pieces = jnp.where(lane == a, hi, jnp.where(lane == a + 1, mid, jnp.where(lane == a + 2, lo, 0.0)))
        q_aug.append(pieces.astype(_BF16))
        k_aug = jnp.where((lane >= a) & (lane < a + 3), prow, 0.0).astype(_BF16)
        for n in range(nb):
            rows = slice(n * blk, (n + 1) * blk)
            kaug_sc[e, rows, :] = jnp.where(head, k_ref[rows, :], k_aug)
            vaug_sc[e, n, :HEAD_DIM, :] = vt_ref[n, e * HEAD_DIM:(e + 1) * HEAD_DIM, :]
            vaug_sc[e, n, HEAD_DIM:, :] = jnp.ones((vaug_sc.shape[2] - HEAD_DIM, blk), _BF16)
        in_head.append(head)
        km_e = (km_lane >= e * HEAD_DIM) & (km_lane < (e + 1) * HEAD_DIM)
        km_head.append(jnp.where(km_e, km_ref[0], 0.0).astype(_BF16))

    def scores(i, e, slot):
        slope = slopes_ref[2 * hp + e]
        qm = jnp.where(in_head[e], q_ref[i * blk:(i + 1) * blk, :], q_aug[e])
        gs = lax.dot_general(km_head[e], qm, _NT, preferred_element_type=_F32)
        radj = []
        for n in range(i):
            row = gs[n:n + 1, :]
            ahead = ((gs > row) | ((gs == row) & (ridx < n))) & (ridx < i)
            rank = jnp.sum(ahead.astype(_F32), axis=0, keepdims=True)
            radj.append(jnp.where(rank < MOBA_TOPK, 0.0, NEG) - slope * float(blk * (i - n)))
        m = None
        for n in range(i + 1):
            t = lax.dot_general(kaug_sc[e, n * blk:(n + 1) * blk, :], qm, _NT, preferred_element_type=_F32)
            if n == i:
                t = t + causal_sc[...]
            s_sc[slot, n + dyn0] = t
            bm = jnp.max(t, axis=0, keepdims=True)
            if n < i:
                bm = bm + radj[n]
            m = bm if m is None else jnp.maximum(m, bm)
        return [m - radj[n] if n < i else m for n in range(i + 1)]

    def weighted_values(i, e, slot, shifts):
        acc = None
        for n in range(i + 1):
            p = jnp.exp2(s_sc[slot, n + dyn0] - shifts[n]).astype(_BF16)
            pv = jnp.dot(vaug_sc[e, n], p, preferred_element_type=_F32)
            acc = pv if acc is None else acc + pv
        return acc[:HEAD_DIM] / acc[HEAD_DIM:HEAD_DIM + 1]

    units = [(i, e, u % s_sc.shape[0]) for u, (i, e) in enumerate((i, e) for i in range(nb) for e in range(2))]
    shifts = {u: scores(*units[u]) for u in range(MOBA_LOOKAHEAD)}
    outs = []
    for u, (i, e, slot) in enumerate(units):
        ahead = u + MOBA_LOOKAHEAD
        if ahead < len(units):
            shifts[ahead] = scores(*units[ahead])
        outs.append(weighted_values(i, e, slot, shifts.pop(u)))
        if e == 1:
            o = jnp.concatenate(outs, axis=0)
            o_ref[i * blk:(i + 1) * blk, :] = o.T.astype(o_ref.dtype)
            outs = []


def _moba(slopes, qa, ka, vat, km, batch, seq):
    nb = seq // MOBA_BLOCK
    pair = 2 * HEAD_DIM
    seq_spec = pl.BlockSpec((seq, pair), lambda b, hp: (b, hp))
    return pl.pallas_call(
        _moba_kernel, grid=(batch, A_HEADS // 2),
        in_specs=[pl.BlockSpec(memory_space=pltpu.SMEM), seq_spec, seq_spec,
                  pl.BlockSpec((nb, pair, MOBA_BLOCK), lambda b, hp: (b, hp, 0)),
                  pl.BlockSpec((1, nb, pair), lambda b, hp: (b, 0, hp))],
        out_specs=seq_spec,
        out_shape=jax.ShapeDtypeStruct(qa.shape, _BF16),
        scratch_shapes=[pltpu.VMEM((2, seq, pair), _BF16),
                        pltpu.VMEM((2, nb, HEAD_DIM + ONES_ROWS, MOBA_BLOCK), _BF16),
                        pltpu.VMEM((MOBA_BLOCK, MOBA_BLOCK), _F32),
                        pltpu.VMEM((MOBA_LOOKAHEAD + 1, nb, MOBA_BLOCK, MOBA_BLOCK), _F32)],
        compiler_params=pltpu.CompilerParams(dimension_semantics=("parallel", "parallel"),
                                             vmem_limit_bytes=VMEM_LIMIT),
        name="moba",
    )(slopes, qa, ka, vat, km)


def _swa_kernel(slopes_ref, sinks_ref, q_ref, k_ref, vt_ref, o_ref, bias_sc, vaug_sc, s_sc):
    hk = pl.program_id(1)
    dyn0 = _dynamic_zero()
    w = SWA_WINDOW
    nblk = q_ref.shape[0] // w
    kp = lax.broadcasted_iota(jnp.int32, (2 * w, 2 * w), 0)
    col = lax.broadcasted_iota(jnp.int32, (2 * w, 2 * w), 1)
    dist = (col & (w - 1)) + w - kp
    in_window = (dist >= 0) & (dist < w)
    first_head = col < w
    lane = lax.broadcasted_iota(jnp.int32, (w, 2 * HEAD_DIM), 1)
    col_row = lax.broadcasted_iota(jnp.int32, (1, 2 * w), 1)
    for pr in range(2):
        ha = hk * B_GROUP + 2 * pr
        slope = jnp.where(first_head, slopes_ref[ha], slopes_ref[ha + 1])
        bias_sc[pr] = jnp.where(in_window, -slope * dist.astype(_F32), NEG)

    for j in range(nblk):
        vaug_sc[j, :HEAD_DIM, :] = vt_ref[j]
        vaug_sc[j, HEAD_DIM:, :] = jnp.ones((vaug_sc.shape[1] - HEAD_DIM, w), _BF16)

    def scores(j, pr, slot):
        ha = hk * B_GROUP + 2 * pr
        q_t = q_ref[j * w:(j + 1) * w, pr * 2 * HEAD_DIM:(pr + 1) * 2 * HEAD_DIM]
        zero = jnp.zeros_like(q_t)
        qs = jnp.concatenate([jnp.where(lane < HEAD_DIM, q_t, zero),
                              jnp.where(lane >= HEAD_DIM, q_t, zero)], axis=0)
        k0 = max(j - 1, 0) * w
        nk = (j + 1) * w - k0
        s = lax.dot_general(k_ref[k0:k0 + nk, :], qs, _NT, preferred_element_type=_F32)
        s = s + bias_sc[pr, 2 * w - nk:, :]
        s_sc[slot + dyn0, :nk, :] = s
        sink = jnp.where(col_row < w, sinks_ref[ha], sinks_ref[ha + 1])
        return jnp.maximum(jnp.max(s, axis=0, keepdims=True), sink), sink

    def weighted_values(j, pr, slot, m, sink):
        nk = min(j + 1, 2) * w
        pb = jnp.exp2(s_sc[slot + dyn0, :nk, :] - m).astype(_BF16)
        ot = jnp.dot(vaug_sc[j], pb[nk - w:], preferred_element_type=_F32)
        if j > 0:
            ot = ot + jnp.dot(vaug_sc[j - 1], pb[:w], preferred_element_type=_F32)
        den = ot[HEAD_DIM:HEAD_DIM + 1] + jnp.exp2(sink - m)
        ot = ot[:HEAD_DIM] / den
        o2 = jnp.concatenate([ot[:, :w], ot[:, w:]], axis=0)
        o_ref[j * w:(j + 1) * w, pr * 2 * HEAD_DIM:(pr + 1) * 2 * HEAD_DIM] = o2.T.astype(o_ref.dtype)

    tiles = [(j, pr) for j in range(nblk) for pr in range(2)]
    nslot = s_sc.shape[0]
    stats = {t: scores(*tiles[t], t % nslot) for t in range(SWA_LOOKAHEAD)}
    for t, (j, pr) in enumerate(tiles):
        ahead = t + SWA_LOOKAHEAD
        if ahead < len(tiles):
            stats[ahead] = scores(*tiles[ahead], ahead % nslot)
        weighted_values(j, pr, t % nslot, *stats.pop(t))


def _swa(slopes, sinks, qb, kb, vbt, batch, seq):
    nblk = seq // SWA_WINDOW
    grp = B_GROUP * HEAD_DIM
    q_spec = pl.BlockSpec((seq, grp), lambda b, hk: (b, hk))
    return pl.pallas_call(
        _swa_kernel, grid=(batch, B_KV_HEADS),
        in_specs=[pl.BlockSpec(memory_space=pltpu.SMEM), pl.BlockSpec(memory_space=pltpu.SMEM), q_spec,
                  pl.BlockSpec((seq, 2 * HEAD_DIM), lambda b, hk: (b, hk)),
                  pl.BlockSpec((nblk, HEAD_DIM, SWA_WINDOW), lambda b, hk: (b, hk, 0))],
        out_specs=q_spec,
        out_shape=jax.ShapeDtypeStruct(qb.shape, _BF16),
        scratch_shapes=[pltpu.VMEM((2, 2 * SWA_WINDOW, 2 * SWA_WINDOW), _F32),
                        pltpu.VMEM((nblk, HEAD_DIM + ONES_ROWS, SWA_WINDOW), _BF16),
                        pltpu.VMEM((SWA_LOOKAHEAD + 1, 2 * SWA_WINDOW, 2 * SWA_WINDOW), _F32)],
        compiler_params=pltpu.CompilerParams(dimension_semantics=("parallel", "parallel"),
                                             vmem_limit_bytes=VMEM_LIMIT),
        name="swa",
    )(slopes, sinks, qb, kb, vbt)


def _merge_mlp_kernel(x_ref, oa_ref, ob_ref, g_ref, wa_ref, wb_ref, wo_ref, gm_ref, wup_ref, wdn_ref, o_ref):
    a = jnp.dot(oa_ref[...], wa_ref[...], preferred_element_type=_F32)
    b = jnp.dot(ob_ref[...], wb_ref[...], preferred_element_type=_F32)
    ga = g_ref[:, :D_MODEL].astype(_F32)
    gb = g_ref[:, D_MODEL:].astype(_F32)
    mixed = jax.nn.sigmoid(ga) * a + jax.nn.sigmoid(gb) * b
    x1 = x_ref[...] + jnp.dot(mixed.astype(_BF16), wo_ref[...], preferred_element_type=_F32)
    ms = jnp.mean(x1 * x1, axis=-1, keepdims=True)
    h2 = ((x1 * lax.rsqrt(ms + EPS)) * gm_ref[...]).astype(_BF16)
    acc = x1
    for c in range(D_FF // FF_CHUNK):
        u = jnp.dot(h2, wup_ref[:, c * FF_CHUNK:(c + 1) * FF_CHUNK], preferred_element_type=_F32)
        u = jnp.square(jnp.maximum(u, 0.0)).astype(_BF16)
        acc = acc + jnp.dot(u, wdn_ref[c * FF_CHUNK:(c + 1) * FF_CHUNK, :], preferred_element_type=_F32)
    o_ref[...] = acc


def _merge_mlp(x2, oa, ob, g, wa, wb, wo, gm, wup, wdn):
    n = x2.shape[0]
    tm = TOKEN_TILE
    row = lambda w: pl.BlockSpec((tm, w), lambda i: (i, 0))
    return pl.pallas_call(
        _merge_mlp_kernel, grid=(n // tm,),
        in_specs=[row(D_MODEL), row(W_A), row(W_QB), row(W_GATES), _const_spec(wa.shape), _const_spec(wb.shape),
                  _const_spec(wo.shape), _const_spec(gm.shape), _const_spec(wup.shape), _const_spec(wdn.shape)],
        out_specs=row(D_MODEL),
        out_shape=jax.ShapeDtypeStruct(x2.shape, x2.dtype),
        compiler_params=pltpu.CompilerParams(dimension_semantics=("parallel",),
                                             vmem_limit_bytes=VMEM_LIMIT),
        name="merge_mlp",
    )(x2, oa, ob, g, wa, wb, wo, gm, wup, wdn)


def _alibi_slopes(n):
    return jnp.exp2(-(8.0 / n) * jnp.arange(1, n + 1, dtype=_F32))


def kernel(x, norm_attn, w_in, q_norm_a, k_norm_a, q_norm_b, k_norm_b, sinks_b, w_branch_a, w_branch_b, w_out,
           norm_mlp, w_up, w_down):
    batch, seq, d = x.shape
    assert d == D_MODEL and seq % TOKEN_TILE == 0 and TOKEN_TILE % MOBA_BLOCK == 0
    slopes = _alibi_slopes(N_ATTN_HEADS)
    slopes_b, slopes_a = slopes[:B_HEADS], slopes[B_HEADS:]
    head_of = jnp.arange(MXU_TILE) // HEAD_DIM
    bd = jnp.where(head_of[:, None] == head_of[None, :], 1.0 / HEAD_DIM, 0.0).astype(_BF16)

    x2 = x.reshape(batch * seq, d)
    for l in range(norm_attn.shape[0]):
        tile_gain = lambda g, reps: jnp.tile(g, reps)[None, :]
        qa, ka, qb, kb, g, vat, vbt, km = _inproj(
            x2, norm_attn[l][None, :], w_in[l].astype(_BF16), bd,
            tile_gain(q_norm_a[l], A_HEADS), tile_gain(k_norm_a[l], A_HEADS),
            tile_gain(q_norm_b[l], B_HEADS), tile_gain(k_norm_b[l], B_KV_HEADS))
        km = km.reshape(batch, seq // MOBA_BLOCK, W_A)
        oa = _moba(slopes_a * LOG2E, qa, ka, vat, km, batch, seq)
        ob = _swa(slopes_b * LOG2E, sinks_b[l] * LOG2E, qb, kb, vbt, batch, seq)
        x2 = _merge_mlp(x2, oa, ob, g, w_branch_a[l].astype(_BF16), w_branch_b[l].astype(_BF16),
                        w_out[l].astype(_BF16), norm_mlp[l][None, :], w_up[l].astype(_BF16),
                        w_down[l].astype(_BF16))
    return x2.reshape(batch, seq, d)
```

```python
import functools

import jax
import jax.numpy as jnp
from jax import lax
from jax.experimental import pallas as pl
from jax.experimental.pallas import tpu as pltpu

D_MODEL = 1024
HEAD_DIM = 64
A_HEADS = 8
B_HEADS = 8
B_KV_HEADS = 2
B_GROUP = B_HEADS // B_KV_HEADS
N_ATTN_HEADS = A_HEADS + B_HEADS
MOBA_BLOCK = 256
MOBA_TOPK = 3
SWA_WINDOW = 128
D_FF = 4 * D_MODEL
EPS = 1e-6
NEG = -1e30
SCALE = HEAD_DIM ** -0.5
LOG2E = 1.4426950408889634

W_A = A_HEADS * HEAD_DIM
W_QB = B_HEADS * HEAD_DIM
W_KB = B_KV_HEADS * HEAD_DIM
W_KB_DUP = 2 * W_KB
W_GATES = 2 * D_MODEL

C_QA = 0
C_KA = C_QA + W_A
C_VA = C_KA + W_A
C_QB = C_VA + W_A
C_KB = C_QB + W_QB
C_VB = C_KB + W_KB
C_G = C_VB + W_KB
C_END = C_G + W_GATES

TOKEN_TILE = 512
FF_CHUNK = 1024
ONES_ROWS = 16
MOBA_LOOKAHEAD = 3
SWA_LOOKAHEAD = 5
VMEM_LIMIT = 48 * 1024 * 1024
_NT = (((1,), (1,)), ((), ()))
_BF16 = jnp.bfloat16
_F32 = jnp.float32


def _const_spec(shape):
    return pl.BlockSpec(shape, lambda *_: (0,) * len(shape), pipeline_mode=pl.Buffered(1))


def _dynamic_zero():
    return jnp.minimum(pl.program_id(0), 0)


def _inproj_kernel(x_ref, gn_ref, w_ref, gqa_ref, gka_ref, gqb_ref, gkb_ref, *refs):
    n_cast = (len(refs) - 8) // 2
    qa_ref, ka_ref, qb_ref, kb_ref, g_ref, vat_ref, vbt_ref, km_ref = refs[n_cast:n_cast + 8]
    for src, dst in zip(refs[:n_cast], refs[n_cast + 8:]):
        dst[...] = src[...].astype(_BF16)

    x = x_ref[...]
    ms = jnp.mean(x * x, axis=-1, keepdims=True)
    h = ((x * lax.rsqrt(ms + EPS)) * gn_ref[...]).astype(_BF16)

    def proj(lo, hi):
        return jnp.dot(h, w_ref[:, lo:hi], preferred_element_type=_F32)

    lane = lax.broadcasted_iota(jnp.int32, (x.shape[0], 2 * HEAD_DIM), 1)
    first = lane < HEAD_DIM

    def head_norm(y, gain_ref):
        parts = []
        for c in range(0, y.shape[-1], 2 * HEAD_DIM):
            yc = y[:, c:c + 2 * HEAD_DIM]
            sq = yc * yc
            s0 = jnp.sum(jnp.where(first, sq, 0.0), axis=-1, keepdims=True)
            s1 = jnp.sum(jnp.where(first, 0.0, sq), axis=-1, keepdims=True)
            msq = jnp.where(first, s0, s1) * (1.0 / HEAD_DIM)
            parts.append(yc * lax.rsqrt(msq + EPS))
        return jnp.concatenate(parts, axis=1) * gain_ref[...]

    qa_ref[...] = (head_norm(proj(C_QA, C_KA), gqa_ref) * (SCALE * LOG2E)).astype(_BF16)
    kn = head_norm(proj(C_KA, C_VA), gka_ref)
    ka_ref[...] = kn.astype(_BF16)
    nblk = kn.shape[0] // MOBA_BLOCK
    km_ref[0] = kn.reshape(nblk, MOBA_BLOCK, W_A).sum(axis=1) * (1.0 / MOBA_BLOCK)
    qb_ref[...] = (head_norm(proj(C_QB, C_KB), gqb_ref) * (SCALE * LOG2E)).astype(_BF16)
    g_ref[...] = proj(C_G, C_END).astype(_BF16)

    kv = proj(C_KB, C_G)
    kb = head_norm(kv[:, :W_KB], gkb_ref)
    swapped = pltpu.roll(kb, HEAD_DIM, 1)
    kb_ref[...] = jnp.concatenate([jnp.where(first, kb, swapped), jnp.where(first, swapped, kb)],
                                  axis=1).astype(_BF16)

    vat = proj(C_VA, C_QB).T
    for c in range(vat_ref.shape[0]):
        vat_ref[c] = vat[:, c * MOBA_BLOCK:(c + 1) * MOBA_BLOCK].astype(_BF16)
    vbt = kv[:, W_KB:].T
    for c in range(vbt_ref.shape[0]):
        vbt_ref[c] = vbt[:, c * SWA_WINDOW:(c + 1) * SWA_WINDOW].astype(_BF16)


def _inproj(x2, gn, w_in, gqa, gka, gqb, gkb, later_weights):
    n = x2.shape[0]
    tm = TOKEN_TILE
    steps = n // tm
    row = lambda w: pl.BlockSpec((tm, w), lambda i: (i, 0))
    slabs = [w.reshape(steps, w.shape[0] // steps, w.shape[1]) for w in later_weights]
    slab_specs = [pl.BlockSpec((1,) + s.shape[1:], lambda i: (i, 0, 0)) for s in slabs]
    out_shape = (
        jax.ShapeDtypeStruct((n, W_A), _BF16),
        jax.ShapeDtypeStruct((n, W_A), _BF16),
        jax.ShapeDtypeStruct((n, W_QB), _BF16),
        jax.ShapeDtypeStruct((n, W_KB_DUP), _BF16),
        jax.ShapeDtypeStruct((n, W_GATES), _BF16),
        jax.ShapeDtypeStruct((n // MOBA_BLOCK, W_A, MOBA_BLOCK), _BF16),
        jax.ShapeDtypeStruct((n // SWA_WINDOW, W_KB, SWA_WINDOW), _BF16),
        jax.ShapeDtypeStruct((n // tm, tm // MOBA_BLOCK, W_A), _F32),
    )
    out_specs = (
        row(W_A), row(W_A), row(W_QB), row(W_KB_DUP), row(W_GATES),
        pl.BlockSpec((tm // MOBA_BLOCK, W_A, MOBA_BLOCK), lambda i: (i, 0, 0)),
        pl.BlockSpec((tm // SWA_WINDOW, W_KB, SWA_WINDOW), lambda i: (i, 0, 0)),
        pl.BlockSpec((1, tm // MOBA_BLOCK, W_A), lambda i: (i, 0, 0)),
    )
    in_specs = [row(D_MODEL), _const_spec(gn.shape), _const_spec(w_in.shape),
                _const_spec(gqa.shape), _const_spec(gka.shape),
                _const_spec(gqb.shape), _const_spec(gkb.shape)]
    outs = pl.pallas_call(
        _inproj_kernel, grid=(steps,), in_specs=in_specs + slab_specs,
        out_specs=out_specs + tuple(slab_specs),
        out_shape=out_shape + tuple(jax.ShapeDtypeStruct(s.shape, _BF16) for s in slabs),
        compiler_params=pltpu.CompilerParams(dimension_semantics=("parallel",),
                                             vmem_limit_bytes=VMEM_LIMIT),
        name="inproj",
    )(x2, gn, w_in, gqa, gka, gqb, gkb, *slabs)
    return outs[:8], [o.reshape(w.shape) for o, w in zip(outs[8:], later_weights)]


def _moba_kernel(slopes_ref, q_ref, k_ref, vt_ref, km_ref, o_ref, kaug_sc, vaug_sc, causal_sc, s_sc):
    hp = pl.program_id(1)
    dyn0 = _dynamic_zero()
    blk = MOBA_BLOCK
    nb = q_ref.shape[0] // blk
    pair = 2 * HEAD_DIM
    kp = lax.broadcasted_iota(jnp.int32, (blk, blk), 0)
    qp = lax.broadcasted_iota(jnp.int32, (blk, blk), 1)
    causal_sc[...] = jnp.where(kp <= qp, 0.0, NEG)
    lane = lax.broadcasted_iota(jnp.int32, (blk, pair), 1)
    prow = lax.broadcasted_iota(jnp.int32, (blk, pair), 0).astype(_F32)
    ridx = lax.broadcasted_iota(jnp.int32, (nb, blk), 0)
    km_lane = lax.broadcasted_iota(jnp.int32, (nb, pair), 1)
    in_head, q_aug, km_head = [], [], []
    for e in range(2):
        head = (lane >= e * HEAD_DIM) & (lane < (e + 1) * HEAD_DIM)
        a = (1 - e) * HEAD_DIM
        sv = jnp.full((blk, pair), slopes_ref[2 * hp + e], _F32)
        hi = sv.astype(_BF16).astype(_F32)
        mid = (sv - hi).astype(_BF16).astype(_F32)
        lo = sv - hi - mid
        pieces = jnp.where(lane == a, hi, jnp.where(lane == a + 1, mid, jnp.where(lane == a + 2, lo, 0.0)))
        q_aug.append(pieces.astype(_BF16))
        k_aug = jnp.where((lane >= a) & (lane < a + 3), prow, 0.0).astype(_BF16)
        for n in range(nb):
            rows = slice(n * blk, (n + 1) * blk)
            kaug_sc[e, rows, :] = jnp.where(head, k_ref[rows, :], k_aug)
            vaug_sc[e, n, :HEAD_DIM, :] = vt_ref[n, e * HEAD_DIM:(e + 1) * HEAD_DIM, :]
            vaug_sc[e, n, HEAD_DIM:, :] = jnp.ones((vaug_sc.shape[2] - HEAD_DIM, blk), _BF16)
        in_head.append(head)
        km_e = (km_lane >= e * HEAD_DIM) & (km_lane < (e + 1) * HEAD_DIM)
        km_head.append(jnp.where(km_e, km_ref[0], 0.0).astype(_BF16))

    def scores(i, e, slot):
        slope = slopes_ref[2 * hp + e]
        qm = jnp.where(in_head[e], q_ref[i * blk:(i + 1) * blk, :], q_aug[e])
        gs = lax.dot_general(km_head[e], qm, _NT, preferred_element_type=_F32)
        radj = []
        for n in range(i):
            row = gs[n:n + 1, :]
            ahead = ((gs > row) | ((gs == row) & (ridx < n))) & (ridx < i)
            rank = jnp.sum(ahead.astype(_F32), axis=0, keepdims=True)
            radj.append(jnp.where(rank < MOBA_TOPK, 0.0, NEG) - slope * float(blk * (i - n)))
        m = None
        for n in range(i + 1):
            t = lax.dot_general(kaug_sc[e, n * blk:(n + 1) * blk, :], qm, _NT, preferred_element_type=_F32)
            if n == i:
                t = t + causal_sc[...]
            s_sc[slot, n + dyn0] = t
            bm = jnp.max(t, axis=0, keepdims=True)
            if n < i:
                bm = bm + radj[n]
            m = bm if m is None else jnp.maximum(m, bm)
        return [m - radj[n] if n < i else m for n in range(i + 1)]

    def weighted_values(i, e, slot, shifts):
        acc = None
        for n in range(i + 1):
            p = jnp.exp2(s_sc[slot, n + dyn0] - shifts[n]).astype(_BF16)
            pv = jnp.dot(vaug_sc[e, n], p, preferred_element_type=_F32)
            acc = pv if acc is None else acc + pv
        return acc[:HEAD_DIM] / acc[HEAD_DIM:HEAD_DIM + 1]

    units = [(i, e, u % s_sc.shape[0]) for u, (i, e) in enumerate((i, e) for i in range(nb) for e in range(2))]
    shifts = {u: scores(*units[u]) for u in range(MOBA_LOOKAHEAD)}
    outs = []
    for u, (i, e, slot) in enumerate(units):
        ahead = u + MOBA_LOOKAHEAD
        if ahead < len(units):
            shifts[ahead] = scores(*units[ahead])
        outs.append(weighted_values(i, e, slot, shifts.pop(u)))
        if e == 1:
            o = jnp.concatenate(outs, axis=0)
            o_ref[i * blk:(i + 1) * blk, :] = o.T.astype(o_ref.dtype)
            outs = []


def _moba(slopes, qa, ka, vat, km, batch, seq):
    nb = seq // MOBA_BLOCK
    pair = 2 * HEAD_DIM
    seq_spec = pl.BlockSpec((seq, pair), lambda b, hp: (b, hp))
    return pl.pallas_call(
        _moba_kernel, grid=(batch, A_HEADS // 2),
        in_specs=[pl.BlockSpec(memory_space=pltpu.SMEM), seq_spec, seq_spec,
                  pl.BlockSpec((nb, pair, MOBA_BLOCK), lambda b, hp: (b, hp, 0)),
                  pl.BlockSpec((1, nb, pair), lambda b, hp: (b, 0, hp))],
        out_specs=seq_spec,
        out_shape=jax.ShapeDtypeStruct(qa.shape, _BF16),
        scratch_shapes=[pltpu.VMEM((2, seq, pair), _BF16),
                        pltpu.VMEM((2, nb, HEAD_DIM + ONES_ROWS, MOBA_BLOCK), _BF16),
                        pltpu.VMEM((MOBA_BLOCK, MOBA_BLOCK), _F32),
                        pltpu.VMEM((MOBA_LOOKAHEAD + 1, nb, MOBA_BLOCK, MOBA_BLOCK), _F32)],
        compiler_params=pltpu.CompilerParams(dimension_semantics=("parallel", "parallel"),
                                             vmem_limit_bytes=VMEM_LIMIT),
        name="moba",
    )(slopes, qa, ka, vat, km)


def _swa_kernel(slopes_ref, sinks_ref, q_ref, k_ref, vt_ref, o_ref, bias_sc, vaug_sc, s_sc):
    hk = pl.program_id(1)
    dyn0 = _dynamic_zero()
    w = SWA_WINDOW
    nblk = q_ref.shape[0] // w
    kp = lax.broadcasted_iota(jnp.int32, (2 * w, 2 * w), 0)
    col = lax.broadcasted_iota(jnp.int32, (2 * w, 2 * w), 1)
    dist = (col & (w - 1)) + w - kp
    in_window = (dist >= 0) & (dist < w)
    first_head = col < w
    lane = lax.broadcasted_iota(jnp.int32, (w, 2 * HEAD_DIM), 1)
    col_row = lax.broadcasted_iota(jnp.int32, (1, 2 * w), 1)
    for pr in range(2):
        ha = hk * B_GROUP + 2 * pr
        slope = jnp.where(first_head, slopes_ref[ha], slopes_ref[ha + 1])
        bias_sc[pr] = jnp.where(in_window, -slope * dist.astype(_F32), NEG)

    for j in range(nblk):
        vaug_sc[j, :HEAD_DIM, :] = vt_ref[j]
        vaug_sc[j, HEAD_DIM:, :] = jnp.ones((vaug_sc.shape[1] - HEAD_DIM, w), _BF16)

    def scores(j, pr, slot):
        ha = hk * B_GROUP + 2 * pr
        q_t = q_ref[j * w:(j + 1) * w, pr * 2 * HEAD_DIM:(pr + 1) * 2 * HEAD_DIM]
        zero = jnp.zeros_like(q_t)
        qs = jnp.concatenate([jnp.where(lane < HEAD_DIM, q_t, zero),
                              jnp.where(lane >= HEAD_DIM, q_t, zero)], axis=0)
        k0 = max(j - 1, 0) * w
        nk = (j + 1) * w - k0
        s = lax.dot_general(k_ref[k0:k0 + nk, :], qs, _NT, preferred_element_type=_F32)
        s = s + bias_sc[pr, 2 * w - nk:, :]
        s_sc[slot + dyn0, :nk, :] = s
        sink = jnp.where(col_row < w, sinks_ref[ha], sinks_ref[ha + 1])
        return jnp.maximum(jnp.max(s, axis=0, keepdims=True), sink), sink

    def weighted_values(j, pr, slot, m, sink):
        nk = min(j + 1, 2) * w
        pb = jnp.exp2(s_sc[slot + dyn0, :nk, :] - m).astype(_BF16)
        ot = jnp.dot(vaug_sc[j], pb[nk - w:], preferred_element_type=_F32)
        if j > 0:
            ot = ot + jnp.dot(vaug_sc[j - 1], pb[:w], preferred_element_type=_F32)
        den = ot[HEAD_DIM:HEAD_DIM + 1] + jnp.exp2(sink - m)
        ot = ot[:HEAD_DIM] / den
        o2 = jnp.concatenate([ot[:, :w], ot[:, w:]], axis=0)
        o_ref[j * w:(j + 1) * w, pr * 2 * HEAD_DIM:(pr + 1) * 2 * HEAD_DIM] = o2.T.astype(o_ref.dtype)

    tiles = [(j, pr) for j in range(nblk) for pr in range(2)]
    nslot = s_sc.shape[0]
    stats = {t: scores(*tiles[t], t % nslot) for t in range(SWA_LOOKAHEAD)}
    for t, (j, pr) in enumerate(tiles):
        ahead = t + SWA_LOOKAHEAD
        if ahead < len(tiles):
            stats[ahead] = scores(*tiles[ahead], ahead % nslot)
        weighted_values(j, pr, t % nslot, *stats.pop(t))


def _swa(slopes, sinks, qb, kb, vbt, batch, seq):
    nblk = seq // SWA_WINDOW
    grp = B_GROUP * HEAD_DIM
    q_spec = pl.BlockSpec((seq, grp), lambda b, hk: (b, hk))
    return pl.pallas_call(
        _swa_kernel, grid=(batch, B_KV_HEADS),
        in_specs=[pl.BlockSpec(memory_space=pltpu.SMEM), pl.BlockSpec(memory_space=pltpu.SMEM), q_spec,
                  pl.BlockSpec((seq, 2 * HEAD_DIM), lambda b, hk: (b, hk)),
                  pl.BlockSpec((nblk, HEAD_DIM, SWA_WINDOW), lambda b, hk: (b, hk, 0))],
        out_specs=q_spec,
        out_shape=jax.ShapeDtypeStruct(qb.shape, _BF16),
        scratch_shapes=[pltpu.VMEM((2, 2 * SWA_WINDOW, 2 * SWA_WINDOW), _F32),
                        pltpu.VMEM((nblk, HEAD_DIM + ONES_ROWS, SWA_WINDOW), _BF16),
                        pltpu.VMEM((SWA_LOOKAHEAD + 1, 2 * SWA_WINDOW, 2 * SWA_WINDOW), _F32)],
        compiler_params=pltpu.CompilerParams(dimension_semantics=("parallel", "parallel"),
                                             vmem_limit_bytes=VMEM_LIMIT),
        name="swa",
    )(slopes, sinks, qb, kb, vbt)


def _merge_mlp_kernel(x_ref, oa_ref, ob_ref, g_ref, wa_ref, wb_ref, wo_ref, gm_ref, wup_ref, wdn_ref, o_ref):
    a = jnp.dot(oa_ref[...], wa_ref[...], preferred_element_type=_F32)
    b = jnp.dot(ob_ref[...], wb_ref[...], preferred_element_type=_F32)
    ga = g_ref[:, :D_MODEL].astype(_F32)
    gb = g_ref[:, D_MODEL:].astype(_F32)
    mixed = jax.nn.sigmoid(ga) * a + jax.nn.sigmoid(gb) * b
    x1 = x_ref[...] + jnp.dot(mixed.astype(_BF16), wo_ref[...], preferred_element_type=_F32)
    ms = jnp.mean(x1 * x1, axis=-1, keepdims=True)
    h2 = ((x1 * lax.rsqrt(ms + EPS)) * gm_ref[...]).astype(_BF16)
    acc = x1
    for c in range(D_FF // FF_CHUNK):
        u = jnp.dot(h2, wup_ref[:, c * FF_CHUNK:(c + 1) * FF_CHUNK], preferred_element_type=_F32)
        u = jnp.square(jnp.maximum(u, 0.0)).astype(_BF16)
        acc = acc + jnp.dot(u, wdn_ref[c * FF_CHUNK:(c + 1) * FF_CHUNK, :], preferred_element_type=_F32)
    o_ref[...] = acc


def _merge_mlp(x2, oa, ob, g, wa, wb, wo, gm, wup, wdn):
    n = x2.shape[0]
    tm = TOKEN_TILE
    row = lambda w: pl.BlockSpec((tm, w), lambda i: (i, 0))
    return pl.pallas_call(
        _merge_mlp_kernel, grid=(n // tm,),
        in_specs=[row(D_MODEL), row(W_A), row(W_QB), row(W_GATES), _const_spec(wa.shape), _const_spec(wb.shape),
                  _const_spec(wo.shape), _const_spec(gm.shape), _const_spec(wup.shape), _const_spec(wdn.shape)],
        out_specs=row(D_MODEL),
        out_shape=jax.ShapeDtypeStruct(x2.shape, x2.dtype),
        compiler_params=pltpu.CompilerParams(dimension_semantics=("parallel",),
                                             vmem_limit_bytes=VMEM_LIMIT),
        name="merge_mlp",
    )(x2, oa, ob, g, wa, wb, wo, gm, wup, wdn)


def _alibi_slopes(n):
    return jnp.exp2(-(8.0 / n) * jnp.arange(1, n + 1, dtype=_F32))


def kernel(x, norm_attn, w_in, q_norm_a, k_norm_a, q_norm_b, k_norm_b, sinks_b, w_branch_a, w_branch_b, w_out,
           norm_mlp, w_up, w_down):
    batch, seq, d = x.shape
    assert d == D_MODEL and seq % TOKEN_TILE == 0 and TOKEN_TILE % MOBA_BLOCK == 0
    slopes = _alibi_slopes(N_ATTN_HEADS)
    slopes_b, slopes_a = slopes[:B_HEADS], slopes[B_HEADS:]
    x2 = x.reshape(batch * seq, d)
    for l in range(norm_attn.shape[0]):
        tile_gain = lambda g, reps: jnp.tile(g, reps)[None, :]
        (qa, ka, qb, kb, g, vat, vbt, km), (wa, wb, wo, wup, wdn) = _inproj(
            x2, norm_attn[l][None, :], w_in[l].astype(_BF16),
            tile_gain(q_norm_a[l], A_HEADS), tile_gain(k_norm_a[l], A_HEADS),
            tile_gain(q_norm_b[l], B_HEADS), tile_gain(k_norm_b[l], B_KV_HEADS),
            (w_branch_a[l], w_branch_b[l], w_out[l], w_up[l], w_down[l]))
        km = km.reshape(batch, seq // MOBA_BLOCK, W_A)
        oa = _moba(slopes_a * LOG2E, qa, ka, vat, km, batch, seq)
        ob = _swa(slopes_b * LOG2E, sinks_b[l] * LOG2E, qb, kb, vbt, batch, seq)
        x2 = _merge_mlp(x2, oa, ob, g, wa, wb, wo, norm_mlp[l][None, :], wup, wdn)
    return x2.reshape(batch, seq, d)
```

```python
import functools

import jax
import jax.numpy as jnp
from jax import lax
from jax.experimental import pallas as pl
from jax.experimental.pallas import tpu as pltpu

D_MODEL = 1024
HEAD_DIM = 64
A_HEADS = 8
B_HEADS = 8
B_KV_HEADS = 2
B_GROUP = B_HEADS // B_KV_HEADS
N_ATTN_HEADS = A_HEADS + B_HEADS
MOBA_BLOCK = 256
MOBA_TOPK = 3
SWA_WINDOW = 128
D_FF = 4 * D_MODEL
EPS = 1e-6
NEG = -1e30
SCALE = HEAD_DIM ** -0.5
LOG2E = 1.4426950408889634

W_A = A_HEADS * HEAD_DIM
W_QB = B_HEADS * HEAD_DIM
W_KB = B_KV_HEADS * HEAD_DIM
W_KB_DUP = 2 * W_KB
W_GATES = 2 * D_MODEL

C_QA = 0
C_KA = C_QA + W_A
C_VA = C_KA + W_A
C_QB = C_VA + W_A
C_KB = C_QB + W_QB
C_VB = C_KB + W_KB
C_G = C_VB + W_KB
C_END = C_G + W_GATES

TOKEN_TILE = 512
FF_CHUNK = 1024
ONES_ROWS = 16
MOBA_LOOKAHEAD = 3
SWA_LOOKAHEAD = 5
VMEM_LIMIT = 48 * 1024 * 1024
_NT = (((1,), (1,)), ((), ()))
_BF16 = jnp.bfloat16
_F32 = jnp.float32


def _const_spec(shape):
    return pl.BlockSpec(shape, lambda *_: (0,) * len(shape), pipeline_mode=pl.Buffered(1))


def _dynamic_zero():
    return jnp.minimum(pl.program_id(0), 0)


def _inproj_kernel(x_ref, gn_ref, w_ref, gqa_ref, gka_ref, gqb_ref, gkb_ref, *refs):
    n_cast = (len(refs) - 8) // 2
    qa_ref, ka_ref, qb_ref, kb_ref, g_ref, vat_ref, vbt_ref, km_ref = refs[n_cast:n_cast + 8]
    for src, dst in zip(refs[:n_cast], refs[n_cast + 8:]):
        dst[...] = src[...].astype(_BF16)

    x = x_ref[...]
    ms = jnp.mean(x * x, axis=-1, keepdims=True)
    h = ((x * lax.rsqrt(ms + EPS)) * gn_ref[...]).astype(_BF16)

    def proj(lo, hi):
        return jnp.dot(h, w_ref[:, lo:hi], preferred_element_type=_F32)

    lane = lax.broadcasted_iota(jnp.int32, (x.shape[0], 2 * HEAD_DIM), 1)
    first = lane < HEAD_DIM

    def head_norm(y, gain_ref):
        parts = []
        for c in range(0, y.shape[-1], 2 * HEAD_DIM):
            yc = y[:, c:c + 2 * HEAD_DIM]
            sq = yc * yc
            s0 = jnp.sum(jnp.where(first, sq, 0.0), axis=-1, keepdims=True)
            s1 = jnp.sum(jnp.where(first, 0.0, sq), axis=-1, keepdims=True)
            msq = jnp.where(first, s0, s1) * (1.0 / HEAD_DIM)
            parts.append(yc * lax.rsqrt(msq + EPS))
        return jnp.concatenate(parts, axis=1) * gain_ref[...]

    qa_ref[...] = (head_norm(proj(C_QA, C_KA), gqa_ref) * (SCALE * LOG2E)).astype(_BF16)
    kn = head_norm(proj(C_KA, C_VA), gka_ref)
    ka_ref[...] = kn.astype(_BF16)
    nblk = kn.shape[0] // MOBA_BLOCK
    km_ref[0] = kn.reshape(nblk, MOBA_BLOCK, W_A).sum(axis=1) * (1.0 / MOBA_BLOCK)
    qb_ref[...] = (head_norm(proj(C_QB, C_KB), gqb_ref) * (SCALE * LOG2E)).astype(_BF16)
    g_ref[...] = proj(C_G, C_END).astype(_BF16)

    kv = proj(C_KB, C_G)
    kb = head_norm(kv[:, :W_KB], gkb_ref)
    swapped = pltpu.roll(kb, HEAD_DIM, 1)
    kb_ref[...] = jnp.concatenate([jnp.where(first, kb, swapped), jnp.where(first, swapped, kb)],
                                  axis=1).astype(_BF16)

    vat = proj(C_VA, C_QB).T
    for c in range(vat_ref.shape[0]):
        vat_ref[c] = vat[:, c * MOBA_BLOCK:(c + 1) * MOBA_BLOCK].astype(_BF16)
    vbt = kv[:, W_KB:].T
    for c in range(vbt_ref.shape[0]):
        vbt_ref[c] = vbt[:, c * SWA_WINDOW:(c + 1) * SWA_WINDOW].astype(_BF16)


def _inproj(x2, gn, w_in, gqa, gka, gqb, gkb, later_weights):
    n = x2.shape[0]
    tm = TOKEN_TILE
    steps = n // tm
    row = lambda w: pl.BlockSpec((tm, w), lambda i: (i, 0))
    slabs = [w.reshape(steps, w.shape[0] // steps, w.shape[1]) for w in later_weights]
    slab_specs = [pl.BlockSpec((1,) + s.shape[1:], lambda i: (i, 0, 0)) for s in slabs]
    out_shape = (
        jax.ShapeDtypeStruct((n, W_A), _BF16),
        jax.ShapeDtypeStruct((n, W_A), _BF16),
        jax.ShapeDtypeStruct((n, W_QB), _BF16),
        jax.ShapeDtypeStruct((n, W_KB_DUP), _BF16),
        jax.ShapeDtypeStruct((n, W_GATES), _BF16),
        jax.ShapeDtypeStruct((n // MOBA_BLOCK, W_A, MOBA_BLOCK), _BF16),
        jax.ShapeDtypeStruct((n // SWA_WINDOW, W_KB, SWA_WINDOW), _BF16),
        jax.ShapeDtypeStruct((n // tm, tm // MOBA_BLOCK, W_A), _F32),
    )
    out_specs = (
        row(W_A), row(W_A), row(W_QB), row(W_KB_DUP), row(W_GATES),
        pl.BlockSpec((tm // MOBA_BLOCK, W_A, MOBA_BLOCK), lambda i: (i, 0, 0)),
        pl.BlockSpec((tm // SWA_WINDOW, W_KB, SWA_WINDOW), lambda i: (i, 0, 0)),
        pl.BlockSpec((1, tm // MOBA_BLOCK, W_A), lambda i: (i, 0, 0)),
    )
    in_specs = [row(D_MODEL), _const_spec(gn.shape), _const_spec(w_in.shape),
                _const_spec(gqa.shape), _const_spec(gka.shape),
                _const_spec(gqb.shape), _const_spec(gkb.shape)]
    outs = pl.pallas_call(
        _inproj_kernel, grid=(steps,), in_specs=in_specs + slab_specs,
        out_specs=out_specs + tuple(slab_specs),
        out_shape=out_shape + tuple(jax.ShapeDtypeStruct(s.shape, _BF16) for s in slabs),
        compiler_params=pltpu.CompilerParams(dimension_semantics=("parallel",),
                                             vmem_limit_bytes=VMEM_LIMIT),
        name="inproj",
    )(x2, gn, w_in, gqa, gka, gqb, gkb, *slabs)
    return outs[:8], [o.reshape(w.shape) for o, w in zip(outs[8:], later_weights)]


def _moba_kernel(slopes_ref, q_ref, k_ref, vt_ref, km_ref, o_ref, kaug_sc, vaug_sc, causal_sc, s_sc):
    hp = pl.program_id(1)
    dyn0 = _dynamic_zero()
    blk = MOBA_BLOCK
    nb = q_ref.shape[0] // blk
    pair = 2 * HEAD_DIM
    kp = lax.broadcasted_iota(jnp.int32, (blk, blk), 0)
    qp = lax.broadcasted_iota(jnp.int32, (blk, blk), 1)
    causal_sc[...] = jnp.where(kp <= qp, 0.0, NEG)
    lane = lax.broadcasted_iota(jnp.int32, (blk, pair), 1)
    prow = lax.broadcasted_iota(jnp.int32, (blk, pair), 0).astype(_F32)
    ridx = lax.broadcasted_iota(jnp.int32, (nb, blk), 0)
    km_lane = lax.broadcasted_iota(jnp.int32, (nb, pair), 1)
    in_head, q_aug, km_head = [], [], []
    for e in range(2):
        head = (lane >= e * HEAD_DIM) & (lane < (e + 1) * HEAD_DIM)
        a = (1 - e) * HEAD_DIM
        sv = jnp.full((blk, pair), slopes_ref[2 * hp + e], _F32)
        hi = sv.astype(_BF16).astype(_F32)
        mid = (sv - hi).astype(_BF16).astype(_F32)
        lo = sv - hi - mid
        pieces = jnp.where(lane == a, hi, jnp.where(lane == a + 1, mid, jnp.where(lane == a + 2, lo, 0.0)))
        q_aug.append(pieces.astype(_BF16))
        k_aug = jnp.where((lane >= a) & (lane < a + 3), prow, 0.0).astype(_BF16)
        for n in range(nb):
            rows = slice(n * blk, (n + 1) * blk)
            kaug_sc[e, rows, :] = jnp.where(head, k_ref[rows, :], k_aug)
            vaug_sc[e, n, :HEAD_DIM, :] = vt_ref[n, e * HEAD_DIM:(e + 1) * HEAD_DIM, :]
            vaug_sc[e, n, HEAD_DIM:, :] = jnp.ones((vaug_sc.shape[2] - HEAD_DIM, blk), _BF16)
        in_head.append(head)
        km_e = (km_lane >= e * HEAD_DIM) & (km_lane < (e + 1) * HEAD_DIM)
        km_head.append(jnp.where(km_e, km_ref[0], 0.0).astype(_BF16))

    def scores(i, e, slot):
        slope = slopes_ref[2 * hp + e]
        qm = jnp.where(in_head[e], q_ref[i * blk:(i + 1) * blk, :], q_aug[e])
        gs = lax.dot_general(km_head[e], qm, _NT, preferred_element_type=_F32)
        radj = []
        for n in range(i):
            row = gs[n:n + 1, :]
            ahead = ((gs > row) | ((gs == row) & (ridx < n))) & (ridx < i)
            rank = jnp.sum(ahead.astype(_F32), axis=0, keepdims=True)
            radj.append(jnp.where(rank < MOBA_TOPK, 0.0, NEG) - slope * float(blk * (i - n)))
        m = None
        for n in range(i + 1):
            t = lax.dot_general(kaug_sc[e, n * blk:(n + 1) * blk, :], qm, _NT, preferred_element_type=_F32)
            if n == i:
                t = t + causal_sc[...]
            s_sc[slot, n + dyn0] = t
            bm = jnp.max(t, axis=0, keepdims=True)
            if n < i:
                bm = bm + radj[n]
            m = bm if m is None else jnp.maximum(m, bm)
        return [m - radj[n] if n < i else m for n in range(i + 1)]

    def weighted_values(i, e, slot, shifts):
        acc = None
        for n in range(i + 1):
            p = jnp.exp2(s_sc[slot, n + dyn0] - shifts[n]).astype(_BF16)
            pv = jnp.dot(vaug_sc[e, n], p, preferred_element_type=_F32)
            acc = pv if acc is None else acc + pv
        return acc[:HEAD_DIM] / acc[HEAD_DIM:HEAD_DIM + 1]

    order = [(i, e) for i in reversed(range(nb)) for e in range(2)]
    units = [(i, e, u % s_sc.shape[0]) for u, (i, e) in enumerate(order)]
    shifts = {u: scores(*units[u]) for u in range(MOBA_LOOKAHEAD)}
    outs = []
    for u, (i, e, slot) in enumerate(units):
        ahead = u + MOBA_LOOKAHEAD
        if ahead < len(units):
            shifts[ahead] = scores(*units[ahead])
        outs.append(weighted_values(i, e, slot, shifts.pop(u)))
        if e == 1:
            o = jnp.concatenate(outs, axis=0)
            o_ref[i * blk:(i + 1) * blk, :] = o.T.astype(o_ref.dtype)
            outs = []


def _moba(slopes, qa, ka, vat, km, batch, seq):
    nb = seq // MOBA_BLOCK
    pair = 2 * HEAD_DIM
    seq_spec = pl.BlockSpec((seq, pair), lambda b, hp: (b, hp))
    return pl.pallas_call(
        _moba_kernel, grid=(batch, A_HEADS // 2),
        in_specs=[pl.BlockSpec(memory_space=pltpu.SMEM), seq_spec, seq_spec,
                  pl.BlockSpec((nb, pair, MOBA_BLOCK), lambda b, hp: (b, hp, 0)),
                  pl.BlockSpec((1, nb, pair), lambda b, hp: (b, 0, hp))],
        out_specs=seq_spec,
        out_shape=jax.ShapeDtypeStruct(qa.shape, _BF16),
        scratch_shapes=[pltpu.VMEM((2, seq, pair), _BF16),
                        pltpu.VMEM((2, nb, HEAD_DIM + ONES_ROWS, MOBA_BLOCK), _BF16),
                        pltpu.VMEM((MOBA_BLOCK, MOBA_BLOCK), _F32),
                        pltpu.VMEM((MOBA_LOOKAHEAD + 1, nb, MOBA_BLOCK, MOBA_BLOCK), _F32)],
        compiler_params=pltpu.CompilerParams(dimension_semantics=("parallel", "parallel"),
                                             vmem_limit_bytes=VMEM_LIMIT),
        name="moba",
    )(slopes, qa, ka, vat, km)


def _swa_kernel(slopes_ref, sinks_ref, q_ref, k_ref, vt_ref, o_ref, bias_sc, vaug_sc, s_sc):
    hk = pl.program_id(1)
    dyn0 = _dynamic_zero()
    w = SWA_WINDOW
    nblk = q_ref.shape[0] // w
    kp = lax.broadcasted_iota(jnp.int32, (2 * w, 2 * w), 0)
    col = lax.broadcasted_iota(jnp.int32, (2 * w, 2 * w), 1)
    dist = (col & (w - 1)) + w - kp
    in_window = (dist >= 0) & (dist < w)
    first_head = col < w
    lane = lax.broadcasted_iota(jnp.int32, (w, 2 * HEAD_DIM), 1)
    col_row = lax.broadcasted_iota(jnp.int32, (1, 2 * w), 1)
    for pr in range(2):
        ha = hk * B_GROUP + 2 * pr
        slope = jnp.where(first_head, slopes_ref[ha], slopes_ref[ha + 1])
        bias_sc[pr] = jnp.where(in_window, -slope * dist.astype(_F32), NEG)

    for j in range(nblk):
        vaug_sc[j, :HEAD_DIM, :] = vt_ref[j]
        vaug_sc[j, HEAD_DIM:, :] = jnp.ones((vaug_sc.shape[1] - HEAD_DIM, w), _BF16)

    def scores(j, pr, slot):
        ha = hk * B_GROUP + 2 * pr
        q_t = q_ref[j * w:(j + 1) * w, pr * 2 * HEAD_DIM:(pr + 1) * 2 * HEAD_DIM]
        zero = jnp.zeros_like(q_t)
        qs = jnp.concatenate([jnp.where(lane < HEAD_DIM, q_t, zero),
                              jnp.where(lane >= HEAD_DIM, q_t, zero)], axis=0)
        k0 = max(j - 1, 0) * w
        nk = (j + 1) * w - k0
        s = lax.dot_general(k_ref[k0:k0 + nk, :], qs, _NT, preferred_element_type=_F32)
        s = s + bias_sc[pr, 2 * w - nk:, :]
        s_sc[slot + dyn0, :nk, :] = s
        sink = jnp.where(col_row < w, sinks_ref[ha], sinks_ref[ha + 1])
        return jnp.maximum(jnp.max(s, axis=0, keepdims=True), sink), sink

    def weighted_values(j, pr, slot, m, sink):
        nk = min(j + 1, 2) * w
        pb = jnp.exp2(s_sc[slot + dyn0, :nk, :] - m).astype(_BF16)
        ot = jnp.dot(vaug_sc[j], pb[nk - w:], preferred_element_type=_F32)
        if j > 0:
            ot = ot + jnp.dot(vaug_sc[j - 1], pb[:w], preferred_element_type=_F32)
        den = ot[HEAD_DIM:HEAD_DIM + 1] + jnp.exp2(sink - m)
        ot = ot[:HEAD_DIM] / den
        o2 = jnp.concatenate([ot[:, :w], ot[:, w:]], axis=0)
        o_ref[j * w:(j + 1) * w, pr * 2 * HEAD_DIM:(pr + 1) * 2 * HEAD_DIM] = o2.T.astype(o_ref.dtype)

    tiles = [(j, pr) for j in range(nblk) for pr in range(2)]
    nslot = s_sc.shape[0]
    stats = {t: scores(*tiles[t], t % nslot) for t in range(SWA_LOOKAHEAD)}
    for t, (j, pr) in enumerate(tiles):
        ahead = t + SWA_LOOKAHEAD
        if ahead < len(tiles):
            stats[ahead] = scores(*tiles[ahead], ahead % nslot)
        weighted_values(j, pr, t % nslot, *stats.pop(t))


def _swa(slopes, sinks, qb, kb, vbt, batch, seq):
    nblk = seq // SWA_WINDOW
    grp = B_GROUP * HEAD_DIM
    q_spec = pl.BlockSpec((seq, grp), lambda b, hk: (b, hk))
    return pl.pallas_call(
        _swa_kernel, grid=(batch, B_KV_HEADS),
        in_specs=[pl.BlockSpec(memory_space=pltpu.SMEM), pl.BlockSpec(memory_space=pltpu.SMEM), q_spec,
                  pl.BlockSpec((seq, 2 * HEAD_DIM), lambda b, hk: (b, hk)),
                  pl.BlockSpec((nblk, HEAD_DIM, SWA_WINDOW), lambda b, hk: (b, hk, 0))],
        out_specs=q_spec,
        out_shape=jax.ShapeDtypeStruct(qb.shape, _BF16),
        scratch_shapes=[pltpu.VMEM((2, 2 * SWA_WINDOW, 2 * SWA_WINDOW), _F32),
                        pltpu.VMEM((nblk, HEAD_DIM + ONES_ROWS, SWA_WINDOW), _BF16),
                        pltpu.VMEM((SWA_LOOKAHEAD + 1, 2 * SWA_WINDOW, 2 * SWA_WINDOW), _F32)],
        compiler_params=pltpu.CompilerParams(dimension_semantics=("parallel", "parallel"),
                                             vmem_limit_bytes=VMEM_LIMIT),
        name="swa",
    )(slopes, sinks, qb, kb, vbt)


def _merge_mlp_kernel(x_ref, oa_ref, ob_ref, g_ref, wa_ref, wb_ref, wo_ref, gm_ref, wup_ref, wdn_ref, o_ref):
    a = jnp.dot(oa_ref[...], wa_ref[...], preferred_element_type=_F32)
    b = jnp.dot(ob_ref[...], wb_ref[...], preferred_element_type=_F32)
    ga = g_ref[:, :D_MODEL].astype(_F32)
    gb = g_ref[:, D_MODEL:].astype(_F32)
    mixed = jax.nn.sigmoid(ga) * a + jax.nn.sigmoid(gb) * b
    x1 = x_ref[...] + jnp.dot(mixed.astype(_BF16), wo_ref[...], preferred_element_type=_F32)
    ms = jnp.mean(x1 * x1, axis=-1, keepdims=True)
    h2 = ((x1 * lax.rsqrt(ms + EPS)) * gm_ref[...]).astype(_BF16)
    acc = x1
    for c in range(D_FF // FF_CHUNK):
        u = jnp.dot(h2, wup_ref[:, c * FF_CHUNK:(c + 1) * FF_CHUNK], preferred_element_type=_F32)
        u = jnp.square(jnp.maximum(u, 0.0)).astype(_BF16)
        acc = acc + jnp.dot(u, wdn_ref[c * FF_CHUNK:(c + 1) * FF_CHUNK, :], preferred_element_type=_F32)
    o_ref[...] = acc


def _merge_mlp(x2, oa, ob, g, wa, wb, wo, gm, wup, wdn):
    n = x2.shape[0]
    tm = TOKEN_TILE
    row = lambda w: pl.BlockSpec((tm, w), lambda i: (i, 0))
    return pl.pallas_call(
        _merge_mlp_kernel, grid=(n // tm,),
        in_specs=[row(D_MODEL), row(W_A), row(W_QB), row(W_GATES), _const_spec(wa.shape), _const_spec(wb.shape),
                  _const_spec(wo.shape), _const_spec(gm.shape), _const_spec(wup.shape), _const_spec(wdn.shape)],
        out_specs=row(D_MODEL),
        out_shape=jax.ShapeDtypeStruct(x2.shape, x2.dtype),
        compiler_params=pltpu.CompilerParams(dimension_semantics=("parallel",),
                                             vmem_limit_bytes=VMEM_LIMIT),
        name="merge_mlp",
    )(x2, oa, ob, g, wa, wb, wo, gm, wup, wdn)


def _alibi_slopes(n):
    return jnp.exp2(-(8.0 / n) * jnp.arange(1, n + 1, dtype=_F32))


def kernel(x, norm_attn, w_in, q_norm_a, k_norm_a, q_norm_b, k_norm_b, sinks_b, w_branch_a, w_branch_b, w_out,
           norm_mlp, w_up, w_down):
    batch, seq, d = x.shape
    assert d == D_MODEL and seq % TOKEN_TILE == 0 and TOKEN_TILE % MOBA_BLOCK == 0
    slopes = _alibi_slopes(N_ATTN_HEADS)
    slopes_b, slopes_a = slopes[:B_HEADS], slopes[B_HEADS:]
    x2 = x.reshape(batch * seq, d)
    for l in range(norm_attn.shape[0]):
        tile_gain = lambda g, reps: jnp.tile(g, reps)[None, :]
        (qa, ka, qb, kb, g, vat, vbt, km), (wa, wb, wo, wup, wdn) = _inproj(
            x2, norm_attn[l][None, :], w_in[l].astype(_BF16),
            tile_gain(q_norm_a[l], A_HEADS), tile_gain(k_norm_a[l], A_HEADS),
            tile_gain(q_norm_b[l], B_HEADS), tile_gain(k_norm_b[l], B_KV_HEADS),
            (w_branch_a[l], w_branch_b[l], w_out[l], w_up[l], w_down[l]))
        km = km.reshape(batch, seq // MOBA_BLOCK, W_A)
        oa = _moba(slopes_a * LOG2E, qa, ka, vat, km, batch, seq)
        ob = _swa(slopes_b * LOG2E, sinks_b[l] * LOG2E, qb, kb, vbt, batch, seq)
        x2 = _merge_mlp(x2, oa, ob, g, wa, wb, wo, norm_mlp[l][None, :], wup, wdn)
    return x2.reshape(batch, seq, d)
```

```python
import functools

import jax
import jax.numpy as jnp
from jax import lax
from jax.experimental import pallas as pl
from jax.experimental.pallas import tpu as pltpu

D_MODEL = 1024
HEAD_DIM = 64
A_HEADS = 8
B_HEADS = 8
B_KV_HEADS = 2
B_GROUP = B_HEADS // B_KV_HEADS
N_ATTN_HEADS = A_HEADS + B_HEADS
MOBA_BLOCK = 256
MOBA_TOPK = 3
SWA_WINDOW = 128
D_FF = 4 * D_MODEL
EPS = 1e-6
NEG = -1e30
SCALE = HEAD_DIM ** -0.5
LOG2E = 1.4426950408889634

W_A = A_HEADS * HEAD_DIM
W_QB = B_HEADS * HEAD_DIM
W_KB = B_KV_HEADS * HEAD_DIM
W_KB_DUP = 2 * W_KB
W_GATES = 2 * D_MODEL

C_QA = 0
C_KA = C_QA + W_A
C_VA = C_KA + W_A
C_QB = C_VA + W_A
C_KB = C_QB + W_QB
C_VB = C_KB + W_KB
C_G = C_VB + W_KB
C_END = C_G + W_GATES

TOKEN_TILE = 512
FF_CHUNK = 1024
ONES_ROWS = 16
MOBA_LOOKAHEAD = 3
SWA_LOOKAHEAD = 7
VMEM_LIMIT = 48 * 1024 * 1024

_NT = (((1,), (1,)), ((), ()))
_BF16 = jnp.bfloat16
_F32 = jnp.float32


def _const_spec(shape):
    return pl.BlockSpec(shape, lambda *_: (0,) * len(shape), pipeline_mode=pl.Buffered(1))


def _dynamic_zero():
    return jnp.minimum(pl.program_id(0), 0)


def _inproj_kernel(x_ref, gn_ref, w_ref, gqa_ref, gka_ref, gqb_ref, gkb_ref, *refs):
    n_cast = (len(refs) - 8) // 2
    qa_ref, ka_ref, qb_ref, kb_ref, g_ref, vat_ref, vbt_ref, km_ref = refs[n_cast:n_cast + 8]
    for src, dst in zip(refs[:n_cast], refs[n_cast + 8:]):
        dst[...] = src[...].astype(_BF16)

    sub = MOBA_BLOCK
    n_sub = x_ref.shape[0] // sub
    lane = lax.broadcasted_iota(jnp.int32, (sub, 2 * HEAD_DIM), 1)
    first = lane < HEAD_DIM

    def normed(r):
        x = x_ref[r * sub:(r + 1) * sub, :]
        ms = jnp.mean(x * x, axis=-1, keepdims=True)
        return ((x * lax.rsqrt(ms + EPS)) * gn_ref[...]).astype(_BF16)

    def head_norm(y, gain_ref):
        parts = []
        for c in range(0, y.shape[-1], 2 * HEAD_DIM):
            yc = y[:, c:c + 2 * HEAD_DIM]
            sq = yc * yc
            s0 = jnp.sum(jnp.where(first, sq, 0.0), axis=-1, keepdims=True)
            s1 = jnp.sum(jnp.where(first, 0.0, sq), axis=-1, keepdims=True)
            msq = jnp.where(first, s0, s1) * (1.0 / HEAD_DIM)
            parts.append(yc * lax.rsqrt(msq + EPS))
        return jnp.concatenate(parts, axis=1) * gain_ref[...]

    def project(r, h):
        rows = slice(r * sub, (r + 1) * sub)

        def proj(lo, hi):
            return jnp.dot(h, w_ref[:, lo:hi], preferred_element_type=_F32)

        qa_ref[rows, :] = (head_norm(proj(C_QA, C_KA), gqa_ref) * (SCALE * LOG2E)).astype(_BF16)
        kn = head_norm(proj(C_KA, C_VA), gka_ref)
        ka_ref[rows, :] = kn.astype(_BF16)
        km_ref[0, r:r + 1, :] = jnp.sum(kn, axis=0, keepdims=True) * (1.0 / MOBA_BLOCK)
        qb_ref[rows, :] = (head_norm(proj(C_QB, C_KB), gqb_ref) * (SCALE * LOG2E)).astype(_BF16)
        g_ref[rows, :] = proj(C_G, C_END).astype(_BF16)

        kv = proj(C_KB, C_G)
        kb = head_norm(kv[:, :W_KB], gkb_ref)
        swapped = pltpu.roll(kb, HEAD_DIM, 1)
        kb_ref[rows, :] = jnp.concatenate([jnp.where(first, kb, swapped), jnp.where(first, swapped, kb)],
                                          axis=1).astype(_BF16)

        vat_ref[r] = proj(C_VA, C_QB).T.astype(_BF16)
        vbt = kv[:, W_KB:].T
        per = sub // SWA_WINDOW
        for c in range(per):
            vbt_ref[r * per + c] = vbt[:, c * SWA_WINDOW:(c + 1) * SWA_WINDOW].astype(_BF16)

    hs = [normed(r) for r in range(n_sub)]
    for r in range(n_sub):
        project(r, hs[r])


def _inproj(x2, gn, w_in, gqa, gka, gqb, gkb, later_weights):
    n = x2.shape[0]
    tm = TOKEN_TILE
    steps = n // tm
    row = lambda w: pl.BlockSpec((tm, w), lambda i: (i, 0))
    slabs = [w.reshape(steps, w.shape[0] // steps, w.shape[1]) for w in later_weights]
    slab_specs = [pl.BlockSpec((1,) + s.shape[1:], lambda i: (i, 0, 0)) for s in slabs]
    out_shape = (
        jax.ShapeDtypeStruct((n, W_A), _BF16),
        jax.ShapeDtypeStruct((n, W_A), _BF16),
        jax.ShapeDtypeStruct((n, W_QB), _BF16),
        jax.ShapeDtypeStruct((n, W_KB_DUP), _BF16),
        jax.ShapeDtypeStruct((n, W_GATES), _BF16),
        jax.ShapeDtypeStruct((n // MOBA_BLOCK, W_A, MOBA_BLOCK), _BF16),
        jax.ShapeDtypeStruct((n // SWA_WINDOW, W_KB, SWA_WINDOW), _BF16),
        jax.ShapeDtypeStruct((n // tm, tm // MOBA_BLOCK, W_A), _F32),
    )
    out_specs = (
        row(W_A), row(W_A), row(W_QB), row(W_KB_DUP), row(W_GATES),
        pl.BlockSpec((tm // MOBA_BLOCK, W_A, MOBA_BLOCK), lambda i: (i, 0, 0)),
        pl.BlockSpec((tm // SWA_WINDOW, W_KB, SWA_WINDOW), lambda i: (i, 0, 0)),
        pl.BlockSpec((1, tm // MOBA_BLOCK, W_A), lambda i: (i, 0, 0)),
    )
    in_specs = [row(D_MODEL), _const_spec(gn.shape), _const_spec(w_in.shape),
                _const_spec(gqa.shape), _const_spec(gka.shape),
                _const_spec(gqb.shape), _const_spec(gkb.shape)]
    outs = pl.pallas_call(
        _inproj_kernel, grid=(steps,), in_specs=in_specs + slab_specs,
        out_specs=out_specs + tuple(slab_specs),
        out_shape=out_shape + tuple(jax.ShapeDtypeStruct(s.shape, _BF16) for s in slabs),
        compiler_params=pltpu.CompilerParams(dimension_semantics=("parallel",),
                                             vmem_limit_bytes=VMEM_LIMIT),
        name="inproj",
    )(x2, gn, w_in, gqa, gka, gqb, gkb, *slabs)
    return outs[:8], [o.reshape(w.shape) for o, w in zip(outs[8:], later_weights)]


def _moba_kernel(slopes_ref, q_ref, k_ref, vt_ref, km_ref, o_ref, kaug_sc, vaug_sc, causal_sc, s_sc):
    hp = pl.program_id(1)
    dyn0 = _dynamic_zero()
    blk = MOBA_BLOCK
    nb = q_ref.shape[0] // blk
    pair = 2 * HEAD_DIM
    kp = lax.broadcasted_iota(jnp.int32, (blk, blk), 0)
    qp = lax.broadcasted_iota(jnp.int32, (blk, blk), 1)
    causal_sc[...] = jnp.where(kp <= qp, 0.0, NEG)
    lane = lax.broadcasted_iota(jnp.int32, (blk, pair), 1)
    prow = lax.broadcasted_iota(jnp.int32, (blk, pair), 0).astype(_F32)
    ridx = lax.broadcasted_iota(jnp.int32, (nb, blk), 0)
    km_lane = lax.broadcasted_iota(jnp.int32, (nb, pair), 1)
    in_head, q_aug, km_head = [], [], []
    for e in range(2):
        head = (lane >= e * HEAD_DIM) & (lane < (e + 1) * HEAD_DIM)
        a = (1 - e) * HEAD_DIM
        sv = jnp.full((blk, pair), slopes_ref[2 * hp + e], _F32)
        hi = sv.astype(_BF16).astype(_F32)
        mid = (sv - hi).astype(_BF16).astype(_F32)
        lo = sv - hi - mid
        pieces = jnp.where(lane == a, hi, jnp.where(lane == a + 1, mid, jnp.where(lane == a + 2, lo, 0.0)))
        q_aug.append(pieces.astype(_BF16))
        k_aug = jnp.where((lane >= a) & (lane < a + 3), prow, 0.0).astype(_BF16)
        for n in range(nb):
            rows = slice(n * blk, (n + 1) * blk)
            kaug_sc[e, rows, :] = jnp.where(head, k_ref[rows, :], k_aug)
            vaug_sc[e, n, :HEAD_DIM, :] = vt_ref[n, e * HEAD_DIM:(e + 1) * HEAD_DIM, :]
            vaug_sc[e, n, HEAD_DIM:, :] = jnp.ones((vaug_sc.shape[2] - HEAD_DIM, blk), _BF16)
        in_head.append(head)
        km_e = (km_lane >= e * HEAD_DIM) & (km_lane < (e + 1) * HEAD_DIM)
        km_head.append(jnp.where(km_e, km_ref[0], 0.0).astype(_BF16))

    def scores(i, e, slot):
        slope = slopes_ref[2 * hp + e]
        qm = jnp.where(in_head[e], q_ref[i * blk:(i + 1) * blk, :], q_aug[e])
        gs = lax.dot_general(km_head[e], qm, _NT, preferred_element_type=_F32)
        radj = []
        for n in range(i):
            row = gs[n:n + 1, :]
            ahead = ((gs > row) | ((gs == row) & (ridx < n))) & (ridx < i)
            rank = jnp.sum(ahead.astype(_F32), axis=0, keepdims=True)
            radj.append(jnp.where(rank < MOBA_TOPK, 0.0, NEG) - slope * float(blk * (i - n)))
        m = None
        for n in range(i + 1):
            t = lax.dot_general(kaug_sc[e, n * blk:(n + 1) * blk, :], qm, _NT, preferred_element_type=_F32)
            if n == i:
                t = t + causal_sc[...]
            s_sc[slot, n + dyn0] = t
            bm = jnp.max(t, axis=0, keepdims=True)
            if n < i:
                bm = bm + radj[n]
            m = bm if m is None else jnp.maximum(m, bm)
        return [m - radj[n] if n < i else m for n in range(i + 1)]

    def weighted_values(i, e, slot, shifts):
        acc = None
        for n in range(i + 1):
            p = jnp.exp2(s_sc[slot, n + dyn0] - shifts[n]).astype(_BF16)
            pv = jnp.dot(vaug_sc[e, n], p, preferred_element_type=_F32)
            acc = pv if acc is None else acc + pv
        return acc[:HEAD_DIM] / acc[HEAD_DIM:HEAD_DIM + 1]

    order = [(i, e) for i in reversed(range(nb)) for e in range(2)]
    units = [(i, e, u % s_sc.shape[0]) for u, (i, e) in enumerate(order)]
    shifts = {u: scores(*units[u]) for u in range(MOBA_LOOKAHEAD)}
    outs = []
    for u, (i, e, slot) in enumerate(units):
        ahead = u + MOBA_LOOKAHEAD
        if ahead < len(units):
            shifts[ahead] = scores(*units[ahead])
        outs.append(weighted_values(i, e, slot, shifts.pop(u)))
        if e == 1:
            o = jnp.concatenate(outs, axis=0)
            o_ref[i * blk:(i + 1) * blk, :] = o.astype(o_ref.dtype).T
            outs = []


def _moba(slopes, qa, ka, vat, km, batch, seq):
    nb = seq // MOBA_BLOCK
    pair = 2 * HEAD_DIM
    seq_spec = pl.BlockSpec((seq, pair), lambda b, hp: (b, hp))
    return pl.pallas_call(
        _moba_kernel, grid=(batch, A_HEADS // 2),
        in_specs=[pl.BlockSpec(memory_space=pltpu.SMEM), seq_spec, seq_spec,
                  pl.BlockSpec((nb, pair, MOBA_BLOCK), lambda b, hp: (b, hp, 0)),
                  pl.BlockSpec((1, nb, pair), lambda b, hp: (b, 0, hp))],
        out_specs=seq_spec,
        out_shape=jax.ShapeDtypeStruct(qa.shape, _BF16),
        scratch_shapes=[pltpu.VMEM((2, seq, pair), _BF16),
                        pltpu.VMEM((2, nb, HEAD_DIM + ONES_ROWS, MOBA_BLOCK), _BF16),
                        pltpu.VMEM((MOBA_BLOCK, MOBA_BLOCK), _F32),
                        pltpu.VMEM((MOBA_LOOKAHEAD + 1, nb, MOBA_BLOCK, MOBA_BLOCK), _F32)],
        compiler_params=pltpu.CompilerParams(dimension_semantics=("parallel", "parallel"),
                                             vmem_limit_bytes=VMEM_LIMIT),
        name="moba",
    )(slopes, qa, ka, vat, km)


def _swa_kernel(slopes_ref, sinks_ref, q_ref, k_ref, vt_ref, o_ref, bias_sc, vaug_sc, s_sc):
    hk = pl.program_id(1)
    dyn0 = _dynamic_zero()
    w = SWA_WINDOW
    sq = w // 2
    span = w + sq
    ncol = B_GROUP * sq
    ntile = q_ref.shape[0] // sq
    nblk = q_ref.shape[0] // w

    def per_head(col, ref):
        out = ref[hk * B_GROUP]
        for h in range(1, B_GROUP):
            out = jnp.where(col >= h * sq, ref[hk * B_GROUP + h], out)
        return out

    kp = lax.broadcasted_iota(jnp.int32, (span, ncol), 0)
    col = lax.broadcasted_iota(jnp.int32, (span, ncol), 1)
    dist = (col & (sq - 1)) + w - kp
    bias_sc[...] = jnp.where((dist >= 0) & (dist < w), -per_head(col, slopes_ref) * dist.astype(_F32), NEG)
    sink = per_head(lax.broadcasted_iota(jnp.int32, (1, ncol), 1), sinks_ref)
    lane = lax.broadcasted_iota(jnp.int32, (sq, 2 * HEAD_DIM), 1)
    first = lane < HEAD_DIM

    ones = jnp.ones((vaug_sc.shape[1] - HEAD_DIM, 2 * w), _BF16)
    for j in range(nblk):
        prev = vt_ref[j - 1] if j > 0 else jnp.zeros((HEAD_DIM, w), _BF16)
        vaug_sc[j, :HEAD_DIM, :] = jnp.concatenate([prev, vt_ref[j]], axis=1)
        vaug_sc[j, HEAD_DIM:, :] = ones

    def key_range(t):
        k0 = max(t - 2, 0) * sq
        return k0, (t + 1) * sq - k0

    def scores(t, slot):
        q_t = q_ref[t * sq:(t + 1) * sq, :]
        zero = jnp.zeros((sq, 2 * HEAD_DIM), q_t.dtype)
        stacked = []
        for pr in range(B_GROUP // 2):
            pair = q_t[:, pr * 2 * HEAD_DIM:(pr + 1) * 2 * HEAD_DIM]
            stacked += [jnp.where(first, pair, zero), jnp.where(first, zero, pair)]
        qs = jnp.concatenate(stacked, axis=0)
        k0, nk = key_range(t)
        s = lax.dot_general(k_ref[k0:k0 + nk, :], qs, _NT, preferred_element_type=_F32)
        s = s + bias_sc[span - nk:, :]
        s_sc[slot + dyn0, :nk, :] = s
        return jnp.maximum(jnp.max(s, axis=0, keepdims=True), sink)

    def weighted_values(t, slot, m):
        k0, nk = key_range(t)
        pb = jnp.exp2(s_sc[slot + dyn0, :nk, :] - m).astype(_BF16)
        before = k0 - (t // 2 - 1) * w
        after = 2 * w - before - nk
        pad = lambda rows: [jnp.zeros((rows, ncol), _BF16)] if rows else []
        p_full = jnp.concatenate(pad(before) + [pb] + pad(after), axis=0)
        ot = jnp.dot(vaug_sc[t // 2], p_full, preferred_element_type=_F32)
        den = ot[HEAD_DIM:HEAD_DIM + 1] + jnp.exp2(sink - m)
        return ot[:HEAD_DIM] / den

    def store(j, even, odd):
        even, odd = even.astype(o_ref.dtype), odd.astype(o_ref.dtype)
        for pr in range(B_GROUP // 2):
            lanes = slice(pr * 2 * HEAD_DIM, (pr + 1) * 2 * HEAD_DIM)
            e, o = even[:, lanes], odd[:, lanes]
            xa = jnp.where(first, e, pltpu.roll(o, sq, 1))
            xb = jnp.where(first, pltpu.roll(e, sq, 1), o)
            x = jnp.concatenate([xa, xb], axis=0)
            o_ref[j * w:(j + 1) * w, lanes] = x.T

    nslot = s_sc.shape[0]
    maxes = {t: scores(t, t % nslot) for t in range(SWA_LOOKAHEAD)}
    even = None
    for t in range(ntile):
        ahead = t + SWA_LOOKAHEAD
        if ahead < ntile:
            maxes[ahead] = scores(ahead, ahead % nslot)
        out = weighted_values(t, t % nslot, maxes.pop(t))
        if t % 2 == 0:
            even = out
        else:
            store(t // 2, even, out)


def _swa(slopes, sinks, qb, kb, vbt, batch, seq):
    nblk = seq // SWA_WINDOW
    grp = B_GROUP * HEAD_DIM
    span = SWA_WINDOW + SWA_WINDOW // 2
    q_spec = pl.BlockSpec((seq, grp), lambda b, hk: (b, hk))
    return pl.pallas_call(
        _swa_kernel, grid=(batch, B_KV_HEADS),
        in_specs=[pl.BlockSpec(memory_space=pltpu.SMEM), pl.BlockSpec(memory_space=pltpu.SMEM), q_spec,
                  pl.BlockSpec((seq, 2 * HEAD_DIM), lambda b, hk: (b, hk)),
                  pl.BlockSpec((nblk, HEAD_DIM, SWA_WINDOW), lambda b, hk: (b, hk, 0))],
        out_specs=q_spec,
        out_shape=jax.ShapeDtypeStruct(qb.shape, _BF16),
        scratch_shapes=[pltpu.VMEM((span, grp), _F32),
                        pltpu.VMEM((nblk, HEAD_DIM + ONES_ROWS, 2 * SWA_WINDOW), _BF16),
                        pltpu.VMEM((SWA_LOOKAHEAD + 1, span, grp), _F32)],
        compiler_params=pltpu.CompilerParams(dimension_semantics=("parallel", "parallel"),
                                             vmem_limit_bytes=VMEM_LIMIT),
        name="swa",
    )(slopes, sinks, qb, kb, vbt)


def _merge_mlp_kernel(x_ref, oa_ref, ob_ref, g_ref, wa_ref, wb_ref, wo_ref, gm_ref, wup_ref, wdn_ref, o_ref):
    a = jnp.dot(oa_ref[...], wa_ref[...], preferred_element_type=_F32)
    b = jnp.dot(ob_ref[...], wb_ref[...], preferred_element_type=_F32)
    ga = g_ref[:, :D_MODEL].astype(_F32)
    gb = g_ref[:, D_MODEL:].astype(_F32)
    mixed = jax.nn.sigmoid(ga) * a + jax.nn.sigmoid(gb) * b
    x1 = x_ref[...] + jnp.dot(mixed.astype(_BF16), wo_ref[...], preferred_element_type=_F32)
    ms = jnp.mean(x1 * x1, axis=-1, keepdims=True)
    h2 = ((x1 * lax.rsqrt(ms + EPS)) * gm_ref[...]).astype(_BF16)
    acc = x1
    for c in range(D_FF // FF_CHUNK):
        u = jnp.dot(h2, wup_ref[:, c * FF_CHUNK:(c + 1) * FF_CHUNK], preferred_element_type=_F32)
        u = jnp.square(jnp.maximum(u, 0.0)).astype(_BF16)
        acc = acc + jnp.dot(u, wdn_ref[c * FF_CHUNK:(c + 1) * FF_CHUNK, :], preferred_element_type=_F32)
    o_ref[...] = acc


def _merge_mlp(x2, oa, ob, g, wa, wb, wo, gm, wup, wdn):
    n = x2.shape[0]
    tm = TOKEN_TILE
    row = lambda w: pl.BlockSpec((tm, w), lambda i: (i, 0))
    return pl.pallas_call(
        _merge_mlp_kernel, grid=(n // tm,),
        in_specs=[row(D_MODEL), row(W_A), row(W_QB), row(W_GATES), _const_spec(wa.shape), _const_spec(wb.shape),
                  _const_spec(wo.shape), _const_spec(gm.shape), _const_spec(wup.shape), _const_spec(wdn.shape)],
        out_specs=row(D_MODEL),
        out_shape=jax.ShapeDtypeStruct(x2.shape, x2.dtype),
        compiler_params=pltpu.CompilerParams(dimension_semantics=("parallel",),
                                             vmem_limit_bytes=VMEM_LIMIT),
        name="merge_mlp",
    )(x2, oa, ob, g, wa, wb, wo, gm, wup, wdn)


def _alibi_slopes(n):
    return jnp.exp2(-(8.0 / n) * jnp.arange(1, n + 1, dtype=_F32))


def kernel(x, norm_attn, w_in, q_norm_a, k_norm_a, q_norm_b, k_norm_b, sinks_b, w_branch_a, w_branch_b, w_out,
           norm_mlp, w_up, w_down):
    batch, seq, d = x.shape
    assert d == D_MODEL and seq % TOKEN_TILE == 0 and TOKEN_TILE % MOBA_BLOCK == 0
    slopes = _alibi_slopes(N_ATTN_HEADS)
    slopes_b, slopes_a = slopes[:B_HEADS], slopes[B_HEADS:]
    x2 = x.reshape(batch * seq, d)
    for l in range(norm_attn.shape[0]):
        tile_gain = lambda g, reps: jnp.tile(g, reps)[None, :]
        (qa, ka, qb, kb, g, vat, vbt, km), (wa, wb, wo, wup, wdn) = _inproj(
            x2, norm_attn[l][None, :], w_in[l].astype(_BF16),
            tile_gain(q_norm_a[l], A_HEADS), tile_gain(k_norm_a[l], A_HEADS),
            tile_gain(q_norm_b[l], B_HEADS), tile_gain(k_norm_b[l], B_KV_HEADS),
            (w_branch_a[l], w_branch_b[l], w_out[l], w_up[l], w_down[l]))
        km = km.reshape(batch, seq // MOBA_BLOCK, W_A)
        oa = _moba(slopes_a * LOG2E, qa, ka, vat, km, batch, seq)
        ob = _swa(slopes_b * LOG2E, sinks_b[l] * LOG2E, qb, kb, vbt, batch, seq)
        x2 = _merge_mlp(x2, oa, ob, g, wa, wb, wo, norm_mlp[l][None, :], wup, wdn)
    return x2.reshape(batch, seq, d)
```

```python
import functools

import jax
import jax.numpy as jnp
from jax import lax
from jax.experimental import pallas as pl
from jax.experimental.pallas import tpu as pltpu

D_MODEL = 1024
HEAD_DIM = 64
A_HEADS = 8
B_HEADS = 8
B_KV_HEADS = 2
B_GROUP = B_HEADS // B_KV_HEADS
N_ATTN_HEADS = A_HEADS + B_HEADS
MOBA_BLOCK = 256
MOBA_TOPK = 3
SWA_WINDOW = 128
D_FF = 4 * D_MODEL
EPS = 1e-6
NEG = -1e30
SCALE = HEAD_DIM ** -0.5
LOG2E = 1.4426950408889634

W_A = A_HEADS * HEAD_DIM
W_QB = B_HEADS * HEAD_DIM
W_KB = B_KV_HEADS * HEAD_DIM
W_KB_DUP = 2 * W_KB
W_GATES = 2 * D_MODEL

C_QA = 0
C_KA = C_QA + W_A
C_VA = C_KA + W_A
C_QB = C_VA + W_A
C_KB = C_QB + W_QB
C_VB = C_KB + W_KB
C_G = C_VB + W_KB
C_END = C_G + W_GATES

INPROJ_TILE = 1024
MERGE_TILE = 512
FF_CHUNK = 1024
ONES_ROWS = 16
MOBA_LOOKAHEAD = 3
SWA_LOOKAHEAD = 7
VMEM_LIMIT = 48 * 1024 * 1024

_NT = (((1,), (1,)), ((), ()))
_BF16 = jnp.bfloat16
_F32 = jnp.float32


def _const_spec(shape):
    return pl.BlockSpec(shape, lambda *_: (0,) * len(shape), pipeline_mode=pl.Buffered(1))


def _dynamic_zero():
    return jnp.minimum(pl.program_id(0), 0)


def _inproj_kernel(x_ref, gn_ref, w_ref, gqa_ref, gka_ref, gqb_ref, gkb_ref, *refs):
    n_cast = (len(refs) - 8) // 2
    qa_ref, ka_ref, qb_ref, kb_ref, g_ref, vat_ref, vbt_ref, km_ref = refs[n_cast:n_cast + 8]
    for src, dst in zip(refs[:n_cast], refs[n_cast + 8:]):
        dst[...] = src[...].astype(_BF16)

    sub = MOBA_BLOCK
    n_sub = x_ref.shape[0] // sub
    lane = lax.broadcasted_iota(jnp.int32, (sub, 2 * HEAD_DIM), 1)
    first = lane < HEAD_DIM

    def normed(r):
        x = x_ref[r * sub:(r + 1) * sub, :]
        ms = jnp.mean(x * x, axis=-1, keepdims=True)
        return ((x * lax.rsqrt(ms + EPS)) * gn_ref[...]).astype(_BF16)

    def head_norm(y, gain_ref):
        parts = []
        for c in range(0, y.shape[-1], 2 * HEAD_DIM):
            yc = y[:, c:c + 2 * HEAD_DIM]
            sq = yc * yc
            s0 = jnp.sum(jnp.where(first, sq, 0.0), axis=-1, keepdims=True)
            s1 = jnp.sum(jnp.where(first, 0.0, sq), axis=-1, keepdims=True)
            msq = jnp.where(first, s0, s1) * (1.0 / HEAD_DIM)
            parts.append(yc * lax.rsqrt(msq + EPS))
        return jnp.concatenate(parts, axis=1) * gain_ref[...]

    def project(r, h):
        rows = slice(r * sub, (r + 1) * sub)

        def proj(lo, hi):
            return jnp.dot(h, w_ref[:, lo:hi], preferred_element_type=_F32)

        qa_ref[rows, :] = (head_norm(proj(C_QA, C_KA), gqa_ref) * (SCALE * LOG2E)).astype(_BF16)
        kn = head_norm(proj(C_KA, C_VA), gka_ref)
        ka_ref[rows, :] = kn.astype(_BF16)
        km_ref[0, r:r + 1, :] = jnp.sum(kn, axis=0, keepdims=True) * (1.0 / MOBA_BLOCK)
        qb_ref[rows, :] = (head_norm(proj(C_QB, C_KB), gqb_ref) * (SCALE * LOG2E)).astype(_BF16)
        g_ref[rows, :] = proj(C_G, C_END).astype(_BF16)

        kv = proj(C_KB, C_G)
        kb = head_norm(kv[:, :W_KB], gkb_ref)
        swapped = pltpu.roll(kb, HEAD_DIM, 1)
        kb_ref[rows, :] = jnp.concatenate([jnp.where(first, kb, swapped), jnp.where(first, swapped, kb)],
                                          axis=1).astype(_BF16)

        vat_ref[r] = proj(C_VA, C_QB).T.astype(_BF16)
        vbt = kv[:, W_KB:].T
        per = sub // SWA_WINDOW
        for c in range(per):
            vbt_ref[r * per + c] = vbt[:, c * SWA_WINDOW:(c + 1) * SWA_WINDOW].astype(_BF16)

    hs = [normed(r) for r in range(n_sub)]
    for r in range(n_sub):
        project(r, hs[r])


def _inproj(x2, gn, w_in, gqa, gka, gqb, gkb, later_weights):
    n = x2.shape[0]
    tm = INPROJ_TILE
    steps = n // tm
    row = lambda w: pl.BlockSpec((tm, w), lambda i: (i, 0))
    slabs = [w.reshape(steps, w.shape[0] // steps, w.shape[1]) for w in later_weights]
    slab_specs = [pl.BlockSpec((1,) + s.shape[1:], lambda i: (i, 0, 0)) for s in slabs]
    out_shape = (
        jax.ShapeDtypeStruct((n, W_A), _BF16),
        jax.ShapeDtypeStruct((n, W_A), _BF16),
        jax.ShapeDtypeStruct((n, W_QB), _BF16),
        jax.ShapeDtypeStruct((n, W_KB_DUP), _BF16),
        jax.ShapeDtypeStruct((n, W_GATES), _BF16),
        jax.ShapeDtypeStruct((n // MOBA_BLOCK, W_A, MOBA_BLOCK), _BF16),
        jax.ShapeDtypeStruct((n // SWA_WINDOW, W_KB, SWA_WINDOW), _BF16),
        jax.ShapeDtypeStruct((n // tm, tm // MOBA_BLOCK, W_A), _F32),
    )
    out_specs = (
        row(W_A), row(W_A), row(W_QB), row(W_KB_DUP), row(W_GATES),
        pl.BlockSpec((tm // MOBA_BLOCK, W_A, MOBA_BLOCK), lambda i: (i, 0, 0)),
        pl.BlockSpec((tm // SWA_WINDOW, W_KB, SWA_WINDOW), lambda i: (i, 0, 0)),
        pl.BlockSpec((1, tm // MOBA_BLOCK, W_A), lambda i: (i, 0, 0)),
    )
    in_specs = [row(D_MODEL), _const_spec(gn.shape), _const_spec(w_in.shape),
                _const_spec(gqa.shape), _const_spec(gka.shape),
                _const_spec(gqb.shape), _const_spec(gkb.shape)]
    outs = pl.pallas_call(
        _inproj_kernel, grid=(steps,), in_specs=in_specs + slab_specs,
        out_specs=out_specs + tuple(slab_specs),
        out_shape=out_shape + tuple(jax.ShapeDtypeStruct(s.shape, _BF16) for s in slabs),
        compiler_params=pltpu.CompilerParams(dimension_semantics=("parallel",),
                                             vmem_limit_bytes=VMEM_LIMIT),
        name="inproj",
    )(x2, gn, w_in, gqa, gka, gqb, gkb, *slabs)
    return outs[:8], [o.reshape(w.shape) for o, w in zip(outs[8:], later_weights)]


def _moba_kernel(slopes_ref, q_ref, k_ref, vt_ref, km_ref, o_ref, kaug_sc, vaug_sc, causal_sc, s_sc):
    hp = pl.program_id(1)
    dyn0 = _dynamic_zero()
    blk = MOBA_BLOCK
    nb = q_ref.shape[0] // blk
    pair = 2 * HEAD_DIM
    kp = lax.broadcasted_iota(jnp.int32, (blk, blk), 0)
    qp = lax.broadcasted_iota(jnp.int32, (blk, blk), 1)
    causal_sc[...] = jnp.where(kp <= qp, 0.0, NEG)
    lane = lax.broadcasted_iota(jnp.int32, (blk, pair), 1)
    prow = lax.broadcasted_iota(jnp.int32, (blk, pair), 0).astype(_F32)
    ridx = lax.broadcasted_iota(jnp.int32, (nb, blk), 0)
    km_lane = lax.broadcasted_iota(jnp.int32, (nb, pair), 1)
    in_head, q_aug, km_head = [], [], []
    for e in range(2):
        head = (lane >= e * HEAD_DIM) & (lane < (e + 1) * HEAD_DIM)
        a = (1 - e) * HEAD_DIM
        sv = jnp.full((blk, pair), slopes_ref[2 * hp + e], _F32)
        hi = sv.astype(_BF16).astype(_F32)
        mid = (sv - hi).astype(_BF16).astype(_F32)
        lo = sv - hi - mid
        pieces = jnp.where(lane == a, hi, jnp.where(lane == a + 1, mid, jnp.where(lane == a + 2, lo, 0.0)))
        q_aug.append(pieces.astype(_BF16))
        k_aug = jnp.where((lane >= a) & (lane < a + 3), prow, 0.0).astype(_BF16)
        for n in range(nb):
            rows = slice(n * blk, (n + 1) * blk)
            kaug_sc[e, rows, :] = jnp.where(head, k_ref[rows, :], k_aug)
            vaug_sc[e, n, :HEAD_DIM, :] = vt_ref[n, e * HEAD_DIM:(e + 1) * HEAD_DIM, :]
            vaug_sc[e, n, HEAD_DIM:, :] = jnp.ones((vaug_sc.shape[2] - HEAD_DIM, blk), _BF16)
        in_head.append(head)
        km_e = (km_lane >= e * HEAD_DIM) & (km_lane < (e + 1) * HEAD_DIM)
        km_head.append(jnp.where(km_e, km_ref[0], 0.0).astype(_BF16))

    def scores(i, e, slot):
        slope = slopes_ref[2 * hp + e]
        qm = jnp.where(in_head[e], q_ref[i * blk:(i + 1) * blk, :], q_aug[e])
        gs = lax.dot_general(km_head[e], qm, _NT, preferred_element_type=_F32)
        radj = []
        for n in range(i):
            row = gs[n:n + 1, :]
            ahead = ((gs > row) | ((gs == row) & (ridx < n))) & (ridx < i)
            rank = jnp.sum(ahead.astype(_F32), axis=0, keepdims=True)
            radj.append(jnp.where(rank < MOBA_TOPK, 0.0, NEG) - slope * float(blk * (i - n)))
        m = None
        for n in range(i + 1):
            t = lax.dot_general(kaug_sc[e, n * blk:(n + 1) * blk, :], qm, _NT, preferred_element_type=_F32)
            if n == i:
                t = t + causal_sc[...]
            s_sc[slot, n + dyn0] = t
            bm = jnp.max(t, axis=0, keepdims=True)
            if n < i:
                bm = bm + radj[n]
            m = bm if m is None else jnp.maximum(m, bm)
        return [m - radj[n] if n < i else m for n in range(i + 1)]

    def weighted_values(i, e, slot, shifts):
        acc = None
        for n in range(i + 1):
            p = jnp.exp2(s_sc[slot, n + dyn0] - shifts[n]).astype(_BF16)
            pv = jnp.dot(vaug_sc[e, n], p, preferred_element_type=_F32)
            acc = pv if acc is None else acc + pv
        return acc[:HEAD_DIM] / acc[HEAD_DIM:HEAD_DIM + 1]

    order = [(i, e) for i in reversed(range(nb)) for e in range(2)]
    units = [(i, e, u % s_sc.shape[0]) for u, (i, e) in enumerate(order)]
    shifts = {u: scores(*units[u]) for u in range(MOBA_LOOKAHEAD)}
    outs = []
    for u, (i, e, slot) in enumerate(units):
        ahead = u + MOBA_LOOKAHEAD
        if ahead < len(units):
            shifts[ahead] = scores(*units[ahead])
        outs.append(weighted_values(i, e, slot, shifts.pop(u)))
        if e == 1:
            o = jnp.concatenate(outs, axis=0)
            o_ref[i * blk:(i + 1) * blk, :] = o.astype(o_ref.dtype).T
            outs = []


def _moba(slopes, qa, ka, vat, km, batch, seq):
    nb = seq // MOBA_BLOCK
    pair = 2 * HEAD_DIM
    seq_spec = pl.BlockSpec((seq, pair), lambda b, hp: (b, hp))
    return pl.pallas_call(
        _moba_kernel, grid=(batch, A_HEADS // 2),
        in_specs=[pl.BlockSpec(memory_space=pltpu.SMEM), seq_spec, seq_spec,
                  pl.BlockSpec((nb, pair, MOBA_BLOCK), lambda b, hp: (b, hp, 0)),
                  pl.BlockSpec((1, nb, pair), lambda b, hp: (b, 0, hp))],
        out_specs=seq_spec,
        out_shape=jax.ShapeDtypeStruct(qa.shape, _BF16),
        scratch_shapes=[pltpu.VMEM((2, seq, pair), _BF16),
                        pltpu.VMEM((2, nb, HEAD_DIM + ONES_ROWS, MOBA_BLOCK), _BF16),
                        pltpu.VMEM((MOBA_BLOCK, MOBA_BLOCK), _F32),
                        pltpu.VMEM((MOBA_LOOKAHEAD + 1, nb, MOBA_BLOCK, MOBA_BLOCK), _F32)],
        compiler_params=pltpu.CompilerParams(dimension_semantics=("parallel", "parallel"),
                                             vmem_limit_bytes=VMEM_LIMIT),
        name="moba",
    )(slopes, qa, ka, vat, km)


def _swa_kernel(slopes_ref, sinks_ref, q_ref, k_ref, vt_ref, o_ref, bias_sc, vaug_sc, s_sc):
    hk = pl.program_id(1)
    dyn0 = _dynamic_zero()
    w = SWA_WINDOW
    sq = w // 2
    span = w + sq
    ncol = B_GROUP * sq
    ntile = q_ref.shape[0] // sq
    nblk = q_ref.shape[0] // w

    def per_head(col, ref):
        out = ref[hk * B_GROUP]
        for h in range(1, B_GROUP):
            out = jnp.where(col >= h * sq, ref[hk * B_GROUP + h], out)
        return out

    kp = lax.broadcasted_iota(jnp.int32, (span, ncol), 0)
    col = lax.broadcasted_iota(jnp.int32, (span, ncol), 1)
    dist = (col & (sq - 1)) + w - kp
    bias_sc[...] = jnp.where((dist >= 0) & (dist < w), -per_head(col, slopes_ref) * dist.astype(_F32), NEG)
    sink = per_head(lax.broadcasted_iota(jnp.int32, (1, ncol), 1), sinks_ref)
    lane = lax.broadcasted_iota(jnp.int32, (sq, 2 * HEAD_DIM), 1)
    first = lane < HEAD_DIM

    ones = jnp.ones((vaug_sc.shape[1] - HEAD_DIM, 2 * w), _BF16)
    for j in range(nblk):
        prev = vt_ref[j - 1] if j > 0 else jnp.zeros((HEAD_DIM, w), _BF16)
        vaug_sc[j, :HEAD_DIM, :] = jnp.concatenate([prev, vt_ref[j]], axis=1)
        vaug_sc[j, HEAD_DIM:, :] = ones

    def key_range(t):
        k0 = max(t - 2, 0) * sq
        return k0, (t + 1) * sq - k0

    def scores(t, slot):
        q_t = q_ref[t * sq:(t + 1) * sq, :]
        zero = jnp.zeros((sq, 2 * HEAD_DIM), q_t.dtype)
        stacked = []
        for pr in range(B_GROUP // 2):
            pair = q_t[:, pr * 2 * HEAD_DIM:(pr + 1) * 2 * HEAD_DIM]
            stacked += [jnp.where(first, pair, zero), jnp.where(first, zero, pair)]
        qs = jnp.concatenate(stacked, axis=0)
        k0, nk = key_range(t)
        s = lax.dot_general(k_ref[k0:k0 + nk, :], qs, _NT, preferred_element_type=_F32)
        s = s + bias_sc[span - nk:, :]
        s_sc[slot + dyn0, :nk, :] = s
        return jnp.maximum(jnp.max(s, axis=0, keepdims=True), sink)

    def weighted_values(t, slot, m):
        k0, nk = key_range(t)
        pb = jnp.exp2(s_sc[slot + dyn0, :nk, :] - m).astype(_BF16)
        before = k0 - (t // 2 - 1) * w
        after = 2 * w - before - nk
        pad = lambda rows: [jnp.zeros((rows, ncol), _BF16)] if rows else []
        p_full = jnp.concatenate(pad(before) + [pb] + pad(after), axis=0)
        ot = jnp.dot(vaug_sc[t // 2], p_full, preferred_element_type=_F32)
        den = ot[HEAD_DIM:HEAD_DIM + 1] + jnp.exp2(sink - m)
        return ot[:HEAD_DIM] / den

    def store(j, even, odd):
        even, odd = even.astype(o_ref.dtype), odd.astype(o_ref.dtype)
        for pr in range(B_GROUP // 2):
            lanes = slice(pr * 2 * HEAD_DIM, (pr + 1) * 2 * HEAD_DIM)
            e, o = even[:, lanes], odd[:, lanes]
            xa = jnp.where(first, e, pltpu.roll(o, sq, 1))
            xb = jnp.where(first, pltpu.roll(e, sq, 1), o)
            x = jnp.concatenate([xa, xb], axis=0)
            o_ref[j * w:(j + 1) * w, lanes] = x.T

    nslot = s_sc.shape[0]
    maxes = {t: scores(t, t % nslot) for t in range(SWA_LOOKAHEAD)}
    even = None
    for t in range(ntile):
        ahead = t + SWA_LOOKAHEAD
        if ahead < ntile:
            maxes[ahead] = scores(ahead, ahead % nslot)
        out = weighted_values(t, t % nslot, maxes.pop(t))
        if t % 2 == 0:
            even = out
        else:
            store(t // 2, even, out)


def _swa(slopes, sinks, qb, kb, vbt, batch, seq):
    nblk = seq // SWA_WINDOW
    grp = B_GROUP * HEAD_DIM
    span = SWA_WINDOW + SWA_WINDOW // 2
    q_spec = pl.BlockSpec((seq, grp), lambda b, hk: (b, hk))
    return pl.pallas_call(
        _swa_kernel, grid=(batch, B_KV_HEADS),
        in_specs=[pl.BlockSpec(memory_space=pltpu.SMEM), pl.BlockSpec(memory_space=pltpu.SMEM), q_spec,
                  pl.BlockSpec((seq, 2 * HEAD_DIM), lambda b, hk: (b, hk)),
                  pl.BlockSpec((nblk, HEAD_DIM, SWA_WINDOW), lambda b, hk: (b, hk, 0))],
        out_specs=q_spec,
        out_shape=jax.ShapeDtypeStruct(qb.shape, _BF16),
        scratch_shapes=[pltpu.VMEM((span, grp), _F32),
                        pltpu.VMEM((nblk, HEAD_DIM + ONES_ROWS, 2 * SWA_WINDOW), _BF16),
                        pltpu.VMEM((SWA_LOOKAHEAD + 1, span, grp), _F32)],
        compiler_params=pltpu.CompilerParams(dimension_semantics=("parallel", "parallel"),
                                             vmem_limit_bytes=VMEM_LIMIT),
        name="swa",
    )(slopes, sinks, qb, kb, vbt)


def _merge_mlp_kernel(x_ref, oa_ref, ob_ref, g_ref, wa_ref, wb_ref, wo_ref, gm_ref, wup_ref, wdn_ref, o_ref):
    a = jnp.dot(oa_ref[...], wa_ref[...], preferred_element_type=_F32)
    b = jnp.dot(ob_ref[...], wb_ref[...], preferred_element_type=_F32)
    ga = g_ref[:, :D_MODEL].astype(_F32)
    gb = g_ref[:, D_MODEL:].astype(_F32)
    mixed = jax.nn.sigmoid(ga) * a + jax.nn.sigmoid(gb) * b
    x1 = x_ref[...] + jnp.dot(mixed.astype(_BF16), wo_ref[...], preferred_element_type=_F32)
    ms = jnp.mean(x1 * x1, axis=-1, keepdims=True)
    h2 = ((x1 * lax.rsqrt(ms + EPS)) * gm_ref[...]).astype(_BF16)
    acc = x1
    for c in range(D_FF // FF_CHUNK):
        u = jnp.dot(h2, wup_ref[:, c * FF_CHUNK:(c + 1) * FF_CHUNK], preferred_element_type=_F32)
        u = jnp.square(jnp.maximum(u, 0.0)).astype(_BF16)
        acc = acc + jnp.dot(u, wdn_ref[c * FF_CHUNK:(c + 1) * FF_CHUNK, :], preferred_element_type=_F32)
    o_ref[...] = acc


def _merge_mlp(x2, oa, ob, g, wa, wb, wo, gm, wup, wdn):
    n = x2.shape[0]
    tm = MERGE_TILE
    row = lambda w: pl.BlockSpec((tm, w), lambda i: (i, 0))
    return pl.pallas_call(
        _merge_mlp_kernel, grid=(n // tm,),
        in_specs=[row(D_MODEL), row(W_A), row(W_QB), row(W_GATES), _const_spec(wa.shape), _const_spec(wb.shape),
                  _const_spec(wo.shape), _const_spec(gm.shape), _const_spec(wup.shape), _const_spec(wdn.shape)],
        out_specs=row(D_MODEL),
        out_shape=jax.ShapeDtypeStruct(x2.shape, x2.dtype),
        compiler_params=pltpu.CompilerParams(dimension_semantics=("parallel",),
                                             vmem_limit_bytes=VMEM_LIMIT),
        name="merge_mlp",
    )(x2, oa, ob, g, wa, wb, wo, gm, wup, wdn)


def _alibi_slopes(n):
    return jnp.exp2(-(8.0 / n) * jnp.arange(1, n + 1, dtype=_F32))


def kernel(x, norm_attn, w_in, q_norm_a, k_norm_a, q_norm_b, k_norm_b, sinks_b, w_branch_a, w_branch_b, w_out,
           norm_mlp, w_up, w_down):
    batch, seq, d = x.shape
    assert d == D_MODEL and (batch * seq) % INPROJ_TILE == 0 and (batch * seq) % MERGE_TILE == 0
    assert seq % MOBA_BLOCK == 0 and INPROJ_TILE % MOBA_BLOCK == 0
    slopes = _alibi_slopes(N_ATTN_HEADS)
    slopes_b, slopes_a = slopes[:B_HEADS], slopes[B_HEADS:]
    x2 = x.reshape(batch * seq, d)
    for l in range(norm_attn.shape[0]):
        tile_gain = lambda g, reps: jnp.tile(g, reps)[None, :]
        (qa, ka, qb, kb, g, vat, vbt, km), (wa, wb, wo, wup, wdn) = _inproj(
            x2, norm_attn[l][None, :], w_in[l].astype(_BF16),
            tile_gain(q_norm_a[l], A_HEADS), tile_gain(k_norm_a[l], A_HEADS),
            tile_gain(q_norm_b[l], B_HEADS), tile_gain(k_norm_b[l], B_KV_HEADS),
            (w_branch_a[l], w_branch_b[l], w_out[l], w_up[l], w_down[l]))
        km = km.reshape(batch, seq // MOBA_BLOCK, W_A)
        oa = _moba(slopes_a * LOG2E, qa, ka, vat, km, batch, seq)
        ob = _swa(slopes_b * LOG2E, sinks_b[l] * LOG2E, qb, kb, vbt, batch, seq)
        x2 = _merge_mlp(x2, oa, ob, g, wa, wb, wo, norm_mlp[l][None, :], wup, wdn)
    return x2.reshape(batch, seq, d)
```

```python
import functools

import jax
import jax.numpy as jnp
from jax import lax
from jax.experimental import pallas as pl
from jax.experimental.pallas import tpu as pltpu

D_MODEL = 1024
HEAD_DIM = 64
A_HEADS = 8
B_HEADS = 8
B_KV_HEADS = 2
B_GROUP = B_HEADS // B_KV_HEADS
N_ATTN_HEADS = A_HEADS + B_HEADS
MOBA_BLOCK = 256
MOBA_TOPK = 3
SWA_WINDOW = 128
D_FF = 4 * D_MODEL
EPS = 1e-6
NEG = -1e30
SCALE = HEAD_DIM ** -0.5
LOG2E = 1.4426950408889634

W_A = A_HEADS * HEAD_DIM
W_QB = B_HEADS * HEAD_DIM
W_KB = B_KV_HEADS * HEAD_DIM
W_KB_DUP = 2 * W_KB
W_GATES = 2 * D_MODEL

C_QA = 0
C_KA = C_QA + W_A
C_VA = C_KA + W_A
C_QB = C_VA + W_A
C_KB = C_QB + W_QB
C_VB = C_KB + W_KB
C_G = C_VB + W_KB
C_END = C_G + W_GATES

INPROJ_TILE = 1024
MERGE_TILE = 512
FF_CHUNK = 1024
ONES_ROWS = 16
MOBA_PAIRS = 4
MOBA_LOOKAHEAD = 3
SWA_LOOKAHEAD = 7
VMEM_LIMIT = 48 * 1024 * 1024

_NT = (((1,), (1,)), ((), ()))
_BF16 = jnp.bfloat16
_F32 = jnp.float32


def _const_spec(shape):
    return pl.BlockSpec(shape, lambda *_: (0,) * len(shape), pipeline_mode=pl.Buffered(1))


def _dynamic_zero():
    return jnp.minimum(pl.program_id(0), 0)


def _inproj_kernel(x_ref, gn_ref, w_ref, gqa_ref, gka_ref, gqb_ref, gkb_ref, *refs):
    n_cast = (len(refs) - 8) // 2
    qa_ref, ka_ref, qb_ref, kb_ref, g_ref, vat_ref, vbt_ref, km_ref = refs[n_cast:n_cast + 8]
    for src, dst in zip(refs[:n_cast], refs[n_cast + 8:]):
        dst[...] = src[...].astype(_BF16)

    sub = MOBA_BLOCK
    n_sub = x_ref.shape[0] // sub
    lane = lax.broadcasted_iota(jnp.int32, (sub, 2 * HEAD_DIM), 1)
    first = lane < HEAD_DIM

    def normed(r):
        x = x_ref[r * sub:(r + 1) * sub, :]
        ms = jnp.mean(x * x, axis=-1, keepdims=True)
        return ((x * lax.rsqrt(ms + EPS)) * gn_ref[...]).astype(_BF16)

    def head_norm(y, gain_ref):
        parts = []
        for c in range(0, y.shape[-1], 2 * HEAD_DIM):
            yc = y[:, c:c + 2 * HEAD_DIM]
            sq = yc * yc
            s0 = jnp.sum(jnp.where(first, sq, 0.0), axis=-1, keepdims=True)
            s1 = jnp.sum(jnp.where(first, 0.0, sq), axis=-1, keepdims=True)
            msq = jnp.where(first, s0, s1) * (1.0 / HEAD_DIM)
            parts.append(yc * lax.rsqrt(msq + EPS))
        return jnp.concatenate(parts, axis=1) * gain_ref[...]

    def project(r, h):
        rows = slice(r * sub, (r + 1) * sub)

        def proj(lo, hi):
            return jnp.dot(h, w_ref[:, lo:hi], preferred_element_type=_F32)

        qa_ref[rows, :] = (head_norm(proj(C_QA, C_KA), gqa_ref) * (SCALE * LOG2E)).astype(_BF16)
        kn = head_norm(proj(C_KA, C_VA), gka_ref)
        ka_ref[rows, :] = kn.astype(_BF16)
        km_ref[0, r:r + 1, :] = jnp.sum(kn, axis=0, keepdims=True) * (1.0 / MOBA_BLOCK)
        qb_ref[rows, :] = (head_norm(proj(C_QB, C_KB), gqb_ref) * (SCALE * LOG2E)).astype(_BF16)
        g_ref[rows, :] = proj(C_G, C_END).astype(_BF16)

        kv = proj(C_KB, C_G)
        kb = head_norm(kv[:, :W_KB], gkb_ref)
        swapped = pltpu.roll(kb, HEAD_DIM, 1)
        kb_ref[rows, :] = jnp.concatenate([jnp.where(first, kb, swapped), jnp.where(first, swapped, kb)],
                                          axis=1).astype(_BF16)

        vat_ref[r] = proj(C_VA, C_QB).T.astype(_BF16)
        vbt = kv[:, W_KB:].T
        per = sub // SWA_WINDOW
        for c in range(per):
            vbt_ref[r * per + c] = vbt[:, c * SWA_WINDOW:(c + 1) * SWA_WINDOW].astype(_BF16)

    hs = [normed(r) for r in range(n_sub)]
    for r in range(n_sub):
        project(r, hs[r])


def _inproj(x2, gn, w_in, gqa, gka, gqb, gkb, later_weights):
    n = x2.shape[0]
    tm = INPROJ_TILE
    steps = n // tm
    row = lambda w: pl.BlockSpec((tm, w), lambda i: (i, 0))
    slabs = [w.reshape(steps, w.shape[0] // steps, w.shape[1]) for w in later_weights]
    slab_specs = [pl.BlockSpec((1,) + s.shape[1:], lambda i: (i, 0, 0)) for s in slabs]
    out_shape = (
        jax.ShapeDtypeStruct((n, W_A), _BF16),
        jax.ShapeDtypeStruct((n, W_A), _BF16),
        jax.ShapeDtypeStruct((n, W_QB), _BF16),
        jax.ShapeDtypeStruct((n, W_KB_DUP), _BF16),
        jax.ShapeDtypeStruct((n, W_GATES), _BF16),
        jax.ShapeDtypeStruct((n // MOBA_BLOCK, W_A, MOBA_BLOCK), _BF16),
        jax.ShapeDtypeStruct((n // SWA_WINDOW, W_KB, SWA_WINDOW), _BF16),
        jax.ShapeDtypeStruct((n // tm, tm // MOBA_BLOCK, W_A), _F32),
    )
    out_specs = (
        row(W_A), row(W_A), row(W_QB), row(W_KB_DUP), row(W_GATES),
        pl.BlockSpec((tm // MOBA_BLOCK, W_A, MOBA_BLOCK), lambda i: (i, 0, 0)),
        pl.BlockSpec((tm // SWA_WINDOW, W_KB, SWA_WINDOW), lambda i: (i, 0, 0)),
        pl.BlockSpec((1, tm // MOBA_BLOCK, W_A), lambda i: (i, 0, 0)),
    )
    in_specs = [row(D_MODEL), _const_spec(gn.shape), _const_spec(w_in.shape),
                _const_spec(gqa.shape), _const_spec(gka.shape),
                _const_spec(gqb.shape), _const_spec(gkb.shape)]
    outs = pl.pallas_call(
        _inproj_kernel, grid=(steps,), in_specs=in_specs + slab_specs,
        out_specs=out_specs + tuple(slab_specs),
        out_shape=out_shape + tuple(jax.ShapeDtypeStruct(s.shape, _BF16) for s in slabs),
        compiler_params=pltpu.CompilerParams(dimension_semantics=("parallel",),
                                             vmem_limit_bytes=VMEM_LIMIT),
        name="inproj",
    )(x2, gn, w_in, gqa, gka, gqb, gkb, *slabs)
    return outs[:8], [o.reshape(w.shape) for o, w in zip(outs[8:], later_weights)]


def _moba_kernel(slopes_ref, q_ref, k_ref, vt_ref, km_ref, o_ref, kaug_sc, vaug_sc, causal_sc, s_sc):
    dyn0 = _dynamic_zero()
    blk = MOBA_BLOCK
    nb = q_ref.shape[0] // blk
    pair = 2 * HEAD_DIM
    npair = q_ref.shape[1] // pair
    first_head = pl.program_id(1) * (2 * npair)
    kp = lax.broadcasted_iota(jnp.int32, (blk, blk), 0)
    qp = lax.broadcasted_iota(jnp.int32, (blk, blk), 1)
    causal_sc[...] = jnp.where(kp <= qp, 0.0, NEG)
    lane = lax.broadcasted_iota(jnp.int32, (blk, pair), 1)
    prow = lax.broadcasted_iota(jnp.int32, (blk, pair), 0).astype(_F32)
    ridx = lax.broadcasted_iota(jnp.int32, (nb, blk), 0)
    km_lane = lax.broadcasted_iota(jnp.int32, (nb, pair), 1)
    in_head = [(lane >= e * HEAD_DIM) & (lane < (e + 1) * HEAD_DIM) for e in range(2)]

    def prepare(lp):
        lanes = slice(lp * pair, (lp + 1) * pair)
        heads = []
        for e in range(2):
            h = 2 * lp + e
            slope = slopes_ref[first_head + h]
            a = (1 - e) * HEAD_DIM
            sv = jnp.full((blk, pair), slope, _F32)
            hi = sv.astype(_BF16).astype(_F32)
            mid = (sv - hi).astype(_BF16).astype(_F32)
            lo = sv - hi - mid
            pieces = jnp.where(lane == a, hi, jnp.where(lane == a + 1, mid, jnp.where(lane == a + 2, lo, 0.0)))
            k_aug = jnp.where((lane >= a) & (lane < a + 3), prow, 0.0).astype(_BF16)
            for n in range(nb):
                rows = slice(n * blk, (n + 1) * blk)
                kaug_sc[h, rows, :] = jnp.where(in_head[e], k_ref[rows, lanes], k_aug)
                vaug_sc[h, n, :HEAD_DIM, :] = vt_ref[n, h * HEAD_DIM:(h + 1) * HEAD_DIM, :]
                vaug_sc[h, n, HEAD_DIM:, :] = jnp.ones((vaug_sc.shape[2] - HEAD_DIM, blk), _BF16)
            km_e = (km_lane >= e * HEAD_DIM) & (km_lane < (e + 1) * HEAD_DIM)
            km_head = jnp.where(km_e, km_ref[0, :, lanes], 0.0).astype(_BF16)
            heads.append((slope, pieces.astype(_BF16), km_head))
        return heads

    prepared = {}

    def scores(lp, i, e, slot):
        if lp not in prepared:
            prepared[lp] = prepare(lp)
        slope, q_aug, km_head = prepared[lp][e]
        h = 2 * lp + e
        qm = jnp.where(in_head[e], q_ref[i * blk:(i + 1) * blk, lp * pair:(lp + 1) * pair], q_aug)
        gs = lax.dot_general(km_head, qm, _NT, preferred_element_type=_F32)
        radj = []
        for n in range(i):
            row = gs[n:n + 1, :]
            ahead = ((gs > row) | ((gs == row) & (ridx < n))) & (ridx < i)
            rank = jnp.sum(ahead.astype(_F32), axis=0, keepdims=True)
            radj.append(jnp.where(rank < MOBA_TOPK, 0.0, NEG) - slope * float(blk * (i - n)))
        m = None
        for n in range(i + 1):
            t = lax.dot_general(kaug_sc[h, n * blk:(n + 1) * blk, :], qm, _NT, preferred_element_type=_F32)
            if n == i:
                t = t + causal_sc[...]
            s_sc[slot, n + dyn0] = t
            bm = jnp.max(t, axis=0, keepdims=True)
            if n < i:
                bm = bm + radj[n]
            m = bm if m is None else jnp.maximum(m, bm)
        return [m - radj[n] if n < i else m for n in range(i + 1)]

    def weighted_values(lp, i, e, slot, shifts):
        acc = None
        for n in range(i + 1):
            p = jnp.exp2(s_sc[slot, n + dyn0] - shifts[n]).astype(_BF16)
            pv = jnp.dot(vaug_sc[2 * lp + e, n], p, preferred_element_type=_F32)
            acc = pv if acc is None else acc + pv
        return acc[:HEAD_DIM] / acc[HEAD_DIM:HEAD_DIM + 1]

    order = [(lp, i, e) for lp in range(npair) for i in reversed(range(nb)) for e in range(2)]
    units = [unit + (u % s_sc.shape[0],) for u, unit in enumerate(order)]
    shifts = {u: scores(*units[u]) for u in range(MOBA_LOOKAHEAD)}
    outs = []
    for u, (lp, i, e, slot) in enumerate(units):
        ahead = u + MOBA_LOOKAHEAD
        if ahead < len(units):
            shifts[ahead] = scores(*units[ahead])
        outs.append(weighted_values(lp, i, e, slot, shifts.pop(u)))
        if e == 1:
            o = jnp.concatenate(outs, axis=0)
            o_ref[i * blk:(i + 1) * blk, lp * pair:(lp + 1) * pair] = o.astype(o_ref.dtype).T
            outs = []


def _moba(slopes, qa, ka, vat, km, batch, seq):
    nb = seq // MOBA_BLOCK
    width = MOBA_PAIRS * 2 * HEAD_DIM
    seq_spec = pl.BlockSpec((seq, width), lambda b, g: (b, g))
    return pl.pallas_call(
        _moba_kernel, grid=(batch, A_HEADS // (2 * MOBA_PAIRS)),
        in_specs=[pl.BlockSpec(memory_space=pltpu.SMEM), seq_spec, seq_spec,
                  pl.BlockSpec((nb, width, MOBA_BLOCK), lambda b, g: (b, g, 0)),
                  pl.BlockSpec((1, nb, width), lambda b, g: (b, 0, g))],
        out_specs=seq_spec,
        out_shape=jax.ShapeDtypeStruct(qa.shape, _BF16),
        scratch_shapes=[pltpu.VMEM((2 * MOBA_PAIRS, seq, 2 * HEAD_DIM), _BF16),
                        pltpu.VMEM((2 * MOBA_PAIRS, nb, HEAD_DIM + ONES_ROWS, MOBA_BLOCK), _BF16),
                        pltpu.VMEM((MOBA_BLOCK, MOBA_BLOCK), _F32),
                        pltpu.VMEM((MOBA_LOOKAHEAD + 1, nb, MOBA_BLOCK, MOBA_BLOCK), _F32)],
        compiler_params=pltpu.CompilerParams(dimension_semantics=("parallel", "parallel"),
                                             vmem_limit_bytes=VMEM_LIMIT),
        name="moba",
    )(slopes, qa, ka, vat, km)


def _swa_kernel(slopes_ref, sinks_ref, q_ref, k_ref, vt_ref, o_ref, bias_sc, vaug_sc, s_sc):
    hk = pl.program_id(1)
    dyn0 = _dynamic_zero()
    w = SWA_WINDOW
    sq = w // 2
    span = w + sq
    ncol = B_GROUP * sq
    ntile = q_ref.shape[0] // sq
    nblk = q_ref.shape[0] // w

    def per_head(col, ref):
        out = ref[hk * B_GROUP]
        for h in range(1, B_GROUP):
            out = jnp.where(col >= h * sq, ref[hk * B_GROUP + h], out)
        return out

    kp = lax.broadcasted_iota(jnp.int32, (span, ncol), 0)
    col = lax.broadcasted_iota(jnp.int32, (span, ncol), 1)
    dist = (col & (sq - 1)) + w - kp
    bias_sc[...] = jnp.where((dist >= 0) & (dist < w), -per_head(col, slopes_ref) * dist.astype(_F32), NEG)
    sink = per_head(lax.broadcasted_iota(jnp.int32, (1, ncol), 1), sinks_ref)
    lane = lax.broadcasted_iota(jnp.int32, (sq, 2 * HEAD_DIM), 1)
    first = lane < HEAD_DIM

    ones = jnp.ones((vaug_sc.shape[1] - HEAD_DIM, 2 * w), _BF16)
    for j in range(nblk):
        prev = vt_ref[j - 1] if j > 0 else jnp.zeros((HEAD_DIM, w), _BF16)
        vaug_sc[j, :HEAD_DIM, :] = jnp.concatenate([prev, vt_ref[j]], axis=1)
        vaug_sc[j, HEAD_DIM:, :] = ones

    def key_range(t):
        k0 = max(t - 2, 0) * sq
        return k0, (t + 1) * sq - k0

    def scores(t, slot):
        q_t = q_ref[t * sq:(t + 1) * sq, :]
        zero = jnp.zeros((sq, 2 * HEAD_DIM), q_t.dtype)
        stacked = []
        for pr in range(B_GROUP // 2):
            pair = q_t[:, pr * 2 * HEAD_DIM:(pr + 1) * 2 * HEAD_DIM]
            stacked += [jnp.where(first, pair, zero), jnp.where(first, zero, pair)]
        qs = jnp.concatenate(stacked, axis=0)
        k0, nk = key_range(t)
        s = lax.dot_general(k_ref[k0:k0 + nk, :], qs, _NT, preferred_element_type=_F32)
        s = s + bias_sc[span - nk:, :]
        s_sc[slot + dyn0, :nk, :] = s
        return jnp.maximum(jnp.max(s, axis=0, keepdims=True), sink)

    def weighted_values(t, slot, m):
        k0, nk = key_range(t)
        pb = jnp.exp2(s_sc[slot + dyn0, :nk, :] - m).astype(_BF16)
        before = k0 - (t // 2 - 1) * w
        after = 2 * w - before - nk
        pad = lambda rows: [jnp.zeros((rows, ncol), _BF16)] if rows else []
        p_full = jnp.concatenate(pad(before) + [pb] + pad(after), axis=0)
        ot = jnp.dot(vaug_sc[t // 2], p_full, preferred_element_type=_F32)
        den = ot[HEAD_DIM:HEAD_DIM + 1] + jnp.exp2(sink - m)
        return ot[:HEAD_DIM] / den

    def store(j, even, odd):
        even, odd = even.astype(o_ref.dtype), odd.astype(o_ref.dtype)
        for pr in range(B_GROUP // 2):
            lanes = slice(pr * 2 * HEAD_DIM, (pr + 1) * 2 * HEAD_DIM)
            e, o = even[:, lanes], odd[:, lanes]
            xa = jnp.where(first, e, pltpu.roll(o, sq, 1))
            xb = jnp.where(first, pltpu.roll(e, sq, 1), o)
            x = jnp.concatenate([xa, xb], axis=0)
            o_ref[j * w:(j + 1) * w, lanes] = x.T

    nslot = s_sc.shape[0]
    maxes = {t: scores(t, t % nslot) for t in range(SWA_LOOKAHEAD)}
    even = None
    for t in range(ntile):
        ahead = t + SWA_LOOKAHEAD
        if ahead < ntile:
            maxes[ahead] = scores(ahead, ahead % nslot)
        out = weighted_values(t, t % nslot, maxes.pop(t))
        if t % 2 == 0:
            even = out
        else:
            store(t // 2, even, out)


def _swa(slopes, sinks, qb, kb, vbt, batch, seq):
    nblk = seq // SWA_WINDOW
    grp = B_GROUP * HEAD_DIM
    span = SWA_WINDOW + SWA_WINDOW // 2
    q_spec = pl.BlockSpec((seq, grp), lambda b, hk: (b, hk))
    return pl.pallas_call(
        _swa_kernel, grid=(batch, B_KV_HEADS),
        in_specs=[pl.BlockSpec(memory_space=pltpu.SMEM), pl.BlockSpec(memory_space=pltpu.SMEM), q_spec,
                  pl.BlockSpec((seq, 2 * HEAD_DIM), lambda b, hk: (b, hk)),
                  pl.BlockSpec((nblk, HEAD_DIM, SWA_WINDOW), lambda b, hk: (b, hk, 0))],
        out_specs=q_spec,
        out_shape=jax.ShapeDtypeStruct(qb.shape, _BF16),
        scratch_shapes=[pltpu.VMEM((span, grp), _F32),
                        pltpu.VMEM((nblk, HEAD_DIM + ONES_ROWS, 2 * SWA_WINDOW), _BF16),
                        pltpu.VMEM((SWA_LOOKAHEAD + 1, span, grp), _F32)],
        compiler_params=pltpu.CompilerParams(dimension_semantics=("parallel", "parallel"),
                                             vmem_limit_bytes=VMEM_LIMIT),
        name="swa",
    )(slopes, sinks, qb, kb, vbt)


def _merge_mlp_kernel(x_ref, oa_ref, ob_ref, g_ref, wa_ref, wb_ref, wo_ref, gm_ref, wup_ref, wdn_ref, o_ref):
    a = jnp.dot(oa_ref[...], wa_ref[...], preferred_element_type=_F32)
    b = jnp.dot(ob_ref[...], wb_ref[...], preferred_element_type=_F32)
    ga = g_ref[:, :D_MODEL].astype(_F32)
    gb = g_ref[:, D_MODEL:].astype(_F32)
    mixed = jax.nn.sigmoid(ga) * a + jax.nn.sigmoid(gb) * b
    x1 = x_ref[...] + jnp.dot(mixed.astype(_BF16), wo_ref[...], preferred_element_type=_F32)
    ms = jnp.mean(x1 * x1, axis=-1, keepdims=True)
    h2 = ((x1 * lax.rsqrt(ms + EPS)) * gm_ref[...]).astype(_BF16)
    acc = x1
    for c in range(D_FF // FF_CHUNK):
        u = jnp.dot(h2, wup_ref[:, c * FF_CHUNK:(c + 1) * FF_CHUNK], preferred_element_type=_F32)
        u = jnp.square(jnp.maximum(u, 0.0)).astype(_BF16)
        acc = acc + jnp.dot(u, wdn_ref[c * FF_CHUNK:(c + 1) * FF_CHUNK, :], preferred_element_type=_F32)
    o_ref[...] = acc


def _merge_mlp(x2, oa, ob, g, wa, wb, wo, gm, wup, wdn):
    n = x2.shape[0]
    tm = MERGE_TILE
    row = lambda w: pl.BlockSpec((tm, w), lambda i: (i, 0))
    return pl.pallas_call(
        _merge_mlp_kernel, grid=(n // tm,),
        in_specs=[row(D_MODEL), row(W_A), row(W_QB), row(W_GATES), _const_spec(wa.shape), _const_spec(wb.shape),
                  _const_spec(wo.shape), _const_spec(gm.shape), _const_spec(wup.shape), _const_spec(wdn.shape)],
        out_specs=row(D_MODEL),
        out_shape=jax.ShapeDtypeStruct(x2.shape, x2.dtype),
        compiler_params=pltpu.CompilerParams(dimension_semantics=("parallel",),
                                             vmem_limit_bytes=VMEM_LIMIT),
        name="merge_mlp",
    )(x2, oa, ob, g, wa, wb, wo, gm, wup, wdn)


def _alibi_slopes(n):
    return jnp.exp2(-(8.0 / n) * jnp.arange(1, n + 1, dtype=_F32))


def kernel(x, norm_attn, w_in, q_norm_a, k_norm_a, q_norm_b, k_norm_b, sinks_b, w_branch_a, w_branch_b, w_out,
           norm_mlp, w_up, w_down):
    batch, seq, d = x.shape
    assert d == D_MODEL and (batch * seq) % INPROJ_TILE == 0 and (batch * seq) % MERGE_TILE == 0
    assert seq % MOBA_BLOCK == 0 and INPROJ_TILE % MOBA_BLOCK == 0
    slopes = _alibi_slopes(N_ATTN_HEADS)
    slopes_b, slopes_a = slopes[:B_HEADS], slopes[B_HEADS:]
    x2 = x.reshape(batch * seq, d)
    for l in range(norm_attn.shape[0]):
        tile_gain = lambda g, reps: jnp.tile(g, reps)[None, :]
        (qa, ka, qb, kb, g, vat, vbt, km), (wa, wb, wo, wup, wdn) = _inproj(
            x2, norm_attn[l][None, :], w_in[l].astype(_BF16),
            tile_gain(q_norm_a[l], A_HEADS), tile_gain(k_norm_a[l], A_HEADS),
            tile_gain(q_norm_b[l], B_HEADS), tile_gain(k_norm_b[l], B_KV_HEADS),
            (w_branch_a[l], w_branch_b[l], w_out[l], w_up[l], w_down[l]))
        km = km.reshape(batch, seq // MOBA_BLOCK, W_A)
        oa = _moba(slopes_a * LOG2E, qa, ka, vat, km, batch, seq)
        ob = _swa(slopes_b * LOG2E, sinks_b[l] * LOG2E, qb, kb, vbt, batch, seq)
        x2 = _merge_mlp(x2, oa, ob, g, wa, wb, wo, norm_mlp[l][None, :], wup, wdn)
    return x2.reshape(batch, seq, d)
```

```python
import functools

import jax
import jax.numpy as jnp
from jax import lax
from jax.experimental import pallas as pl
from jax.experimental.pallas import tpu as pltpu

D_MODEL = 1024
HEAD_DIM = 64
A_HEADS = 8
B_HEADS = 8
B_KV_HEADS = 2
B_GROUP = B_HEADS // B_KV_HEADS
N_ATTN_HEADS = A_HEADS + B_HEADS
MOBA_BLOCK = 256
MOBA_TOPK = 3
SWA_WINDOW = 128
D_FF = 4 * D_MODEL
EPS = 1e-6
NEG = -1e30
SCALE = HEAD_DIM ** -0.5
LOG2E = 1.4426950408889634

W_A = A_HEADS * HEAD_DIM
W_QB = B_HEADS * HEAD_DIM
W_KB = B_KV_HEADS * HEAD_DIM
W_KB_DUP = 2 * W_KB
W_GATES = 2 * D_MODEL

C_QA = 0
C_KA = C_QA + W_A
C_VA = C_KA + W_A
C_QB = C_VA + W_A
C_KB = C_QB + W_QB
C_VB = C_KB + W_KB
C_G = C_VB + W_KB
C_END = C_G + W_GATES

INPROJ_TILE = 1024
MERGE_TILE = 512
FF_CHUNK = 1024
ONES_ROWS = 16
MOBA_PAIRS = 4
MOBA_LOOKAHEAD = 2
SWA_LOOKAHEAD = 7
VMEM_LIMIT = 48 * 1024 * 1024

_NT = (((1,), (1,)), ((), ()))
_BF16 = jnp.bfloat16
_F32 = jnp.float32


def _const_spec(shape):
    return pl.BlockSpec(shape, lambda *_: (0,) * len(shape), pipeline_mode=pl.Buffered(1))


def _dynamic_zero():
    return jnp.minimum(pl.program_id(0), 0)


def _inproj_kernel(x_ref, gn_ref, w_ref, gqa_ref, gka_ref, gqb_ref, gkb_ref, *refs):
    n_cast = (len(refs) - 8) // 2
    qa_ref, ka_ref, qb_ref, kb_ref, g_ref, vat_ref, vbt_ref, km_ref = refs[n_cast:n_cast + 8]
    for src, dst in zip(refs[:n_cast], refs[n_cast + 8:]):
        dst[...] = src[...].astype(_BF16)

    sub = MOBA_BLOCK
    n_sub = x_ref.shape[0] // sub
    lane = lax.broadcasted_iota(jnp.int32, (sub, 2 * HEAD_DIM), 1)
    first = lane < HEAD_DIM

    def normed(r):
        x = x_ref[r * sub:(r + 1) * sub, :]
        ms = jnp.mean(x * x, axis=-1, keepdims=True)
        return ((x * lax.rsqrt(ms + EPS)) * gn_ref[...]).astype(_BF16)

    def head_norm(y, gain_ref):
        parts = []
        for c in range(0, y.shape[-1], 2 * HEAD_DIM):
            yc = y[:, c:c + 2 * HEAD_DIM]
            sq = yc * yc
            s0 = jnp.sum(jnp.where(first, sq, 0.0), axis=-1, keepdims=True)
            s1 = jnp.sum(jnp.where(first, 0.0, sq), axis=-1, keepdims=True)
            msq = jnp.where(first, s0, s1) * (1.0 / HEAD_DIM)
            parts.append(yc * lax.rsqrt(msq + EPS))
        return jnp.concatenate(parts, axis=1) * gain_ref[...]

    def project(r, h):
        rows = slice(r * sub, (r + 1) * sub)

        def proj(lo, hi):
            return jnp.dot(h, w_ref[:, lo:hi], preferred_element_type=_F32)

        qa_ref[rows, :] = (head_norm(proj(C_QA, C_KA), gqa_ref) * (SCALE * LOG2E)).astype(_BF16)
        kn = head_norm(proj(C_KA, C_VA), gka_ref)
        ka_ref[rows, :] = kn.astype(_BF16)
        km_ref[0, r:r + 1, :] = jnp.sum(kn, axis=0, keepdims=True) * (1.0 / MOBA_BLOCK)
        qb_ref[rows, :] = (head_norm(proj(C_QB, C_KB), gqb_ref) * (SCALE * LOG2E)).astype(_BF16)
        g_ref[rows, :] = proj(C_G, C_END).astype(_BF16)

        kv = proj(C_KB, C_G)
        kb = head_norm(kv[:, :W_KB], gkb_ref)
        swapped = pltpu.roll(kb, HEAD_DIM, 1)
        kb_ref[rows, :] = jnp.concatenate([jnp.where(first, kb, swapped), jnp.where(first, swapped, kb)],
                                          axis=1).astype(_BF16)

        vat_ref[r] = proj(C_VA, C_QB).T.astype(_BF16)
        vbt = kv[:, W_KB:].T
        per = sub // SWA_WINDOW
        for c in range(per):
            vbt_ref[r * per + c] = vbt[:, c * SWA_WINDOW:(c + 1) * SWA_WINDOW].astype(_BF16)

    hs = [normed(r) for r in range(n_sub)]
    for r in range(n_sub):
        project(r, hs[r])


def _inproj(x2, gn, w_in, gqa, gka, gqb, gkb, later_weights):
    n = x2.shape[0]
    tm = INPROJ_TILE
    steps = n // tm
    row = lambda w: pl.BlockSpec((tm, w), lambda i: (i, 0))
    slabs = [w.reshape(steps, w.shape[0] // steps, w.shape[1]) for w in later_weights]
    slab_specs = [pl.BlockSpec((1,) + s.shape[1:], lambda i: (i, 0, 0)) for s in slabs]
    out_shape = (
        jax.ShapeDtypeStruct((n, W_A), _BF16),
        jax.ShapeDtypeStruct((n, W_A), _BF16),
        jax.ShapeDtypeStruct((n, W_QB), _BF16),
        jax.ShapeDtypeStruct((n, W_KB_DUP), _BF16),
        jax.ShapeDtypeStruct((n, W_GATES), _BF16),
        jax.ShapeDtypeStruct((n // MOBA_BLOCK, W_A, MOBA_BLOCK), _BF16),
        jax.ShapeDtypeStruct((n // SWA_WINDOW, W_KB, SWA_WINDOW), _BF16),
        jax.ShapeDtypeStruct((n // tm, tm // MOBA_BLOCK, W_A), _F32),
    )
    out_specs = (
        row(W_A), row(W_A), row(W_QB), row(W_KB_DUP), row(W_GATES),
        pl.BlockSpec((tm // MOBA_BLOCK, W_A, MOBA_BLOCK), lambda i: (i, 0, 0)),
        pl.BlockSpec((tm // SWA_WINDOW, W_KB, SWA_WINDOW), lambda i: (i, 0, 0)),
        pl.BlockSpec((1, tm // MOBA_BLOCK, W_A), lambda i: (i, 0, 0)),
    )
    in_specs = [row(D_MODEL), _const_spec(gn.shape), _const_spec(w_in.shape),
                _const_spec(gqa.shape), _const_spec(gka.shape),
                _const_spec(gqb.shape), _const_spec(gkb.shape)]
    outs = pl.pallas_call(
        _inproj_kernel, grid=(steps,), in_specs=in_specs + slab_specs,
        out_specs=out_specs + tuple(slab_specs),
        out_shape=out_shape + tuple(jax.ShapeDtypeStruct(s.shape, _BF16) for s in slabs),
        compiler_params=pltpu.CompilerParams(dimension_semantics=("parallel",),
                                             vmem_limit_bytes=VMEM_LIMIT),
        name="inproj",
    )(x2, gn, w_in, gqa, gka, gqb, gkb, *slabs)
    return outs[:8], [o.reshape(w.shape) for o, w in zip(outs[8:], later_weights)]


def _moba_kernel(slopes_ref, q_ref, k_ref, vt_ref, km_ref, o_ref, kaug_sc, vaug_sc, causal_sc, s_sc):
    dyn0 = _dynamic_zero()
    blk = MOBA_BLOCK
    nb = q_ref.shape[0] // blk
    pair = 2 * HEAD_DIM
    npair = q_ref.shape[1] // pair
    first_head = pl.program_id(1) * (2 * npair)
    kp = lax.broadcasted_iota(jnp.int32, (blk, blk), 0)
    qp = lax.broadcasted_iota(jnp.int32, (blk, blk), 1)
    causal_sc[...] = jnp.where(kp <= qp, 0.0, NEG)
    lane = lax.broadcasted_iota(jnp.int32, (blk, pair), 1)
    prow = lax.broadcasted_iota(jnp.int32, (blk, pair), 0).astype(_F32)
    ridx = lax.broadcasted_iota(jnp.int32, (nb, blk), 0)
    km_lane = lax.broadcasted_iota(jnp.int32, (nb, pair), 1)
    in_head = [(lane >= e * HEAD_DIM) & (lane < (e + 1) * HEAD_DIM) for e in range(2)]

    def prepare(lp):
        lanes = slice(lp * pair, (lp + 1) * pair)
        heads = []
        for e in range(2):
            h = 2 * lp + e
            slope = slopes_ref[first_head + h]
            a = (1 - e) * HEAD_DIM
            sv = jnp.full((blk, pair), slope, _F32)
            hi = sv.astype(_BF16).astype(_F32)
            mid = (sv - hi).astype(_BF16).astype(_F32)
            lo = sv - hi - mid
            pieces = jnp.where(lane == a, hi, jnp.where(lane == a + 1, mid, jnp.where(lane == a + 2, lo, 0.0)))
            k_aug = jnp.where((lane >= a) & (lane < a + 3), prow, 0.0).astype(_BF16)
            for n in range(nb):
                rows = slice(n * blk, (n + 1) * blk)
                kaug_sc[h, rows, :] = jnp.where(in_head[e], k_ref[rows, lanes], k_aug)
                vaug_sc[h, n, :HEAD_DIM, :] = vt_ref[n, h * HEAD_DIM:(h + 1) * HEAD_DIM, :]
                vaug_sc[h, n, HEAD_DIM:, :] = jnp.ones((vaug_sc.shape[2] - HEAD_DIM, blk), _BF16)
            km_e = (km_lane >= e * HEAD_DIM) & (km_lane < (e + 1) * HEAD_DIM)
            km_head = jnp.where(km_e, km_ref[0, :, lanes], 0.0).astype(_BF16)
            heads.append((slope, pieces.astype(_BF16), km_head))
        return heads

    prepared = {}

    def scores(lp, i, e, slot):
        if lp not in prepared:
            prepared[lp] = prepare(lp)
        slope, q_aug, km_head = prepared[lp][e]
        h = 2 * lp + e
        qm = jnp.where(in_head[e], q_ref[i * blk:(i + 1) * blk, lp * pair:(lp + 1) * pair], q_aug)
        gs = lax.dot_general(km_head, qm, _NT, preferred_element_type=_F32)
        radj = []
        for n in range(i):
            row = gs[n:n + 1, :]
            ahead = ((gs > row) | ((gs == row) & (ridx < n))) & (ridx < i)
            rank = jnp.sum(ahead.astype(_F32), axis=0, keepdims=True)
            radj.append(jnp.where(rank < MOBA_TOPK, 0.0, NEG) - slope * float(blk * (i - n)))
        m = None
        for n in range(i + 1):
            t = lax.dot_general(kaug_sc[h, n * blk:(n + 1) * blk, :], qm, _NT, preferred_element_type=_F32)
            if n == i:
                t = t + causal_sc[...]
            s_sc[slot, n + dyn0] = t
            bm = jnp.max(t, axis=0, keepdims=True)
            if n < i:
                bm = bm + radj[n]
            m = bm if m is None else jnp.maximum(m, bm)
        return [m - radj[n] if n < i else m for n in range(i + 1)]

    def weighted_values(lp, i, e, slot, shifts):
        acc = None
        for n in range(i + 1):
            p = jnp.exp2(s_sc[slot, n + dyn0] - shifts[n]).astype(_BF16)
            pv = jnp.dot(vaug_sc[2 * lp + e, n], p, preferred_element_type=_F32)
            acc = pv if acc is None else acc + pv
        return acc[:HEAD_DIM] / acc[HEAD_DIM:HEAD_DIM + 1]

    order = [(lp, i, e) for lp in range(npair) for i in reversed(range(nb)) for e in range(2)]
    units = [unit + (u % s_sc.shape[0],) for u, unit in enumerate(order)]
    shifts = {u: scores(*units[u]) for u in range(MOBA_LOOKAHEAD)}
    outs = []
    for u, (lp, i, e, slot) in enumerate(units):
        ahead = u + MOBA_LOOKAHEAD
        if ahead < len(units):
            shifts[ahead] = scores(*units[ahead])
        outs.append(weighted_values(lp, i, e, slot, shifts.pop(u)))
        if e == 1:
            o = jnp.concatenate(outs, axis=0)
            o_ref[i * blk:(i + 1) * blk, lp * pair:(lp + 1) * pair] = o.astype(o_ref.dtype).T
            outs = []


def _moba(slopes, qa, ka, vat, km, batch, seq):
    nb = seq // MOBA_BLOCK
    width = MOBA_PAIRS * 2 * HEAD_DIM
    seq_spec = pl.BlockSpec((seq, width), lambda b, g: (b, g))
    return pl.pallas_call(
        _moba_kernel, grid=(batch, A_HEADS // (2 * MOBA_PAIRS)),
        in_specs=[pl.BlockSpec(memory_space=pltpu.SMEM), seq_spec, seq_spec,
                  pl.BlockSpec((nb, width, MOBA_BLOCK), lambda b, g: (b, g, 0)),
                  pl.BlockSpec((1, nb, width), lambda b, g: (b, 0, g))],
        out_specs=seq_spec,
        out_shape=jax.ShapeDtypeStruct(qa.shape, _BF16),
        scratch_shapes=[pltpu.VMEM((2 * MOBA_PAIRS, seq, 2 * HEAD_DIM), _BF16),
                        pltpu.VMEM((2 * MOBA_PAIRS, nb, HEAD_DIM + ONES_ROWS, MOBA_BLOCK), _BF16),
                        pltpu.VMEM((MOBA_BLOCK, MOBA_BLOCK), _F32),
                        pltpu.VMEM((MOBA_LOOKAHEAD + 1, nb, MOBA_BLOCK, MOBA_BLOCK), _F32)],
        compiler_params=pltpu.CompilerParams(dimension_semantics=("parallel", "parallel"),
                                             vmem_limit_bytes=VMEM_LIMIT),
        name="moba",
    )(slopes, qa, ka, vat, km)


def _swa_kernel(slopes_ref, sinks_ref, q_ref, k_ref, vt_ref, o_ref, bias_sc, vaug_sc, s_sc):
    hk = pl.program_id(1)
    dyn0 = _dynamic_zero()
    w = SWA_WINDOW
    sq = w // 2
    span = w + sq
    ncol = B_GROUP * sq
    ntile = q_ref.shape[0] // sq
    nblk = q_ref.shape[0] // w

    def per_head(col, ref):
        out = ref[hk * B_GROUP]
        for h in range(1, B_GROUP):
            out = jnp.where(col >= h * sq, ref[hk * B_GROUP + h], out)
        return out

    kp = lax.broadcasted_iota(jnp.int32, (span, ncol), 0)
    col = lax.broadcasted_iota(jnp.int32, (span, ncol), 1)
    dist = (col & (sq - 1)) + w - kp
    bias_sc[...] = jnp.where((dist >= 0) & (dist < w), -per_head(col, slopes_ref) * dist.astype(_F32), NEG)
    sink = per_head(lax.broadcasted_iota(jnp.int32, (1, ncol), 1), sinks_ref)
    lane = lax.broadcasted_iota(jnp.int32, (sq, 2 * HEAD_DIM), 1)
    first = lane < HEAD_DIM

    ones = jnp.ones((vaug_sc.shape[1] - HEAD_DIM, 2 * w), _BF16)
    for j in range(nblk):
        prev = vt_ref[j - 1] if j > 0 else jnp.zeros((HEAD_DIM, w), _BF16)
        vaug_sc[j, :HEAD_DIM, :] = jnp.concatenate([prev, vt_ref[j]], axis=1)
        vaug_sc[j, HEAD_DIM:, :] = ones

    def key_range(t):
        k0 = max(t - 2, 0) * sq
        return k0, (t + 1) * sq - k0

    def scores(t, slot):
        q_t = q_ref[t * sq:(t + 1) * sq, :]
        zero = jnp.zeros((sq, 2 * HEAD_DIM), q_t.dtype)
        stacked = []
        for pr in range(B_GROUP // 2):
            pair = q_t[:, pr * 2 * HEAD_DIM:(pr + 1) * 2 * HEAD_DIM]
            stacked += [jnp.where(first, pair, zero), jnp.where(first, zero, pair)]
        qs = jnp.concatenate(stacked, axis=0)
        k0, nk = key_range(t)
        s = lax.dot_general(k_ref[k0:k0 + nk, :], qs, _NT, preferred_element_type=_F32)
        s = s + bias_sc[span - nk:, :]
        s_sc[slot + dyn0, :nk, :] = s
        return jnp.maximum(jnp.max(s, axis=0, keepdims=True), sink)

    def weighted_values(t, slot, m):
        k0, nk = key_range(t)
        pb = jnp.exp2(s_sc[slot + dyn0, :nk, :] - m).astype(_BF16)
        before = k0 - (t // 2 - 1) * w
        after = 2 * w - before - nk
        pad = lambda rows: [jnp.zeros((rows, ncol), _BF16)] if rows else []
        p_full = jnp.concatenate(pad(before) + [pb] + pad(after), axis=0)
        ot = jnp.dot(vaug_sc[t // 2], p_full, preferred_element_type=_F32)
        den = ot[HEAD_DIM:HEAD_DIM + 1] + jnp.exp2(sink - m)
        return ot[:HEAD_DIM] / den

    def store(j, even, odd):
        even, odd = even.astype(o_ref.dtype), odd.astype(o_ref.dtype)
        for pr in range(B_GROUP // 2):
            lanes = slice(pr * 2 * HEAD_DIM, (pr + 1) * 2 * HEAD_DIM)
            e, o = even[:, lanes], odd[:, lanes]
            xa = jnp.where(first, e, pltpu.roll(o, sq, 1))
            xb = jnp.where(first, pltpu.roll(e, sq, 1), o)
            x = jnp.concatenate([xa, xb], axis=0)
            o_ref[j * w:(j + 1) * w, lanes] = x.T

    nslot = s_sc.shape[0]
    maxes = {t: scores(t, t % nslot) for t in range(SWA_LOOKAHEAD)}
    even = None
    for t in range(ntile):
        ahead = t + SWA_LOOKAHEAD
        if ahead < ntile:
            maxes[ahead] = scores(ahead, ahead % nslot)
        out = weighted_values(t, t % nslot, maxes.pop(t))
        if t % 2 == 0:
            even = out
        else:
            store(t // 2, even, out)


def _swa(slopes, sinks, qb, kb, vbt, batch, seq):
    nblk = seq // SWA_WINDOW
    grp = B_GROUP * HEAD_DIM
    span = SWA_WINDOW + SWA_WINDOW // 2
    q_spec = pl.BlockSpec((seq, grp), lambda b, hk: (b, hk))
    return pl.pallas_call(
        _swa_kernel, grid=(batch, B_KV_HEADS),
        in_specs=[pl.BlockSpec(memory_space=pltpu.SMEM), pl.BlockSpec(memory_space=pltpu.SMEM), q_spec,
                  pl.BlockSpec((seq, 2 * HEAD_DIM), lambda b, hk: (b, hk)),
                  pl.BlockSpec((nblk, HEAD_DIM, SWA_WINDOW), lambda b, hk: (b, hk, 0))],
        out_specs=q_spec,
        out_shape=jax.ShapeDtypeStruct(qb.shape, _BF16),
        scratch_shapes=[pltpu.VMEM((span, grp), _F32),
                        pltpu.VMEM((nblk, HEAD_DIM + ONES_ROWS, 2 * SWA_WINDOW), _BF16),
                        pltpu.VMEM((SWA_LOOKAHEAD + 1, span, grp), _F32)],
        compiler_params=pltpu.CompilerParams(dimension_semantics=("parallel", "parallel"),
                                             vmem_limit_bytes=VMEM_LIMIT),
        name="swa",
    )(slopes, sinks, qb, kb, vbt)


def _merge_mlp_kernel(x_ref, oa_ref, ob_ref, g_ref, wa_ref, wb_ref, wo_ref, gm_ref, wup_ref, wdn_ref, o_ref):
    a = jnp.dot(oa_ref[...], wa_ref[...], preferred_element_type=_F32)
    b = jnp.dot(ob_ref[...], wb_ref[...], preferred_element_type=_F32)
    ga = g_ref[:, :D_MODEL].astype(_F32)
    gb = g_ref[:, D_MODEL:].astype(_F32)
    mixed = jax.nn.sigmoid(ga) * a + jax.nn.sigmoid(gb) * b
    x1 = x_ref[...] + jnp.dot(mixed.astype(_BF16), wo_ref[...], preferred_element_type=_F32)
    ms = jnp.mean(x1 * x1, axis=-1, keepdims=True)
    h2 = ((x1 * lax.rsqrt(ms + EPS)) * gm_ref[...]).astype(_BF16)
    acc = x1
    for c in range(D_FF // FF_CHUNK):
        u = jnp.dot(h2, wup_ref[:, c * FF_CHUNK:(c + 1) * FF_CHUNK], preferred_element_type=_F32)
        u = jnp.square(jnp.maximum(u, 0.0)).astype(_BF16)
        acc = acc + jnp.dot(u, wdn_ref[c * FF_CHUNK:(c + 1) * FF_CHUNK, :], preferred_element_type=_F32)
    o_ref[...] = acc


def _merge_mlp(x2, oa, ob, g, wa, wb, wo, gm, wup, wdn):
    n = x2.shape[0]
    tm = MERGE_TILE
    row = lambda w: pl.BlockSpec((tm, w), lambda i: (i, 0))
    return pl.pallas_call(
        _merge_mlp_kernel, grid=(n // tm,),
        in_specs=[row(D_MODEL), row(W_A), row(W_QB), row(W_GATES), _const_spec(wa.shape), _const_spec(wb.shape),
                  _const_spec(wo.shape), _const_spec(gm.shape), _const_spec(wup.shape), _const_spec(wdn.shape)],
        out_specs=row(D_MODEL),
        out_shape=jax.ShapeDtypeStruct(x2.shape, x2.dtype),
        compiler_params=pltpu.CompilerParams(dimension_semantics=("parallel",),
                                             vmem_limit_bytes=VMEM_LIMIT),
        name="merge_mlp",
    )(x2, oa, ob, g, wa, wb, wo, gm, wup, wdn)


def _alibi_slopes(n):
    return jnp.exp2(-(8.0 / n) * jnp.arange(1, n + 1, dtype=_F32))


def kernel(x, norm_attn, w_in, q_norm_a, k_norm_a, q_norm_b, k_norm_b, sinks_b, w_branch_a, w_branch_b, w_out,
           norm_mlp, w_up, w_down):
    batch, seq, d = x.shape
    assert d == D_MODEL and (batch * seq) % INPROJ_TILE == 0 and (batch * seq) % MERGE_TILE == 0
    assert seq % MOBA_BLOCK == 0 and INPROJ_TILE % MOBA_BLOCK == 0
    slopes = _alibi_slopes(N_ATTN_HEADS)
    slopes_b, slopes_a = slopes[:B_HEADS], slopes[B_HEADS:]
    x2 = x.reshape(batch * seq, d)
    for l in range(norm_attn.shape[0]):
        tile_gain = lambda g, reps: jnp.tile(g, reps)[None, :]
        (qa, ka, qb, kb, g, vat, vbt, km), (wa, wb, wo, wup, wdn) = _inproj(
            x2, norm_attn[l][None, :], w_in[l].astype(_BF16),
            tile_gain(q_norm_a[l], A_HEADS), tile_gain(k_norm_a[l], A_HEADS),
            tile_gain(q_norm_b[l], B_HEADS), tile_gain(k_norm_b[l], B_KV_HEADS),
            (w_branch_a[l], w_branch_b[l], w_out[l], w_up[l], w_down[l]))
        km = km.reshape(batch, seq // MOBA_BLOCK, W_A)
        oa = _moba(slopes_a * LOG2E, qa, ka, vat, km, batch, seq)
        ob = _swa(slopes_b * LOG2E, sinks_b[l] * LOG2E, qb, kb, vbt, batch, seq)
        x2 = _merge_mlp(x2, oa, ob, g, wa, wb, wo, norm_mlp[l][None, :], wup, wdn)
    return x2.reshape(batch, seq, d)
```

```python
import functools

import jax
import jax.numpy as jnp
from jax import lax
from jax.experimental import pallas as pl
from jax.experimental.pallas import tpu as pltpu

D_MODEL = 1024
HEAD_DIM = 64
A_HEADS = 8
B_HEADS = 8
B_KV_HEADS = 2
B_GROUP = B_HEADS // B_KV_HEADS
N_ATTN_HEADS = A_HEADS + B_HEADS
MOBA_BLOCK = 256
MOBA_TOPK = 3
SWA_WINDOW = 128
D_FF = 4 * D_MODEL
EPS = 1e-6
NEG = -1e30
SCALE = HEAD_DIM ** -0.5
LOG2E = 1.4426950408889634

W_A = A_HEADS * HEAD_DIM
W_QB = B_HEADS * HEAD_DIM
W_KB = B_KV_HEADS * HEAD_DIM
W_KB_DUP = 2 * W_KB
W_GATES = 2 * D_MODEL

C_QA = 0
C_KA = C_QA + W_A
C_VA = C_KA + W_A
C_QB = C_VA + W_A
C_KB = C_QB + W_QB
C_VB = C_KB + W_KB
C_G = C_VB + W_KB
C_END = C_G + W_GATES

INPROJ_TILE = 1024
MERGE_TILE = 512
FF_CHUNK = 1024
ONES_ROWS = 16
MOBA_PAIRS = 4
MOBA_LOOKAHEAD = 2
SWA_LOOKAHEAD = 7
VMEM_LIMIT = 48 * 1024 * 1024

_NT = (((1,), (1,)), ((), ()))
_BF16 = jnp.bfloat16
_F32 = jnp.float32


def _const_spec(shape):
    return pl.BlockSpec(shape, lambda *_: (0,) * len(shape), pipeline_mode=pl.Buffered(1))


def _dynamic_zero():
    return jnp.minimum(pl.program_id(0), 0)


def _inproj_kernel(x_ref, gn_ref, w_ref, gqa_ref, gka_ref, gqb_ref, gkb_ref, *refs):
    n_cast = (len(refs) - 8) // 2
    qa_ref, ka_ref, qb_ref, kb_ref, g_ref, vat_ref, vbt_ref, km_ref = refs[n_cast:n_cast + 8]
    for src, dst in zip(refs[:n_cast], refs[n_cast + 8:]):
        dst[...] = src[...].astype(_BF16)

    sub = MOBA_BLOCK
    n_sub = x_ref.shape[0] // sub
    lane = lax.broadcasted_iota(jnp.int32, (sub, 2 * HEAD_DIM), 1)
    first = lane < HEAD_DIM

    def normed(r):
        x = x_ref[r * sub:(r + 1) * sub, :]
        ms = jnp.mean(x * x, axis=-1, keepdims=True)
        return ((x * lax.rsqrt(ms + EPS)) * gn_ref[...]).astype(_BF16)

    def head_norm(y, gain_ref):
        parts = []
        for c in range(0, y.shape[-1], 2 * HEAD_DIM):
            yc = y[:, c:c + 2 * HEAD_DIM]
            sq = yc * yc
            s0 = jnp.sum(jnp.where(first, sq, 0.0), axis=-1, keepdims=True)
            s1 = jnp.sum(jnp.where(first, 0.0, sq), axis=-1, keepdims=True)
            msq = jnp.where(first, s0, s1) * (1.0 / HEAD_DIM)
            parts.append(yc * lax.rsqrt(msq + EPS))
        return jnp.concatenate(parts, axis=1) * gain_ref[...]

    def project(r, h):
        rows = slice(r * sub, (r + 1) * sub)

        def proj(lo, hi):
            return jnp.dot(h, w_ref[:, lo:hi], preferred_element_type=_F32)

        qa_ref[rows, :] = (head_norm(proj(C_QA, C_KA), gqa_ref) * (SCALE * LOG2E)).astype(_BF16)
        kn = head_norm(proj(C_KA, C_VA), gka_ref)
        ka_ref[rows, :] = kn.astype(_BF16)
        km_ref[0, r:r + 1, :] = jnp.sum(kn, axis=0, keepdims=True) * (1.0 / MOBA_BLOCK)
        qb_ref[rows, :] = (head_norm(proj(C_QB, C_KB), gqb_ref) * (SCALE * LOG2E)).astype(_BF16)
        g_ref[rows, :] = proj(C_G, C_END).astype(_BF16)

        kv = proj(C_KB, C_G)
        kb = head_norm(kv[:, :W_KB], gkb_ref)
        swapped = pltpu.roll(kb, HEAD_DIM, 1)
        kb_ref[rows, :] = jnp.concatenate([jnp.where(first, kb, swapped), jnp.where(first, swapped, kb)],
                                          axis=1).astype(_BF16)

        vat_ref[r] = proj(C_VA, C_QB).T.astype(_BF16)
        vbt = kv[:, W_KB:].T
        per = sub // SWA_WINDOW
        for c in range(per):
            vbt_ref[r * per + c] = vbt[:, c * SWA_WINDOW:(c + 1) * SWA_WINDOW].astype(_BF16)

    hs = [normed(r) for r in range(n_sub)]
    for r in range(n_sub):
        project(r, hs[r])


def _inproj(x2, gn, w_in, gqa, gka, gqb, gkb, later_weights):
    n = x2.shape[0]
    tm = INPROJ_TILE
    steps = n // tm
    row = lambda w: pl.BlockSpec((tm, w), lambda i: (i, 0))
    slabs = [w.reshape(steps, w.shape[0] // steps, w.shape[1]) for w in later_weights]
    slab_specs = [pl.BlockSpec((1,) + s.shape[1:], lambda i: (i, 0, 0)) for s in slabs]
    out_shape = (
        jax.ShapeDtypeStruct((n, W_A), _BF16),
        jax.ShapeDtypeStruct((n, W_A), _BF16),
        jax.ShapeDtypeStruct((n, W_QB), _BF16),
        jax.ShapeDtypeStruct((n, W_KB_DUP), _BF16),
        jax.ShapeDtypeStruct((n, W_GATES), _BF16),
        jax.ShapeDtypeStruct((n // MOBA_BLOCK, W_A, MOBA_BLOCK), _BF16),
        jax.ShapeDtypeStruct((n // SWA_WINDOW, W_KB, SWA_WINDOW), _BF16),
        jax.ShapeDtypeStruct((n // tm, tm // MOBA_BLOCK, W_A), _F32),
    )
    out_specs = (
        row(W_A), row(W_A), row(W_QB), row(W_KB_DUP), row(W_GATES),
        pl.BlockSpec((tm // MOBA_BLOCK, W_A, MOBA_BLOCK), lambda i: (i, 0, 0)),
        pl.BlockSpec((tm // SWA_WINDOW, W_KB, SWA_WINDOW), lambda i: (i, 0, 0)),
        pl.BlockSpec((1, tm // MOBA_BLOCK, W_A), lambda i: (i, 0, 0)),
    )
    in_specs = [row(D_MODEL), _const_spec(gn.shape), _const_spec(w_in.shape),
                _const_spec(gqa.shape), _const_spec(gka.shape),
                _const_spec(gqb.shape), _const_spec(gkb.shape)]
    outs = pl.pallas_call(
        _inproj_kernel, grid=(steps,), in_specs=in_specs + slab_specs,
        out_specs=out_specs + tuple(slab_specs),
        out_shape=out_shape + tuple(jax.ShapeDtypeStruct(s.shape, _BF16) for s in slabs),
        compiler_params=pltpu.CompilerParams(dimension_semantics=("parallel",),
                                             vmem_limit_bytes=VMEM_LIMIT),
        name="inproj",
    )(x2, gn, w_in, gqa, gka, gqb, gkb, *slabs)
    return outs[:8], [o.reshape(w.shape) for o, w in zip(outs[8:], later_weights)]


def _moba_kernel(slopes_ref, q_ref, k_ref, vt_ref, km_ref, o_ref, kaug_sc, vaug_sc, causal_sc, s_sc):
    dyn0 = _dynamic_zero()
    blk = MOBA_BLOCK
    nb = q_ref.shape[0] // blk
    pair = 2 * HEAD_DIM
    npair = q_ref.shape[1] // pair
    first_head = pl.program_id(1) * (2 * npair)
    kp = lax.broadcasted_iota(jnp.int32, (blk, blk), 0)
    qp = lax.broadcasted_iota(jnp.int32, (blk, blk), 1)
    causal_sc[...] = jnp.where(kp <= qp, 0.0, NEG)
    lane = lax.broadcasted_iota(jnp.int32, (blk, pair), 1)
    prow = lax.broadcasted_iota(jnp.int32, (blk, pair), 0).astype(_F32)
    ridx = lax.broadcasted_iota(jnp.int32, (nb, blk), 0)
    km_lane = lax.broadcasted_iota(jnp.int32, (nb, pair), 1)
    in_head = [(lane >= e * HEAD_DIM) & (lane < (e + 1) * HEAD_DIM) for e in range(2)]

    def prepare(lp):
        lanes = slice(lp * pair, (lp + 1) * pair)
        heads = []
        for e in range(2):
            h = 2 * lp + e
            slope = slopes_ref[first_head + h]
            a = (1 - e) * HEAD_DIM
            sv = jnp.full((blk, pair), slope, _F32)
            hi = sv.astype(_BF16).astype(_F32)
            mid = (sv - hi).astype(_BF16).astype(_F32)
            lo = sv - hi - mid
            pieces = jnp.where(lane == a, hi, jnp.where(lane == a + 1, mid, jnp.where(lane == a + 2, lo, 0.0)))
            k_aug = jnp.where((lane >= a) & (lane < a + 3), prow, 0.0).astype(_BF16)
            for n in range(nb):
                rows = slice(n * blk, (n + 1) * blk)
                kaug_sc[h, rows, :] = jnp.where(in_head[e], k_ref[rows, lanes], k_aug)
                vaug_sc[h, n, :HEAD_DIM, :] = vt_ref[n, h * HEAD_DIM:(h + 1) * HEAD_DIM, :]
                vaug_sc[h, n, HEAD_DIM:, :] = jnp.ones((vaug_sc.shape[2] - HEAD_DIM, blk), _BF16)
            km_e = (km_lane >= e * HEAD_DIM) & (km_lane < (e + 1) * HEAD_DIM)
            km_head = jnp.where(km_e, km_ref[0, :, lanes], 0.0).astype(_BF16)
            heads.append((slope, pieces.astype(_BF16), km_head))
        return heads

    prepared = {}

    def scores(lp, i, e, slot):
        if lp not in prepared:
            prepared[lp] = prepare(lp)
        slope, q_aug, km_head = prepared[lp][e]
        h = 2 * lp + e
        qm = jnp.where(in_head[e], q_ref[i * blk:(i + 1) * blk, lp * pair:(lp + 1) * pair], q_aug)
        gs = lax.dot_general(km_head, qm, _NT, preferred_element_type=_F32)
        radj = []
        for n in range(i):
            row = gs[n:n + 1, :]
            ahead = ((gs > row) | ((gs == row) & (ridx < n))) & (ridx < i)
            rank = jnp.sum(ahead.astype(_F32), axis=0, keepdims=True)
            radj.append(jnp.where(rank < MOBA_TOPK, 0.0, NEG) - slope * float(blk * (i - n)))
        m = None
        for n in range(i + 1):
            t = lax.dot_general(kaug_sc[h, n * blk:(n + 1) * blk, :], qm, _NT, preferred_element_type=_F32)
            if n == i:
                t = t + causal_sc[...]
            s_sc[slot, n + dyn0] = t
            bm = jnp.max(t, axis=0, keepdims=True)
            if n < i:
                bm = bm + radj[n]
            m = bm if m is None else jnp.maximum(m, bm)
        return [m - radj[n] if n < i else m for n in range(i + 1)]

    def weighted_values(lp, i, e, slot, shifts):
        acc = None
        for n in range(i + 1):
            p = jnp.exp2((s_sc[slot, n + dyn0] - shifts[n]).astype(_BF16))
            pv = jnp.dot(vaug_sc[2 * lp + e, n], p, preferred_element_type=_F32)
            acc = pv if acc is None else acc + pv
        return acc[:HEAD_DIM] / acc[HEAD_DIM:HEAD_DIM + 1]

    order = [(lp, i, e) for lp in range(npair) for i in reversed(range(nb)) for e in range(2)]
    units = [unit + (u % s_sc.shape[0],) for u, unit in enumerate(order)]
    shifts = {u: scores(*units[u]) for u in range(MOBA_LOOKAHEAD)}
    outs = []
    for u, (lp, i, e, slot) in enumerate(units):
        ahead = u + MOBA_LOOKAHEAD
        if ahead < len(units):
            shifts[ahead] = scores(*units[ahead])
        outs.append(weighted_values(lp, i, e, slot, shifts.pop(u)))
        if e == 1:
            o = jnp.concatenate(outs, axis=0)
            o_ref[i * blk:(i + 1) * blk, lp * pair:(lp + 1) * pair] = o.astype(o_ref.dtype).T
            outs = []


def _moba(slopes, qa, ka, vat, km, batch, seq):
    nb = seq // MOBA_BLOCK
    width = MOBA_PAIRS * 2 * HEAD_DIM
    seq_spec = pl.BlockSpec((seq, width), lambda b, g: (b, g))
    return pl.pallas_call(
        _moba_kernel, grid=(batch, A_HEADS // (2 * MOBA_PAIRS)),
        in_specs=[pl.BlockSpec(memory_space=pltpu.SMEM), seq_spec, seq_spec,
                  pl.BlockSpec((nb, width, MOBA_BLOCK), lambda b, g: (b, g, 0)),
                  pl.BlockSpec((1, nb, width), lambda b, g: (b, 0, g))],
        out_specs=seq_spec,
        out_shape=jax.ShapeDtypeStruct(qa.shape, _BF16),
        scratch_shapes=[pltpu.VMEM((2 * MOBA_PAIRS, seq, 2 * HEAD_DIM), _BF16),
                        pltpu.VMEM((2 * MOBA_PAIRS, nb, HEAD_DIM + ONES_ROWS, MOBA_BLOCK), _BF16),
                        pltpu.VMEM((MOBA_BLOCK, MOBA_BLOCK), _F32),
                        pltpu.VMEM((MOBA_LOOKAHEAD + 1, nb, MOBA_BLOCK, MOBA_BLOCK), _F32)],
        compiler_params=pltpu.CompilerParams(dimension_semantics=("parallel", "parallel"),
                                             vmem_limit_bytes=VMEM_LIMIT),
        name="moba",
    )(slopes, qa, ka, vat, km)


def _swa_kernel(slopes_ref, sinks_ref, q_ref, k_ref, vt_ref, o_ref, bias_sc, vaug_sc, s_sc):
    hk = pl.program_id(1)
    dyn0 = _dynamic_zero()
    w = SWA_WINDOW
    sq = w // 2
    span = w + sq
    ncol = B_GROUP * sq
    ntile = q_ref.shape[0] // sq
    nblk = q_ref.shape[0] // w

    def per_head(col, ref):
        out = ref[hk * B_GROUP]
        for h in range(1, B_GROUP):
            out = jnp.where(col >= h * sq, ref[hk * B_GROUP + h], out)
        return out

    kp = lax.broadcasted_iota(jnp.int32, (span, ncol), 0)
    col = lax.broadcasted_iota(jnp.int32, (span, ncol), 1)
    dist = (col & (sq - 1)) + w - kp
    bias_sc[...] = jnp.where((dist >= 0) & (dist < w), -per_head(col, slopes_ref) * dist.astype(_F32), NEG)
    sink = per_head(lax.broadcasted_iota(jnp.int32, (1, ncol), 1), sinks_ref)
    lane = lax.broadcasted_iota(jnp.int32, (sq, 2 * HEAD_DIM), 1)
    first = lane < HEAD_DIM

    ones = jnp.ones((vaug_sc.shape[1] - HEAD_DIM, 2 * w), _BF16)
    for j in range(nblk):
        prev = vt_ref[j - 1] if j > 0 else jnp.zeros((HEAD_DIM, w), _BF16)
        vaug_sc[j, :HEAD_DIM, :] = jnp.concatenate([prev, vt_ref[j]], axis=1)
        vaug_sc[j, HEAD_DIM:, :] = ones

    def key_range(t):
        k0 = max(t - 2, 0) * sq
        return k0, (t + 1) * sq - k0

    def scores(t, slot):
        q_t = q_ref[t * sq:(t + 1) * sq, :]
        zero = jnp.zeros((sq, 2 * HEAD_DIM), q_t.dtype)
        stacked = []
        for pr in range(B_GROUP // 2):
            pair = q_t[:, pr * 2 * HEAD_DIM:(pr + 1) * 2 * HEAD_DIM]
            stacked += [jnp.where(first, pair, zero), jnp.where(first, zero, pair)]
        qs = jnp.concatenate(stacked, axis=0)
        k0, nk = key_range(t)
        s = lax.dot_general(k_ref[k0:k0 + nk, :], qs, _NT, preferred_element_type=_F32)
        s = s + bias_sc[span - nk:, :]
        s_sc[slot + dyn0, :nk, :] = s
        return jnp.maximum(jnp.max(s, axis=0, keepdims=True), sink)

    def weighted_values(t, slot, m):
        k0, nk = key_range(t)
        pb = jnp.exp2((s_sc[slot + dyn0, :nk, :] - m).astype(_BF16))
        before = k0 - (t // 2 - 1) * w
        after = 2 * w - before - nk
        pad = lambda rows: [jnp.zeros((rows, ncol), _BF16)] if rows else []
        p_full = jnp.concatenate(pad(before) + [pb] + pad(after), axis=0)
        ot = jnp.dot(vaug_sc[t // 2], p_full, preferred_element_type=_F32)
        den = ot[HEAD_DIM:HEAD_DIM + 1] + jnp.exp2(sink - m)
        return ot[:HEAD_DIM] / den

    def store(j, even, odd):
        even, odd = even.astype(o_ref.dtype), odd.astype(o_ref.dtype)
        for pr in range(B_GROUP // 2):
            lanes = slice(pr * 2 * HEAD_DIM, (pr + 1) * 2 * HEAD_DIM)
            e, o = even[:, lanes], odd[:, lanes]
            xa = jnp.where(first, e, pltpu.roll(o, sq, 1))
            xb = jnp.where(first, pltpu.roll(e, sq, 1), o)
            x = jnp.concatenate([xa, xb], axis=0)
            o_ref[j * w:(j + 1) * w, lanes] = x.T

    nslot = s_sc.shape[0]
    maxes = {t: scores(t, t % nslot) for t in range(SWA_LOOKAHEAD)}
    even = None
    for t in range(ntile):
        ahead = t + SWA_LOOKAHEAD
        if ahead < ntile:
            maxes[ahead] = scores(ahead, ahead % nslot)
        out = weighted_values(t, t % nslot, maxes.pop(t))
        if t % 2 == 0:
            even = out
        else:
            store(t // 2, even, out)


def _swa(slopes, sinks, qb, kb, vbt, batch, seq):
    nblk = seq // SWA_WINDOW
    grp = B_GROUP * HEAD_DIM
    span = SWA_WINDOW + SWA_WINDOW // 2
    q_spec = pl.BlockSpec((seq, grp), lambda b, hk: (b, hk))
    return pl.pallas_call(
        _swa_kernel, grid=(batch, B_KV_HEADS),
        in_specs=[pl.BlockSpec(memory_space=pltpu.SMEM), pl.BlockSpec(memory_space=pltpu.SMEM), q_spec,
                  pl.BlockSpec((seq, 2 * HEAD_DIM), lambda b, hk: (b, hk)),
                  pl.BlockSpec((nblk, HEAD_DIM, SWA_WINDOW), lambda b, hk: (b, hk, 0))],
        out_specs=q_spec,
        out_shape=jax.ShapeDtypeStruct(qb.shape, _BF16),
        scratch_shapes=[pltpu.VMEM((span, grp), _F32),
                        pltpu.VMEM((nblk, HEAD_DIM + ONES_ROWS, 2 * SWA_WINDOW), _BF16),
                        pltpu.VMEM((SWA_LOOKAHEAD + 1, span, grp), _F32)],
        compiler_params=pltpu.CompilerParams(dimension_semantics=("parallel", "parallel"),
                                             vmem_limit_bytes=VMEM_LIMIT),
        name="swa",
    )(slopes, sinks, qb, kb, vbt)


def _merge_mlp_kernel(x_ref, oa_ref, ob_ref, g_ref, wa_ref, wb_ref, wo_ref, gm_ref, wup_ref, wdn_ref, o_ref):
    a = jnp.dot(oa_ref[...], wa_ref[...], preferred_element_type=_F32)
    b = jnp.dot(ob_ref[...], wb_ref[...], preferred_element_type=_F32)
    ga = g_ref[:, :D_MODEL].astype(_F32)
    gb = g_ref[:, D_MODEL:].astype(_F32)
    mixed = jax.nn.sigmoid(ga) * a + jax.nn.sigmoid(gb) * b
    x1 = x_ref[...] + jnp.dot(mixed.astype(_BF16), wo_ref[...], preferred_element_type=_F32)
    ms = jnp.mean(x1 * x1, axis=-1, keepdims=True)
    h2 = ((x1 * lax.rsqrt(ms + EPS)) * gm_ref[...]).astype(_BF16)
    acc = x1
    for c in range(D_FF // FF_CHUNK):
        u = jnp.dot(h2, wup_ref[:, c * FF_CHUNK:(c + 1) * FF_CHUNK], preferred_element_type=_F32)
        u = jnp.square(jnp.maximum(u, 0.0)).astype(_BF16)
        acc = acc + jnp.dot(u, wdn_ref[c * FF_CHUNK:(c + 1) * FF_CHUNK, :], preferred_element_type=_F32)
    o_ref[...] = acc


def _merge_mlp(x2, oa, ob, g, wa, wb, wo, gm, wup, wdn):
    n = x2.shape[0]
    tm = MERGE_TILE
    row = lambda w: pl.BlockSpec((tm, w), lambda i: (i, 0))
    return pl.pallas_call(
        _merge_mlp_kernel, grid=(n // tm,),
        in_specs=[row(D_MODEL), row(W_A), row(W_QB), row(W_GATES), _const_spec(wa.shape), _const_spec(wb.shape),
                  _const_spec(wo.shape), _const_spec(gm.shape), _const_spec(wup.shape), _const_spec(wdn.shape)],
        out_specs=row(D_MODEL),
        out_shape=jax.ShapeDtypeStruct(x2.shape, x2.dtype),
        compiler_params=pltpu.CompilerParams(dimension_semantics=("parallel",),
                                             vmem_limit_bytes=VMEM_LIMIT),
        name="merge_mlp",
    )(x2, oa, ob, g, wa, wb, wo, gm, wup, wdn)


def _alibi_slopes(n):
    return jnp.exp2(-(8.0 / n) * jnp.arange(1, n + 1, dtype=_F32))


def kernel(x, norm_attn, w_in, q_norm_a, k_norm_a, q_norm_b, k_norm_b, sinks_b, w_branch_a, w_branch_b, w_out,
           norm_mlp, w_up, w_down):
    batch, seq, d = x.shape
    assert d == D_MODEL and (batch * seq) % INPROJ_TILE == 0 and (batch * seq) % MERGE_TILE == 0
    assert seq % MOBA_BLOCK == 0 and INPROJ_TILE % MOBA_BLOCK == 0
    slopes = _alibi_slopes(N_ATTN_HEADS)
    slopes_b, slopes_a = slopes[:B_HEADS], slopes[B_HEADS:]
    x2 = x.reshape(batch * seq, d)
    for l in range(norm_attn.shape[0]):
        tile_gain = lambda g, reps: jnp.tile(g, reps)[None, :]
        (qa, ka, qb, kb, g, vat, vbt, km), (wa, wb, wo, wup, wdn) = _inproj(
            x2, norm_attn[l][None, :], w_in[l].astype(_BF16),
            tile_gain(q_norm_a[l], A_HEADS), tile_gain(k_norm_a[l], A_HEADS),
            tile_gain(q_norm_b[l], B_HEADS), tile_gain(k_norm_b[l], B_KV_HEADS),
            (w_branch_a[l], w_branch_b[l], w_out[l], w_up[l], w_down[l]))
        km = km.reshape(batch, seq // MOBA_BLOCK, W_A)
        oa = _moba(slopes_a * LOG2E, qa, ka, vat, km, batch, seq)
        ob = _swa(slopes_b * LOG2E, sinks_b[l] * LOG2E, qb, kb, vbt, batch, seq)
        x2 = _merge_mlp(x2, oa, ob, g, wa, wb, wo, norm_mlp[l][None, :], wup, wdn)
    return x2.reshape(batch, seq, d)
```

```python
import functools

import jax
import jax.numpy as jnp
from jax import lax
from jax.experimental import pallas as pl
from jax.experimental.pallas import tpu as pltpu

D_MODEL = 1024
HEAD_DIM = 64
A_HEADS = 8
B_HEADS = 8
B_KV_HEADS = 2
B_GROUP = B_HEADS // B_KV_HEADS
N_ATTN_HEADS = A_HEADS + B_HEADS
MOBA_BLOCK = 256
MOBA_TOPK = 3
SWA_WINDOW = 128
D_FF = 4 * D_MODEL
EPS = 1e-6
NEG = -1e30
SCALE = HEAD_DIM ** -0.5
LOG2E = 1.4426950408889634

W_A = A_HEADS * HEAD_DIM
W_QB = B_HEADS * HEAD_DIM
W_KB = B_KV_HEADS * HEAD_DIM
W_KB_DUP = 2 * W_KB
W_GATES = 2 * D_MODEL

C_QA = 0
C_KA = C_QA + W_A
C_VA = C_KA + W_A
C_QB = C_VA + W_A
C_KB = C_QB + W_QB
C_VB = C_KB + W_KB
C_G = C_VB + W_KB
C_END = C_G + W_GATES

INPROJ_TILE = 1024
MERGE_TILE = 512
FF_CHUNK = 1024
ONES_ROWS = 16
MOBA_PAIRS = 4
MOBA_LOOKAHEAD = 3
SWA_LOOKAHEAD = 7
VMEM_LIMIT = 48 * 1024 * 1024

_NT = (((1,), (1,)), ((), ()))
_BF16 = jnp.bfloat16
_F32 = jnp.float32


def _const_spec(shape):
    return pl.BlockSpec(shape, lambda *_: (0,) * len(shape), pipeline_mode=pl.Buffered(1))


def _dynamic_zero():
    return jnp.minimum(pl.program_id(0), 0)


def _inproj_kernel(x_ref, gn_ref, w_ref, gqa_ref, gka_ref, gqb_ref, gkb_ref, *refs):
    n_cast = (len(refs) - 8) // 2
    qa_ref, ka_ref, qb_ref, kb_ref, g_ref, vat_ref, vbt_ref, km_ref = refs[n_cast:n_cast + 8]
    for src, dst in zip(refs[:n_cast], refs[n_cast + 8:]):
        dst[...] = src[...].astype(_BF16)

    sub = MOBA_BLOCK
    n_sub = x_ref.shape[0] // sub
    lane = lax.broadcasted_iota(jnp.int32, (sub, 2 * HEAD_DIM), 1)
    first = lane < HEAD_DIM

    def normed(r):
        x = x_ref[r * sub:(r + 1) * sub, :]
        ms = jnp.mean(x * x, axis=-1, keepdims=True)
        return ((x * lax.rsqrt(ms + EPS)) * gn_ref[...]).astype(_BF16)

    def head_norm(y, gain_ref):
        parts = []
        for c in range(0, y.shape[-1], 2 * HEAD_DIM):
            yc = y[:, c:c + 2 * HEAD_DIM]
            sq = yc * yc
            s0 = jnp.sum(jnp.where(first, sq, 0.0), axis=-1, keepdims=True)
            s1 = jnp.sum(jnp.where(first, 0.0, sq), axis=-1, keepdims=True)
            msq = jnp.where(first, s0, s1) * (1.0 / HEAD_DIM)
            parts.append(yc * lax.rsqrt(msq + EPS))
        return jnp.concatenate(parts, axis=1) * gain_ref[...]

    def project(r, h):
        rows = slice(r * sub, (r + 1) * sub)

        def proj(lo, hi):
            return jnp.dot(h, w_ref[:, lo:hi], preferred_element_type=_F32)

        qa_ref[rows, :] = (head_norm(proj(C_QA, C_KA), gqa_ref) * (SCALE * LOG2E)).astype(_BF16)
        kn = head_norm(proj(C_KA, C_VA), gka_ref)
        ka_ref[rows, :] = kn.astype(_BF16)
        km_ref[0, r:r + 1, :] = jnp.sum(kn, axis=0, keepdims=True) * (1.0 / MOBA_BLOCK)
        qb_ref[rows, :] = (head_norm(proj(C_QB, C_KB), gqb_ref) * (SCALE * LOG2E)).astype(_BF16)
        g_ref[rows, :] = proj(C_G, C_END).astype(_BF16)

        kv = proj(C_KB, C_G)
        kb = head_norm(kv[:, :W_KB], gkb_ref)
        swapped = pltpu.roll(kb, HEAD_DIM, 1)
        kb_ref[rows, :] = jnp.concatenate([jnp.where(first, kb, swapped), jnp.where(first, swapped, kb)],
                                          axis=1).astype(_BF16)

        vat_ref[r] = proj(C_VA, C_QB).T.astype(_BF16)
        vbt = kv[:, W_KB:].T
        per = sub // SWA_WINDOW
        for c in range(per):
            vbt_ref[r * per + c] = vbt[:, c * SWA_WINDOW:(c + 1) * SWA_WINDOW].astype(_BF16)

    hs = [normed(r) for r in range(n_sub)]
    for r in range(n_sub):
        project(r, hs[r])


def _inproj(x2, gn, w_in, gqa, gka, gqb, gkb, later_weights):
    n = x2.shape[0]
    tm = INPROJ_TILE
    steps = n // tm
    row = lambda w: pl.BlockSpec((tm, w), lambda i: (i, 0))
    slabs = [w.reshape(steps, w.shape[0] // steps, w.shape[1]) for w in later_weights]
    slab_specs = [pl.BlockSpec((1,) + s.shape[1:], lambda i: (i, 0, 0)) for s in slabs]
    out_shape = (
        jax.ShapeDtypeStruct((n, W_A), _BF16),
        jax.ShapeDtypeStruct((n, W_A), _BF16),
        jax.ShapeDtypeStruct((n, W_QB), _BF16),
        jax.ShapeDtypeStruct((n, W_KB_DUP), _BF16),
        jax.ShapeDtypeStruct((n, W_GATES), _BF16),
        jax.ShapeDtypeStruct((n // MOBA_BLOCK, W_A, MOBA_BLOCK), _BF16),
        jax.ShapeDtypeStruct((n // SWA_WINDOW, W_KB, SWA_WINDOW), _BF16),
        jax.ShapeDtypeStruct((n // tm, tm // MOBA_BLOCK, W_A), _F32),
    )
    out_specs = (
        row(W_A), row(W_A), row(W_QB), row(W_KB_DUP), row(W_GATES),
        pl.BlockSpec((tm // MOBA_BLOCK, W_A, MOBA_BLOCK), lambda i: (i, 0, 0)),
        pl.BlockSpec((tm // SWA_WINDOW, W_KB, SWA_WINDOW), lambda i: (i, 0, 0)),
        pl.BlockSpec((1, tm // MOBA_BLOCK, W_A), lambda i: (i, 0, 0)),
    )
    in_specs = [row(D_MODEL), _const_spec(gn.shape), _const_spec(w_in.shape),
                _const_spec(gqa.shape), _const_spec(gka.shape),
                _const_spec(gqb.shape), _const_spec(gkb.shape)]
    outs = pl.pallas_call(
        _inproj_kernel, grid=(steps,), in_specs=in_specs + slab_specs,
        out_specs=out_specs + tuple(slab_specs),
        out_shape=out_shape + tuple(jax.ShapeDtypeStruct(s.shape, _BF16) for s in slabs),
        compiler_params=pltpu.CompilerParams(dimension_semantics=("parallel",),
                                             vmem_limit_bytes=VMEM_LIMIT),
        name="inproj",
    )(x2, gn, w_in, gqa, gka, gqb, gkb, *slabs)
    return outs[:8], [o.reshape(w.shape) for o, w in zip(outs[8:], later_weights)]


def _moba_kernel(slopes_ref, q_ref, k_ref, vt_ref, km_ref, o_ref, kaug_sc, vaug_sc, causal_sc, s_sc):
    dyn0 = _dynamic_zero()
    blk = MOBA_BLOCK
    nb = q_ref.shape[0] // blk
    pair = 2 * HEAD_DIM
    npair = q_ref.shape[1] // pair
    first_head = pl.program_id(1) * (2 * npair)
    kp = lax.broadcasted_iota(jnp.int32, (blk, blk), 0)
    qp = lax.broadcasted_iota(jnp.int32, (blk, blk), 1)
    causal_sc[...] = jnp.where(kp <= qp, 0.0, NEG)
    lane = lax.broadcasted_iota(jnp.int32, (blk, pair), 1)
    prow = lax.broadcasted_iota(jnp.int32, (blk, pair), 0).astype(_F32)
    ridx = lax.broadcasted_iota(jnp.int32, (nb, blk), 0)
    km_lane = lax.broadcasted_iota(jnp.int32, (nb, pair), 1)
    in_head = [(lane >= e * HEAD_DIM) & (lane < (e + 1) * HEAD_DIM) for e in range(2)]

    def prepare(lp):
        lanes = slice(lp * pair, (lp + 1) * pair)
        heads = []
        for e in range(2):
            h = 2 * lp + e
            slope = slopes_ref[first_head + h]
            a = (1 - e) * HEAD_DIM
            sv = jnp.full((blk, pair), slope, _F32)
            hi = sv.astype(_BF16).astype(_F32)
            mid = (sv - hi).astype(_BF16).astype(_F32)
            lo = sv - hi - mid
            pieces = jnp.where(lane == a, hi, jnp.where(lane == a + 1, mid, jnp.where(lane == a + 2, lo, 0.0)))
            k_aug = jnp.where((lane >= a) & (lane < a + 3), prow, 0.0).astype(_BF16)
            for n in range(nb):
                rows = slice(n * blk, (n + 1) * blk)
                kaug_sc[h, rows, :] = jnp.where(in_head[e], k_ref[rows, lanes], k_aug)
                vaug_sc[h, n, :HEAD_DIM, :] = vt_ref[n, h * HEAD_DIM:(h + 1) * HEAD_DIM, :]
                vaug_sc[h, n, HEAD_DIM:, :] = jnp.ones((vaug_sc.shape[2] - HEAD_DIM, blk), _BF16)
            km_e = (km_lane >= e * HEAD_DIM) & (km_lane < (e + 1) * HEAD_DIM)
            km_head = jnp.where(km_e, km_ref[0, :, lanes], 0.0).astype(_BF16)
            heads.append((slope, pieces.astype(_BF16), km_head))
        return heads

    prepared = {}

    def scores(lp, i, e, slot):
        if lp not in prepared:
            prepared[lp] = prepare(lp)
        slope, q_aug, km_head = prepared[lp][e]
        h = 2 * lp + e
        qm = jnp.where(in_head[e], q_ref[i * blk:(i + 1) * blk, lp * pair:(lp + 1) * pair], q_aug)
        gs = lax.dot_general(km_head, qm, _NT, preferred_element_type=_F32)
        radj = []
        for n in range(i):
            row = gs[n:n + 1, :]
            ahead = ((gs > row) | ((gs == row) & (ridx < n))) & (ridx < i)
            rank = jnp.sum(ahead.astype(_F32), axis=0, keepdims=True)
            radj.append(jnp.where(rank < MOBA_TOPK, 0.0, NEG) - slope * float(blk * (i - n)))
        m = None
        t_all = lax.dot_general(kaug_sc[h, :(i + 1) * blk, :], qm, _NT, preferred_element_type=_F32)
        for n in range(i + 1):
            t = t_all[n * blk:(n + 1) * blk]
            if n == i:
                t = t + causal_sc[...]
            s_sc[slot, n + dyn0] = t
            bm = jnp.max(t, axis=0, keepdims=True)
            if n < i:
                bm = bm + radj[n]
            m = bm if m is None else jnp.maximum(m, bm)
        return [m - radj[n] if n < i else m for n in range(i + 1)]

    def weighted_values(lp, i, e, slot, shifts):
        acc = None
        for n in range(i + 1):
            p = jnp.exp2(s_sc[slot, n + dyn0] - shifts[n]).astype(_BF16)
            pv = jnp.dot(vaug_sc[2 * lp + e, n], p, preferred_element_type=_F32)
            acc = pv if acc is None else acc + pv
        return acc[:HEAD_DIM] / acc[HEAD_DIM:HEAD_DIM + 1]

    order = [(lp, i, e) for lp in range(npair) for i in reversed(range(nb)) for e in range(2)]
    units = [unit + (u % s_sc.shape[0],) for u, unit in enumerate(order)]
    shifts = {u: scores(*units[u]) for u in range(MOBA_LOOKAHEAD)}
    outs = []
    for u, (lp, i, e, slot) in enumerate(units):
        ahead = u + MOBA_LOOKAHEAD
        if ahead < len(units):
            shifts[ahead] = scores(*units[ahead])
        outs.append(weighted_values(lp, i, e, slot, shifts.pop(u)))
        if e == 1:
            o = jnp.concatenate(outs, axis=0)
            o_ref[i * blk:(i + 1) * blk, lp * pair:(lp + 1) * pair] = o.astype(o_ref.dtype).T
            outs = []


def _moba(slopes, qa, ka, vat, km, batch, seq):
    nb = seq // MOBA_BLOCK
    width = MOBA_PAIRS * 2 * HEAD_DIM
    seq_spec = pl.BlockSpec((seq, width), lambda b, g: (b, g))
    return pl.pallas_call(
        _moba_kernel, grid=(batch, A_HEADS // (2 * MOBA_PAIRS)),
        in_specs=[pl.BlockSpec(memory_space=pltpu.SMEM), seq_spec, seq_spec,
                  pl.BlockSpec((nb, width, MOBA_BLOCK), lambda b, g: (b, g, 0)),
                  pl.BlockSpec((1, nb, width), lambda b, g: (b, 0, g))],
        out_specs=seq_spec,
        out_shape=jax.ShapeDtypeStruct(qa.shape, _BF16),
        scratch_shapes=[pltpu.VMEM((2 * MOBA_PAIRS, seq, 2 * HEAD_DIM), _BF16),
                        pltpu.VMEM((2 * MOBA_PAIRS, nb, HEAD_DIM + ONES_ROWS, MOBA_BLOCK), _BF16),
                        pltpu.VMEM((MOBA_BLOCK, MOBA_BLOCK), _F32),
                        pltpu.VMEM((MOBA_LOOKAHEAD + 1, nb, MOBA_BLOCK, MOBA_BLOCK), _F32)],
        compiler_params=pltpu.CompilerParams(dimension_semantics=("parallel", "parallel"),
                                             vmem_limit_bytes=VMEM_LIMIT),
        name="moba",
    )(slopes, qa, ka, vat, km)


def _swa_kernel(slopes_ref, sinks_ref, q_ref, k_ref, vt_ref, o_ref, bias_sc, vaug_sc, s_sc):
    hk = pl.program_id(1)
    dyn0 = _dynamic_zero()
    w = SWA_WINDOW
    sq = w // 2
    span = w + sq
    ncol = B_GROUP * sq
    ntile = q_ref.shape[0] // sq
    nblk = q_ref.shape[0] // w

    def per_head(col, ref):
        out = ref[hk * B_GROUP]
        for h in range(1, B_GROUP):
            out = jnp.where(col >= h * sq, ref[hk * B_GROUP + h], out)
        return out

    kp = lax.broadcasted_iota(jnp.int32, (span, ncol), 0)
    col = lax.broadcasted_iota(jnp.int32, (span, ncol), 1)
    dist = (col & (sq - 1)) + w - kp
    bias_sc[...] = jnp.where((dist >= 0) & (dist < w), -per_head(col, slopes_ref) * dist.astype(_F32), NEG)
    sink = per_head(lax.broadcasted_iota(jnp.int32, (1, ncol), 1), sinks_ref)
    lane = lax.broadcasted_iota(jnp.int32, (sq, 2 * HEAD_DIM), 1)
    first = lane < HEAD_DIM

    ones = jnp.ones((vaug_sc.shape[1] - HEAD_DIM, 2 * w), _BF16)
    for j in range(nblk):
        prev = vt_ref[j - 1] if j > 0 else jnp.zeros((HEAD_DIM, w), _BF16)
        vaug_sc[j, :HEAD_DIM, :] = jnp.concatenate([prev, vt_ref[j]], axis=1)
        vaug_sc[j, HEAD_DIM:, :] = ones

    def key_range(t):
        k0 = max(t - 2, 0) * sq
        return k0, (t + 1) * sq - k0

    def scores(t, slot):
        q_t = q_ref[t * sq:(t + 1) * sq, :]
        zero = jnp.zeros((sq, 2 * HEAD_DIM), q_t.dtype)
        stacked = []
        for pr in range(B_GROUP // 2):
            pair = q_t[:, pr * 2 * HEAD_DIM:(pr + 1) * 2 * HEAD_DIM]
            stacked += [jnp.where(first, pair, zero), jnp.where(first, zero, pair)]
        qs = jnp.concatenate(stacked, axis=0)
        k0, nk = key_range(t)
        s = lax.dot_general(k_ref[k0:k0 + nk, :], qs, _NT, preferred_element_type=_F32)
        s = s + bias_sc[span - nk:, :]
        s_sc[slot + dyn0, :nk, :] = s
        return jnp.maximum(jnp.max(s, axis=0, keepdims=True), sink)

    def weighted_values(t, slot, m):
        k0, nk = key_range(t)
        pb = jnp.exp2(s_sc[slot + dyn0, :nk, :] - m).astype(_BF16)
        before = k0 - (t // 2 - 1) * w
        after = 2 * w - before - nk
        pad = lambda rows: [jnp.zeros((rows, ncol), _BF16)] if rows else []
        p_full = jnp.concatenate(pad(before) + [pb] + pad(after), axis=0)
        ot = jnp.dot(vaug_sc[t // 2], p_full, preferred_element_type=_F32)
        den = ot[HEAD_DIM:HEAD_DIM + 1] + jnp.exp2(sink - m)
        return ot[:HEAD_DIM] / den

    def store(j, even, odd):
        even, odd = even.astype(o_ref.dtype), odd.astype(o_ref.dtype)
        for pr in range(B_GROUP // 2):
            lanes = slice(pr * 2 * HEAD_DIM, (pr + 1) * 2 * HEAD_DIM)
            e, o = even[:, lanes], odd[:, lanes]
            xa = jnp.where(first, e, pltpu.roll(o, sq, 1))
            xb = jnp.where(first, pltpu.roll(e, sq, 1), o)
            x = jnp.concatenate([xa, xb], axis=0)
            o_ref[j * w:(j + 1) * w, lanes] = x.T

    nslot = s_sc.shape[0]
    maxes = {t: scores(t, t % nslot) for t in range(SWA_LOOKAHEAD)}
    even = None
    for t in range(ntile):
        ahead = t + SWA_LOOKAHEAD
        if ahead < ntile:
            maxes[ahead] = scores(ahead, ahead % nslot)
        out = weighted_values(t, t % nslot, maxes.pop(t))
        if t % 2 == 0:
            even = out
        else:
            store(t // 2, even, out)


def _swa(slopes, sinks, qb, kb, vbt, batch, seq):
    nblk = seq // SWA_WINDOW
    grp = B_GROUP * HEAD_DIM
    span = SWA_WINDOW + SWA_WINDOW // 2
    q_spec = pl.BlockSpec((seq, grp), lambda b, hk: (b, hk))
    return pl.pallas_call(
        _swa_kernel, grid=(batch, B_KV_HEADS),
        in_specs=[pl.BlockSpec(memory_space=pltpu.SMEM), pl.BlockSpec(memory_space=pltpu.SMEM), q_spec,
                  pl.BlockSpec((seq, 2 * HEAD_DIM), lambda b, hk: (b, hk)),
                  pl.BlockSpec((nblk, HEAD_DIM, SWA_WINDOW), lambda b, hk: (b, hk, 0))],
        out_specs=q_spec,
        out_shape=jax.ShapeDtypeStruct(qb.shape, _BF16),
        scratch_shapes=[pltpu.VMEM((span, grp), _F32),
                        pltpu.VMEM((nblk, HEAD_DIM + ONES_ROWS, 2 * SWA_WINDOW), _BF16),
                        pltpu.VMEM((SWA_LOOKAHEAD + 1, span, grp), _F32)],
        compiler_params=pltpu.CompilerParams(dimension_semantics=("parallel", "parallel"),
                                             vmem_limit_bytes=VMEM_LIMIT),
        name="swa",
    )(slopes, sinks, qb, kb, vbt)


def _merge_mlp_kernel(x_ref, oa_ref, ob_ref, g_ref, wa_ref, wb_ref, wo_ref, gm_ref, wup_ref, wdn_ref, o_ref):
    a = jnp.dot(oa_ref[...], wa_ref[...], preferred_element_type=_F32)
    b = jnp.dot(ob_ref[...], wb_ref[...], preferred_element_type=_F32)
    ga = g_ref[:, :D_MODEL].astype(_F32)
    gb = g_ref[:, D_MODEL:].astype(_F32)
    mixed = jax.nn.sigmoid(ga) * a + jax.nn.sigmoid(gb) * b
    x1 = x_ref[...] + jnp.dot(mixed.astype(_BF16), wo_ref[...], preferred_element_type=_F32)
    ms = jnp.mean(x1 * x1, axis=-1, keepdims=True)
    h2 = ((x1 * lax.rsqrt(ms + EPS)) * gm_ref[...]).astype(_BF16)
    acc = x1
    for c in range(D_FF // FF_CHUNK):
        u = jnp.dot(h2, wup_ref[:, c * FF_CHUNK:(c + 1) * FF_CHUNK], preferred_element_type=_F32)
        u = jnp.square(jnp.maximum(u, 0.0)).astype(_BF16)
        acc = acc + jnp.dot(u, wdn_ref[c * FF_CHUNK:(c + 1) * FF_CHUNK, :], preferred_element_type=_F32)
    o_ref[...] = acc


def _merge_mlp(x2, oa, ob, g, wa, wb, wo, gm, wup, wdn):
    n = x2.shape[0]
    tm = MERGE_TILE
    row = lambda w: pl.BlockSpec((tm, w), lambda i: (i, 0))
    return pl.pallas_call(
        _merge_mlp_kernel, grid=(n // tm,),
        in_specs=[row(D_MODEL), row(W_A), row(W_QB), row(W_GATES), _const_spec(wa.shape), _const_spec(wb.shape),
                  _const_spec(wo.shape), _const_spec(gm.shape), _const_spec(wup.shape), _const_spec(wdn.shape)],
        out_specs=row(D_MODEL),
        out_shape=jax.ShapeDtypeStruct(x2.shape, x2.dtype),
        compiler_params=pltpu.CompilerParams(dimension_semantics=("parallel",),
                                             vmem_limit_bytes=VMEM_LIMIT),
        name="merge_mlp",
    )(x2, oa, ob, g, wa, wb, wo, gm, wup, wdn)


def _alibi_slopes(n):
    return jnp.exp2(-(8.0 / n) * jnp.arange(1, n + 1, dtype=_F32))


def kernel(x, norm_attn, w_in, q_norm_a, k_norm_a, q_norm_b, k_norm_b, sinks_b, w_branch_a, w_branch_b, w_out,
           norm_mlp, w_up, w_down):
    batch, seq, d = x.shape
    assert d == D_MODEL and (batch * seq) % INPROJ_TILE == 0 and (batch * seq) % MERGE_TILE == 0
    assert seq % MOBA_BLOCK == 0 and INPROJ_TILE % MOBA_BLOCK == 0
    slopes = _alibi_slopes(N_ATTN_HEADS)
    slopes_b, slopes_a = slopes[:B_HEADS], slopes[B_HEADS:]
    x2 = x.reshape(batch * seq, d)
    for l in range(norm_attn.shape[0]):
        tile_gain = lambda g, reps: jnp.tile(g, reps)[None, :]
        (qa, ka, qb, kb, g, vat, vbt, km), (wa, wb, wo, wup, wdn) = _inproj(
            x2, norm_attn[l][None, :], w_in[l].astype(_BF16),
            tile_gain(q_norm_a[l], A_HEADS), tile_gain(k_norm_a[l], A_HEADS),
            tile_gain(q_norm_b[l], B_HEADS), tile_gain(k_norm_b[l], B_KV_HEADS),
            (w_branch_a[l], w_branch_b[l], w_out[l], w_up[l], w_down[l]))
        km = km.reshape(batch, seq // MOBA_BLOCK, W_A)
        oa = _moba(slopes_a * LOG2E, qa, ka, vat, km, batch, seq)
        ob = _swa(slopes_b * LOG2E, sinks_b[l] * LOG2E, qb, kb, vbt, batch, seq)
        x2 = _merge_mlp(x2, oa, ob, g, wa, wb, wo, norm_mlp[l][None, :], wup, wdn)
    return x2.reshape(batch, seq, d)
```

```python
import jax
import jax.numpy as jnp
from jax import lax
from jax.experimental import pallas as pl
from jax.experimental.pallas import tpu as pltpu

D_MODEL = 1024
HEAD_DIM = 64
A_HEADS = 8
B_HEADS = 8
B_KV_HEADS = 2
B_GROUP = B_HEADS // B_KV_HEADS
N_ATTN_HEADS = A_HEADS + B_HEADS
MOBA_BLOCK = 256
MOBA_TOPK = 3
SWA_WINDOW = 128
D_FF = 4 * D_MODEL
EPS = 1e-6
NEG = -1e30
SCALE = HEAD_DIM ** -0.5
LOG2E = 1.4426950408889634

W_A = A_HEADS * HEAD_DIM
W_QB = B_HEADS * HEAD_DIM
W_KB = B_KV_HEADS * HEAD_DIM
W_KB_DUP = 2 * W_KB
W_GATES = 2 * D_MODEL

C_QA = 0
C_KA = C_QA + W_A
C_VA = C_KA + W_A
C_QB = C_VA + W_A
C_KB = C_QB + W_QB
C_VB = C_KB + W_KB
C_G = C_VB + W_KB
C_END = C_G + W_GATES

N_INPROJ_OUT = 8
INPROJ_TILE = 1024
MERGE_TILE = 512
FF_CHUNK = 1024
ONES_ROWS = 16
MOBA_PAIRS = 4
MOBA_LOOKAHEAD = 3
SWA_LOOKAHEAD = 7
VMEM_LIMIT = 48 * 1024 * 1024

_NT = (((1,), (1,)), ((), ()))
_BF16 = jnp.bfloat16
_F32 = jnp.float32


def _const_spec(shape):
    return pl.BlockSpec(shape, lambda *_: (0,) * len(shape), pipeline_mode=pl.Buffered(1))


def _dynamic_zero():
    return jnp.minimum(pl.program_id(0), 0)


def _inproj_kernel(x_ref, gn_ref, w_ref, gqa_ref, gka_ref, gqb_ref, gkb_ref, *refs):
    n_cast = (len(refs) - N_INPROJ_OUT) // 2
    qa_ref, ka_ref, qb_ref, kb_ref, g_ref, vat_ref, vbt_ref, km_ref = refs[n_cast:n_cast + N_INPROJ_OUT]
    for src, dst in zip(refs[:n_cast], refs[n_cast + N_INPROJ_OUT:]):
        dst[...] = src[...].astype(_BF16)

    sub = MOBA_BLOCK
    n_sub = x_ref.shape[0] // sub
    lane = lax.broadcasted_iota(jnp.int32, (sub, 2 * HEAD_DIM), 1)
    first = lane < HEAD_DIM

    def normed(r):
        x = x_ref[r * sub:(r + 1) * sub, :]
        ms = jnp.mean(x * x, axis=-1, keepdims=True)
        return ((x * lax.rsqrt(ms + EPS)) * gn_ref[...]).astype(_BF16)

    def head_norm(y, gain_ref):
        parts = []
        for c in range(0, y.shape[-1], 2 * HEAD_DIM):
            yc = y[:, c:c + 2 * HEAD_DIM]
            sq = yc * yc
            s0 = jnp.sum(jnp.where(first, sq, 0.0), axis=-1, keepdims=True)
            s1 = jnp.sum(jnp.where(first, 0.0, sq), axis=-1, keepdims=True)
            msq = jnp.where(first, s0, s1) * (1.0 / HEAD_DIM)
            parts.append(yc * lax.rsqrt(msq + EPS))
        return jnp.concatenate(parts, axis=1) * gain_ref[...]

    def project(r, h):
        rows = slice(r * sub, (r + 1) * sub)

        def proj(lo, hi):
            return jnp.dot(h, w_ref[:, lo:hi], preferred_element_type=_F32)

        qa_ref[rows, :] = (head_norm(proj(C_QA, C_KA), gqa_ref) * (SCALE * LOG2E)).astype(_BF16)
        kn = head_norm(proj(C_KA, C_VA), gka_ref)
        ka_ref[rows, :] = kn.astype(_BF16)
        km_ref[0, r:r + 1, :] = jnp.sum(kn, axis=0, keepdims=True) * (1.0 / MOBA_BLOCK)
        qb_ref[rows, :] = (head_norm(proj(C_QB, C_KB), gqb_ref) * (SCALE * LOG2E)).astype(_BF16)
        g_ref[rows, :] = proj(C_G, C_END).astype(_BF16)

        kv = proj(C_KB, C_G)
        kb = head_norm(kv[:, :W_KB], gkb_ref)
        swapped = pltpu.roll(kb, HEAD_DIM, 1)
        kb_ref[rows, :] = jnp.concatenate([jnp.where(first, kb, swapped), jnp.where(first, swapped, kb)],
                                          axis=1).astype(_BF16)

        vat_ref[r] = proj(C_VA, C_QB).T.astype(_BF16)
        vbt = kv[:, W_KB:].T
        per = sub // SWA_WINDOW
        for c in range(per):
            vbt_ref[r * per + c] = vbt[:, c * SWA_WINDOW:(c + 1) * SWA_WINDOW].astype(_BF16)

    hs = [normed(r) for r in range(n_sub)]
    for r in range(n_sub):
        project(r, hs[r])


def _inproj(x2, gn, w_in, gqa, gka, gqb, gkb, later_weights):
    n = x2.shape[0]
    tm = INPROJ_TILE
    steps = n // tm
    row = lambda w: pl.BlockSpec((tm, w), lambda i: (i, 0))
    slabs = [w.reshape(steps, w.shape[0] // steps, w.shape[1]) for w in later_weights]
    slab_specs = [pl.BlockSpec((1,) + s.shape[1:], lambda i: (i, 0, 0)) for s in slabs]
    out_shape = (
        jax.ShapeDtypeStruct((n, W_A), _BF16),
        jax.ShapeDtypeStruct((n, W_A), _BF16),
        jax.ShapeDtypeStruct((n, W_QB), _BF16),
        jax.ShapeDtypeStruct((n, W_KB_DUP), _BF16),
        jax.ShapeDtypeStruct((n, W_GATES), _BF16),
        jax.ShapeDtypeStruct((n // MOBA_BLOCK, W_A, MOBA_BLOCK), _BF16),
        jax.ShapeDtypeStruct((n // SWA_WINDOW, W_KB, SWA_WINDOW), _BF16),
        jax.ShapeDtypeStruct((n // tm, tm // MOBA_BLOCK, W_A), _F32),
    )
    out_specs = (
        row(W_A), row(W_A), row(W_QB), row(W_KB_DUP), row(W_GATES),
        pl.BlockSpec((tm // MOBA_BLOCK, W_A, MOBA_BLOCK), lambda i: (i, 0, 0)),
        pl.BlockSpec((tm // SWA_WINDOW, W_KB, SWA_WINDOW), lambda i: (i, 0, 0)),
        pl.BlockSpec((1, tm // MOBA_BLOCK, W_A), lambda i: (i, 0, 0)),
    )
    in_specs = [row(D_MODEL), _const_spec(gn.shape), _const_spec(w_in.shape),
                _const_spec(gqa.shape), _const_spec(gka.shape),
                _const_spec(gqb.shape), _const_spec(gkb.shape)]
    outs = pl.pallas_call(
        _inproj_kernel, grid=(steps,), in_specs=in_specs + slab_specs,
        out_specs=out_specs + tuple(slab_specs),
        out_shape=out_shape + tuple(jax.ShapeDtypeStruct(s.shape, _BF16) for s in slabs),
        compiler_params=pltpu.CompilerParams(dimension_semantics=("parallel",),
                                             vmem_limit_bytes=VMEM_LIMIT),
        name="inproj",
    )(x2, gn, w_in, gqa, gka, gqb, gkb, *slabs)
    assert len(out_shape) == N_INPROJ_OUT
    return outs[:N_INPROJ_OUT], [o.reshape(w.shape) for o, w in zip(outs[N_INPROJ_OUT:], later_weights)]


def _moba_kernel(slopes_ref, q_ref, k_ref, vt_ref, km_ref, o_ref, kaug_sc, vaug_sc, causal_sc, s_sc):
    dyn0 = _dynamic_zero()
    blk = MOBA_BLOCK
    nb = q_ref.shape[0] // blk
    pair = 2 * HEAD_DIM
    npair = q_ref.shape[1] // pair
    first_head = pl.program_id(1) * (2 * npair)
    kp = lax.broadcasted_iota(jnp.int32, (blk, blk), 0)
    qp = lax.broadcasted_iota(jnp.int32, (blk, blk), 1)
    causal_sc[...] = jnp.where(kp <= qp, 0.0, NEG)
    lane = lax.broadcasted_iota(jnp.int32, (blk, pair), 1)
    prow = lax.broadcasted_iota(jnp.int32, (blk, pair), 0).astype(_F32)
    ridx = lax.broadcasted_iota(jnp.int32, (nb, blk), 0)
    km_lane = lax.broadcasted_iota(jnp.int32, (nb, pair), 1)
    in_head = [(lane >= e * HEAD_DIM) & (lane < (e + 1) * HEAD_DIM) for e in range(2)]

    def prepare(lp):
        lanes = slice(lp * pair, (lp + 1) * pair)
        heads = []
        for e in range(2):
            h = 2 * lp + e
            slope = slopes_ref[first_head + h]
            a = (1 - e) * HEAD_DIM
            sv = jnp.full((blk, pair), slope, _F32)
            hi = sv.astype(_BF16).astype(_F32)
            mid = (sv - hi).astype(_BF16).astype(_F32)
            lo = sv - hi - mid
            pieces = jnp.where(lane == a, hi, jnp.where(lane == a + 1, mid, jnp.where(lane == a + 2, lo, 0.0)))
            k_aug = jnp.where((lane >= a) & (lane < a + 3), prow, 0.0).astype(_BF16)
            for n in range(nb):
                rows = slice(n * blk, (n + 1) * blk)
                kaug_sc[h, rows, :] = jnp.where(in_head[e], k_ref[rows, lanes], k_aug)
                vaug_sc[h, n, :HEAD_DIM, :] = vt_ref[n, h * HEAD_DIM:(h + 1) * HEAD_DIM, :]
                vaug_sc[h, n, HEAD_DIM:, :] = jnp.ones((vaug_sc.shape[2] - HEAD_DIM, blk), _BF16)
            km_e = (km_lane >= e * HEAD_DIM) & (km_lane < (e + 1) * HEAD_DIM)
            km_head = jnp.where(km_e, km_ref[0, :, lanes], 0.0).astype(_BF16)
            heads.append((slope, pieces.astype(_BF16), km_head))
        return heads

    prepared = {}

    def scores(lp, i, e, slot):
        if lp not in prepared:
            prepared[lp] = prepare(lp)
        slope, q_aug, km_head = prepared[lp][e]
        h = 2 * lp + e
        qm = jnp.where(in_head[e], q_ref[i * blk:(i + 1) * blk, lp * pair:(lp + 1) * pair], q_aug)
        gs = lax.dot_general(km_head, qm, _NT, preferred_element_type=_F32)
        radj = []
        for n in range(i):
            row = gs[n:n + 1, :]
            ahead = ((gs > row) | ((gs == row) & (ridx < n))) & (ridx < i)
            rank = jnp.sum(ahead.astype(_F32), axis=0, keepdims=True)
            radj.append(jnp.where(rank < MOBA_TOPK, 0.0, NEG) - slope * float(blk * (i - n)))
        m = None
        for n in range(i + 1):
            t = lax.dot_general(kaug_sc[h, n * blk:(n + 1) * blk, :], qm, _NT, preferred_element_type=_F32)
            if n == i:
                t = t + causal_sc[...]
            s_sc[slot, n + dyn0] = t
            bm = jnp.max(t, axis=0, keepdims=True)
            if n < i:
                bm = bm + radj[n]
            m = bm if m is None else jnp.maximum(m, bm)
        return [m - radj[n] if n < i else m for n in range(i + 1)]

    def weighted_values(lp, i, e, slot, shifts):
        acc = None
        for n in range(i + 1):
            p = jnp.exp2(s_sc[slot, n + dyn0] - shifts[n]).astype(_BF16)
            pv = jnp.dot(vaug_sc[2 * lp + e, n], p, preferred_element_type=_F32)
            acc = pv if acc is None else acc + pv
        return acc[:HEAD_DIM] / acc[HEAD_DIM:HEAD_DIM + 1]

    order = [(lp, i, e) for lp in range(npair) for i in reversed(range(nb)) for e in range(2)]
    units = [unit + (u % s_sc.shape[0],) for u, unit in enumerate(order)]
    shifts = {u: scores(*units[u]) for u in range(MOBA_LOOKAHEAD)}
    outs = []
    for u, (lp, i, e, slot) in enumerate(units):
        ahead = u + MOBA_LOOKAHEAD
        if ahead < len(units):
            shifts[ahead] = scores(*units[ahead])
        outs.append(weighted_values(lp, i, e, slot, shifts.pop(u)))
        if e == 1:
            o = jnp.concatenate(outs, axis=0)
            o_ref[i * blk:(i + 1) * blk, lp * pair:(lp + 1) * pair] = o.astype(o_ref.dtype).T
            outs = []


def _moba(slopes, qa, ka, vat, km, batch, seq):
    nb = seq // MOBA_BLOCK
    width = MOBA_PAIRS * 2 * HEAD_DIM
    seq_spec = pl.BlockSpec((seq, width), lambda b, g: (b, g))
    return pl.pallas_call(
        _moba_kernel, grid=(batch, A_HEADS // (2 * MOBA_PAIRS)),
        in_specs=[pl.BlockSpec(memory_space=pltpu.SMEM), seq_spec, seq_spec,
                  pl.BlockSpec((nb, width, MOBA_BLOCK), lambda b, g: (b, g, 0)),
                  pl.BlockSpec((1, nb, width), lambda b, g: (b, 0, g))],
        out_specs=seq_spec,
        out_shape=jax.ShapeDtypeStruct(qa.shape, _BF16),
        scratch_shapes=[pltpu.VMEM((2 * MOBA_PAIRS, seq, 2 * HEAD_DIM), _BF16),
                        pltpu.VMEM((2 * MOBA_PAIRS, nb, HEAD_DIM + ONES_ROWS, MOBA_BLOCK), _BF16),
                        pltpu.VMEM((MOBA_BLOCK, MOBA_BLOCK), _F32),
                        pltpu.VMEM((MOBA_LOOKAHEAD + 1, nb, MOBA_BLOCK, MOBA_BLOCK), _F32)],
        compiler_params=pltpu.CompilerParams(dimension_semantics=("parallel", "parallel"),
                                             vmem_limit_bytes=VMEM_LIMIT),
        name="moba",
    )(slopes, qa, ka, vat, km)


def _swa_kernel(slopes_ref, sinks_ref, q_ref, k_ref, vt_ref, o_ref, bias_sc, vaug_sc, s_sc):
    dyn0 = _dynamic_zero()
    w = SWA_WINDOW
    sq = w // 2
    span = w + sq
    ncol = B_GROUP * sq
    ntile = q_ref.shape[0] // sq
    nblk = q_ref.shape[0] // w

    def per_head(hk, col, ref):
        out = ref[hk * B_GROUP]
        for h in range(1, B_GROUP):
            out = jnp.where(col >= h * sq, ref[hk * B_GROUP + h], out)
        return out

    kp = lax.broadcasted_iota(jnp.int32, (span, ncol), 0)
    col = lax.broadcasted_iota(jnp.int32, (span, ncol), 1)
    dist = (col & (sq - 1)) + w - kp
    lane = lax.broadcasted_iota(jnp.int32, (sq, 2 * HEAD_DIM), 1)
    first = lane < HEAD_DIM
    ones = jnp.ones((vaug_sc.shape[2] - HEAD_DIM, 2 * w), _BF16)

    def prepare(hk):
        slope = per_head(hk, col, slopes_ref)
        bias_sc[hk] = jnp.where((dist >= 0) & (dist < w), -slope * dist.astype(_F32), NEG)
        v_rows = slice(hk * HEAD_DIM, (hk + 1) * HEAD_DIM)
        for j in range(nblk):
            prev = vt_ref[j - 1, v_rows, :] if j > 0 else jnp.zeros((HEAD_DIM, w), _BF16)
            vaug_sc[hk, j, :HEAD_DIM, :] = jnp.concatenate([prev, vt_ref[j, v_rows, :]], axis=1)
            vaug_sc[hk, j, HEAD_DIM:, :] = ones
        return per_head(hk, lax.broadcasted_iota(jnp.int32, (1, ncol), 1), sinks_ref)

    sinks = {}

    def key_range(t):
        k0 = max(t - 2, 0) * sq
        return k0, (t + 1) * sq - k0

    def scores(hk, t, slot):
        if hk not in sinks:
            sinks[hk] = prepare(hk)
        q_t = q_ref[t * sq:(t + 1) * sq, hk * ncol:(hk + 1) * ncol]
        zero = jnp.zeros((sq, 2 * HEAD_DIM), q_t.dtype)
        stacked = []
        for pr in range(B_GROUP // 2):
            pair = q_t[:, pr * 2 * HEAD_DIM:(pr + 1) * 2 * HEAD_DIM]
            stacked += [jnp.where(first, pair, zero), jnp.where(first, zero, pair)]
        qs = jnp.concatenate(stacked, axis=0)
        k0, nk = key_range(t)
        keys = k_ref[k0:k0 + nk, hk * 2 * HEAD_DIM:(hk + 1) * 2 * HEAD_DIM]
        s = lax.dot_general(keys, qs, _NT, preferred_element_type=_F32)
        s = s + bias_sc[hk, span - nk:, :]
        s_sc[slot + dyn0, :nk, :] = s
        return jnp.maximum(jnp.max(s, axis=0, keepdims=True), sinks[hk])

    def weighted_values(hk, t, slot, m):
        k0, nk = key_range(t)
        pb = jnp.exp2(s_sc[slot + dyn0, :nk, :] - m).astype(_BF16)
        before = k0 - (t // 2 - 1) * w
        after = 2 * w - before - nk
        pad = lambda rows: [jnp.zeros((rows, ncol), _BF16)] if rows else []
        p_full = jnp.concatenate(pad(before) + [pb] + pad(after), axis=0)
        ot = jnp.dot(vaug_sc[hk, t // 2], p_full, preferred_element_type=_F32)
        den = ot[HEAD_DIM:HEAD_DIM + 1] + jnp.exp2(sinks[hk] - m)
        return ot[:HEAD_DIM] / den

    def store(hk, j, even, odd):
        even, odd = even.astype(o_ref.dtype), odd.astype(o_ref.dtype)
        for pr in range(B_GROUP // 2):
            lanes = slice(pr * 2 * HEAD_DIM, (pr + 1) * 2 * HEAD_DIM)
            e, o = even[:, lanes], odd[:, lanes]
            xa = jnp.where(first, e, pltpu.roll(o, sq, 1))
            xb = jnp.where(first, pltpu.roll(e, sq, 1), o)
            x = jnp.concatenate([xa, xb], axis=0)
            lane0 = hk * ncol + pr * 2 * HEAD_DIM
            o_ref[j * w:(j + 1) * w, lane0:lane0 + 2 * HEAD_DIM] = x.T

    nslot = s_sc.shape[0]
    tiles = [(hk, t) for hk in range(q_ref.shape[1] // ncol) for t in range(ntile)]
    maxes = {u: scores(*tiles[u], u % nslot) for u in range(SWA_LOOKAHEAD)}
    even = None
    for u, (hk, t) in enumerate(tiles):
        ahead = u + SWA_LOOKAHEAD
        if ahead < len(tiles):
            maxes[ahead] = scores(*tiles[ahead], ahead % nslot)
        out = weighted_values(hk, t, u % nslot, maxes.pop(u))
        if t % 2 == 0:
            even = out
        else:
            store(hk, t // 2, even, out)


def _swa(slopes, sinks, qb, kb, vbt, batch, seq):
    nblk = seq // SWA_WINDOW
    grp = B_GROUP * HEAD_DIM
    span = SWA_WINDOW + SWA_WINDOW // 2
    q_spec = pl.BlockSpec((seq, W_QB), lambda b: (b, 0))
    return pl.pallas_call(
        _swa_kernel, grid=(batch,),
        in_specs=[pl.BlockSpec(memory_space=pltpu.SMEM), pl.BlockSpec(memory_space=pltpu.SMEM), q_spec,
                  pl.BlockSpec((seq, W_KB_DUP), lambda b: (b, 0)),
                  pl.BlockSpec((nblk, W_KB, SWA_WINDOW), lambda b: (b, 0, 0))],
        out_specs=q_spec,
        out_shape=jax.ShapeDtypeStruct(qb.shape, _BF16),
        scratch_shapes=[pltpu.VMEM((B_KV_HEADS, span, grp), _F32),
                        pltpu.VMEM((B_KV_HEADS, nblk, HEAD_DIM + ONES_ROWS, 2 * SWA_WINDOW), _BF16),
                        pltpu.VMEM((SWA_LOOKAHEAD + 1, span, grp), _F32)],
        compiler_params=pltpu.CompilerParams(dimension_semantics=("parallel",),
                                             vmem_limit_bytes=VMEM_LIMIT),
        name="swa",
    )(slopes, sinks, qb, kb, vbt)


def _merge_mlp_kernel(x_ref, oa_ref, ob_ref, g_ref, wa_ref, wb_ref, wo_ref, gm_ref, wup_ref, wdn_ref, o_ref):
    a = jnp.dot(oa_ref[...], wa_ref[...], preferred_element_type=_F32)
    b = jnp.dot(ob_ref[...], wb_ref[...], preferred_element_type=_F32)
    ga = g_ref[:, :D_MODEL].astype(_F32)
    gb = g_ref[:, D_MODEL:].astype(_F32)
    mixed = jax.nn.sigmoid(ga) * a + jax.nn.sigmoid(gb) * b
    x1 = x_ref[...] + jnp.dot(mixed.astype(_BF16), wo_ref[...], preferred_element_type=_F32)
    ms = jnp.mean(x1 * x1, axis=-1, keepdims=True)
    h2 = ((x1 * lax.rsqrt(ms + EPS)) * gm_ref[...]).astype(_BF16)
    acc = x1
    for c in range(D_FF // FF_CHUNK):
        u = jnp.dot(h2, wup_ref[:, c * FF_CHUNK:(c + 1) * FF_CHUNK], preferred_element_type=_F32)
        u = jnp.square(jnp.maximum(u, 0.0)).astype(_BF16)
        acc = acc + jnp.dot(u, wdn_ref[c * FF_CHUNK:(c + 1) * FF_CHUNK, :], preferred_element_type=_F32)
    o_ref[...] = acc


def _merge_mlp(x2, oa, ob, g, wa, wb, wo, gm, wup, wdn):
    n = x2.shape[0]
    tm = MERGE_TILE
    row = lambda w: pl.BlockSpec((tm, w), lambda i: (i, 0))
    return pl.pallas_call(
        _merge_mlp_kernel, grid=(n // tm,),
        in_specs=[row(D_MODEL), row(W_A), row(W_QB), row(W_GATES), _const_spec(wa.shape), _const_spec(wb.shape),
                  _const_spec(wo.shape), _const_spec(gm.shape), _const_spec(wup.shape), _const_spec(wdn.shape)],
        out_specs=row(D_MODEL),
        out_shape=jax.ShapeDtypeStruct(x2.shape, x2.dtype),
        compiler_params=pltpu.CompilerParams(dimension_semantics=("parallel",),
                                             vmem_limit_bytes=VMEM_LIMIT),
        name="merge_mlp",
    )(x2, oa, ob, g, wa, wb, wo, gm, wup, wdn)


def _alibi_slopes(n):
    return jnp.exp2(-(8.0 / n) * jnp.arange(1, n + 1, dtype=_F32))


def kernel(x, norm_attn, w_in, q_norm_a, k_norm_a, q_norm_b, k_norm_b, sinks_b, w_branch_a, w_branch_b, w_out,
           norm_mlp, w_up, w_down):
    batch, seq, d = x.shape
    assert d == D_MODEL and (batch * seq) % INPROJ_TILE == 0 and (batch * seq) % MERGE_TILE == 0
    assert seq % MOBA_BLOCK == 0 and INPROJ_TILE % MOBA_BLOCK == 0
    slopes = _alibi_slopes(N_ATTN_HEADS)
    slopes_b, slopes_a = slopes[:B_HEADS], slopes[B_HEADS:]
    x2 = x.reshape(batch * seq, d)
    for l in range(norm_attn.shape[0]):
        tile_gain = lambda g, reps: jnp.tile(g, reps)[None, :]
        (qa, ka, qb, kb, g, vat, vbt, km), (wa, wb, wo, wup, wdn) = _inproj(
            x2, norm_attn[l][None, :], w_in[l].astype(_BF16),
            tile_gain(q_norm_a[l], A_HEADS), tile_gain(k_norm_a[l], A_HEADS),
            tile_gain(q_norm_b[l], B_HEADS), tile_gain(k_norm_b[l], B_KV_HEADS),
            (w_branch_a[l], w_branch_b[l], w_out[l], w_up[l], w_down[l]))
        km = km.reshape(batch, seq // MOBA_BLOCK, W_A)
        oa = _moba(slopes_a * LOG2E, qa, ka, vat, km, batch, seq)
        ob = _swa(slopes_b * LOG2E, sinks_b[l] * LOG2E, qb, kb, vbt, batch, seq)
        x2 = _merge_mlp(x2, oa, ob, g, wa, wb, wo, norm_mlp[l][None, :], wup, wdn)
    return x2.reshape(batch, seq, d)
```

```python
import jax
import jax.numpy as jnp
from jax import lax
from jax.experimental import pallas as pl
from jax.experimental.pallas import tpu as pltpu

D_MODEL = 1024
HEAD_DIM = 64
A_HEADS = 8
B_HEADS = 8
B_KV_HEADS = 2
B_GROUP = B_HEADS // B_KV_HEADS
N_ATTN_HEADS = A_HEADS + B_HEADS
MOBA_BLOCK = 256
MOBA_TOPK = 3
SWA_WINDOW = 128
D_FF = 4 * D_MODEL
EPS = 1e-6
NEG = -1e30
SCALE = HEAD_DIM ** -0.5
LOG2E = 1.4426950408889634

W_A = A_HEADS * HEAD_DIM
W_QB = B_HEADS * HEAD_DIM
W_KB = B_KV_HEADS * HEAD_DIM
W_KB_DUP = 2 * W_KB
W_GATES = 2 * D_MODEL

C_QA = 0
C_KA = C_QA + W_A
C_VA = C_KA + W_A
C_QB = C_VA + W_A
C_KB = C_QB + W_QB
C_VB = C_KB + W_KB
C_G = C_VB + W_KB
C_END = C_G + W_GATES

N_INPROJ_OUT = 8
INPROJ_TILE = 1024
MERGE_TILE = 512
FF_CHUNK = 1024
ONES_ROWS = 16
ATTN_LOOKAHEAD = 6
VMEM_LIMIT = 48 * 1024 * 1024
ATTN_VMEM_LIMIT = 56 * 1024 * 1024

_NT = (((1,), (1,)), ((), ()))
_BF16 = jnp.bfloat16
_F32 = jnp.float32


def _const_spec(shape):
    return pl.BlockSpec(shape, lambda *_: (0,) * len(shape), pipeline_mode=pl.Buffered(1))


def _dynamic_zero():
    return jnp.minimum(pl.program_id(0), 0)


def _inproj_kernel(x_ref, gn_ref, w_ref, gqa_ref, gka_ref, gqb_ref, gkb_ref, *refs):
    n_cast = (len(refs) - N_INPROJ_OUT) // 2
    qa_ref, ka_ref, qb_ref, kb_ref, g_ref, vat_ref, vbt_ref, km_ref = refs[n_cast:n_cast + N_INPROJ_OUT]
    for src, dst in zip(refs[:n_cast], refs[n_cast + N_INPROJ_OUT:]):
        dst[...] = src[...].astype(_BF16)

    sub = MOBA_BLOCK
    n_sub = x_ref.shape[0] // sub
    lane = lax.broadcasted_iota(jnp.int32, (sub, 2 * HEAD_DIM), 1)
    first = lane < HEAD_DIM

    def normed(r):
        x = x_ref[r * sub:(r + 1) * sub, :]
        ms = jnp.mean(x * x, axis=-1, keepdims=True)
        return ((x * lax.rsqrt(ms + EPS)) * gn_ref[...]).astype(_BF16)

    def head_norm(y, gain_ref):
        parts = []
        for c in range(0, y.shape[-1], 2 * HEAD_DIM):
            yc = y[:, c:c + 2 * HEAD_DIM]
            sq = yc * yc
            s0 = jnp.sum(jnp.where(first, sq, 0.0), axis=-1, keepdims=True)
            s1 = jnp.sum(jnp.where(first, 0.0, sq), axis=-1, keepdims=True)
            msq = jnp.where(first, s0, s1) * (1.0 / HEAD_DIM)
            parts.append(yc * lax.rsqrt(msq + EPS))
        return jnp.concatenate(parts, axis=1) * gain_ref[...]

    def project(r, h):
        rows = slice(r * sub, (r + 1) * sub)

        def proj(lo, hi):
            return jnp.dot(h, w_ref[:, lo:hi], preferred_element_type=_F32)

        qa_ref[rows, :] = (head_norm(proj(C_QA, C_KA), gqa_ref) * (SCALE * LOG2E)).astype(_BF16)
        kn = head_norm(proj(C_KA, C_VA), gka_ref)
        ka_ref[rows, :] = kn.astype(_BF16)
        km_ref[0, r:r + 1, :] = jnp.sum(kn, axis=0, keepdims=True) * (1.0 / MOBA_BLOCK)
        qb_ref[rows, :] = (head_norm(proj(C_QB, C_KB), gqb_ref) * (SCALE * LOG2E)).astype(_BF16)
        g_ref[rows, :] = proj(C_G, C_END).astype(_BF16)

        kv = proj(C_KB, C_G)
        kb = head_norm(kv[:, :W_KB], gkb_ref)
        swapped = pltpu.roll(kb, HEAD_DIM, 1)
        kb_ref[rows, :] = jnp.concatenate([jnp.where(first, kb, swapped), jnp.where(first, swapped, kb)],
                                          axis=1).astype(_BF16)

        vat_ref[r] = proj(C_VA, C_QB).T.astype(_BF16)
        vbt = kv[:, W_KB:].T
        per = sub // SWA_WINDOW
        for c in range(per):
            vbt_ref[r * per + c] = vbt[:, c * SWA_WINDOW:(c + 1) * SWA_WINDOW].astype(_BF16)

    hs = [normed(r) for r in range(n_sub)]
    for r in range(n_sub):
        project(r, hs[r])


def _inproj(x2, gn, w_in, gqa, gka, gqb, gkb, later_weights):
    n = x2.shape[0]
    tm = INPROJ_TILE
    steps = n // tm
    row = lambda w: pl.BlockSpec((tm, w), lambda i: (i, 0))
    slabs = [w.reshape(steps, w.shape[0] // steps, w.shape[1]) for w in later_weights]
    slab_specs = [pl.BlockSpec((1,) + s.shape[1:], lambda i: (i, 0, 0)) for s in slabs]
    out_shape = (
        jax.ShapeDtypeStruct((n, W_A), _BF16),
        jax.ShapeDtypeStruct((n, W_A), _BF16),
        jax.ShapeDtypeStruct((n, W_QB), _BF16),
        jax.ShapeDtypeStruct((n, W_KB_DUP), _BF16),
        jax.ShapeDtypeStruct((n, W_GATES), _BF16),
        jax.ShapeDtypeStruct((n // MOBA_BLOCK, W_A, MOBA_BLOCK), _BF16),
        jax.ShapeDtypeStruct((n // SWA_WINDOW, W_KB, SWA_WINDOW), _BF16),
        jax.ShapeDtypeStruct((n // tm, tm // MOBA_BLOCK, W_A), _F32),
    )
    out_specs = (
        row(W_A), row(W_A), row(W_QB), row(W_KB_DUP), row(W_GATES),
        pl.BlockSpec((tm // MOBA_BLOCK, W_A, MOBA_BLOCK), lambda i: (i, 0, 0)),
        pl.BlockSpec((tm // SWA_WINDOW, W_KB, SWA_WINDOW), lambda i: (i, 0, 0)),
        pl.BlockSpec((1, tm // MOBA_BLOCK, W_A), lambda i: (i, 0, 0)),
    )
    in_specs = [row(D_MODEL), _const_spec(gn.shape), _const_spec(w_in.shape),
                _const_spec(gqa.shape), _const_spec(gka.shape),
                _const_spec(gqb.shape), _const_spec(gkb.shape)]
    outs = pl.pallas_call(
        _inproj_kernel, grid=(steps,), in_specs=in_specs + slab_specs,
        out_specs=out_specs + tuple(slab_specs),
        out_shape=out_shape + tuple(jax.ShapeDtypeStruct(s.shape, _BF16) for s in slabs),
        compiler_params=pltpu.CompilerParams(dimension_semantics=("parallel",),
                                             vmem_limit_bytes=VMEM_LIMIT),
        name="inproj",
    )(x2, gn, w_in, gqa, gka, gqb, gkb, *slabs)
    assert len(out_shape) == N_INPROJ_OUT
    return outs[:N_INPROJ_OUT], [o.reshape(w.shape) for o, w in zip(outs[N_INPROJ_OUT:], later_weights)]


def _moba_items(slopes_ref, q_ref, k_ref, vt_ref, km_ref, o_ref, kaug_sc, vaug_sc, causal_sc, s_sc, dyn0):
    blk = MOBA_BLOCK
    nb = q_ref.shape[0] // blk
    pair = 2 * HEAD_DIM
    npair = q_ref.shape[1] // pair
    kp = lax.broadcasted_iota(jnp.int32, (blk, blk), 0)
    qp = lax.broadcasted_iota(jnp.int32, (blk, blk), 1)
    causal_sc[...] = jnp.where(kp <= qp, 0.0, NEG)
    lane = lax.broadcasted_iota(jnp.int32, (blk, pair), 1)
    prow = lax.broadcasted_iota(jnp.int32, (blk, pair), 0).astype(_F32)
    ridx = lax.broadcasted_iota(jnp.int32, (nb, blk), 0)
    km_lane = lax.broadcasted_iota(jnp.int32, (nb, pair), 1)
    in_head = [(lane >= e * HEAD_DIM) & (lane < (e + 1) * HEAD_DIM) for e in range(2)]

    def prepare(lp):
        lanes = slice(lp * pair, (lp + 1) * pair)
        heads = []
        for e in range(2):
            h = 2 * lp + e
            slope = slopes_ref[h]
            a = (1 - e) * HEAD_DIM
            sv = jnp.full((blk, pair), slope, _F32)
            hi = sv.astype(_BF16).astype(_F32)
            mid = (sv - hi).astype(_BF16).astype(_F32)
            lo = sv - hi - mid
            pieces = jnp.where(lane == a, hi, jnp.where(lane == a + 1, mid, jnp.where(lane == a + 2, lo, 0.0)))
            k_aug = jnp.where((lane >= a) & (lane < a + 3), prow, 0.0).astype(_BF16)
            for n in range(nb):
                rows = slice(n * blk, (n + 1) * blk)
                kaug_sc[h, rows, :] = jnp.where(in_head[e], k_ref[rows, lanes], k_aug)
                vaug_sc[h, n, :HEAD_DIM, :] = vt_ref[n, h * HEAD_DIM:(h + 1) * HEAD_DIM, :]
                vaug_sc[h, n, HEAD_DIM:, :] = jnp.ones((vaug_sc.shape[2] - HEAD_DIM, blk), _BF16)
            km_e = (km_lane >= e * HEAD_DIM) & (km_lane < (e + 1) * HEAD_DIM)
            km_head = jnp.where(km_e, km_ref[0, :, lanes], 0.0).astype(_BF16)
            heads.append((slope, pieces.astype(_BF16), km_head))
        return heads

    prepared = {}

    def scores(lp, i, e, slot):
        if lp not in prepared:
            prepared[lp] = prepare(lp)
        slope, q_aug, km_head = prepared[lp][e]
        h = 2 * lp + e
        qm = jnp.where(in_head[e], q_ref[i * blk:(i + 1) * blk, lp * pair:(lp + 1) * pair], q_aug)
        gs = lax.dot_general(km_head, qm, _NT, preferred_element_type=_F32)
        radj = []
        for n in range(i):
            row = gs[n:n + 1, :]
            ahead = ((gs > row) | ((gs == row) & (ridx < n))) & (ridx < i)
            rank = jnp.sum(ahead.astype(_F32), axis=0, keepdims=True)
            radj.append(jnp.where(rank < MOBA_TOPK, 0.0, NEG) - slope * float(blk * (i - n)))
        m = None
        for n in range(i + 1):
            t = lax.dot_general(kaug_sc[h, n * blk:(n + 1) * blk, :], qm, _NT, preferred_element_type=_F32)
            if n == i:
                t = t + causal_sc[...]
            s_sc[slot, n + dyn0] = t
            bm = jnp.max(t, axis=0, keepdims=True)
            if n < i:
                bm = bm + radj[n]
            m = bm if m is None else jnp.maximum(m, bm)
        return [m - radj[n] if n < i else m for n in range(i + 1)]

    def weighted_values(lp, i, e, slot, shifts):
        acc = None
        for n in range(i + 1):
            p = jnp.exp2(s_sc[slot, n + dyn0] - shifts[n]).astype(_BF16)
            pv = jnp.dot(vaug_sc[2 * lp + e, n], p, preferred_element_type=_F32)
            acc = pv if acc is None else acc + pv
        return acc[:HEAD_DIM] / acc[HEAD_DIM:HEAD_DIM + 1]

    outs = {}

    def item(lp, i, e):
        def finish(slot, shifts):
            outs[e] = weighted_values(lp, i, e, slot, shifts)
            if e == 1:
                o = jnp.concatenate([outs.pop(0), outs.pop(1)], axis=0)
                o_ref[i * blk:(i + 1) * blk, lp * pair:(lp + 1) * pair] = o.astype(o_ref.dtype).T

        return (lambda slot: scores(lp, i, e, slot)), finish

    return [item(lp, i, e) for lp in range(npair) for i in reversed(range(nb)) for e in range(2)]


def _swa_items(slopes_ref, sinks_ref, q_ref, k_ref, vt_ref, o_ref, bias_sc, vaug_sc, s_sc, dyn0):
    w = SWA_WINDOW
    sq = w // 2
    span = w + sq
    ncol = B_GROUP * sq
    ntile = q_ref.shape[0] // sq
    nblk = q_ref.shape[0] // w

    def per_head(hk, col, ref):
        out = ref[hk * B_GROUP]
        for h in range(1, B_GROUP):
            out = jnp.where(col >= h * sq, ref[hk * B_GROUP + h], out)
        return out

    kp = lax.broadcasted_iota(jnp.int32, (span, ncol), 0)
    col = lax.broadcasted_iota(jnp.int32, (span, ncol), 1)
    dist = (col & (sq - 1)) + w - kp
    lane = lax.broadcasted_iota(jnp.int32, (sq, 2 * HEAD_DIM), 1)
    first = lane < HEAD_DIM
    ones = jnp.ones((vaug_sc.shape[2] - HEAD_DIM, 2 * w), _BF16)

    def prepare(hk):
        slope = per_head(hk, col, slopes_ref)
        bias_sc[hk] = jnp.where((dist >= 0) & (dist < w), -slope * dist.astype(_F32), NEG)
        v_rows = slice(hk * HEAD_DIM, (hk + 1) * HEAD_DIM)
        for j in range(nblk):
            prev = vt_ref[j - 1, v_rows, :] if j > 0 else jnp.zeros((HEAD_DIM, w), _BF16)
            vaug_sc[hk, j, :HEAD_DIM, :] = jnp.concatenate([prev, vt_ref[j, v_rows, :]], axis=1)
            vaug_sc[hk, j, HEAD_DIM:, :] = ones
        return per_head(hk, lax.broadcasted_iota(jnp.int32, (1, ncol), 1), sinks_ref)

    sinks = {}

    def key_range(t):
        k0 = max(t - 2, 0) * sq
        return k0, (t + 1) * sq - k0

    def scores(hk, t, slot):
        if hk not in sinks:
            sinks[hk] = prepare(hk)
        q_t = q_ref[t * sq:(t + 1) * sq, hk * ncol:(hk + 1) * ncol]
        zero = jnp.zeros((sq, 2 * HEAD_DIM), q_t.dtype)
        stacked = []
        for pr in range(B_GROUP // 2):
            pair = q_t[:, pr * 2 * HEAD_DIM:(pr + 1) * 2 * HEAD_DIM]
            stacked += [jnp.where(first, pair, zero), jnp.where(first, zero, pair)]
        qs = jnp.concatenate(stacked, axis=0)
        k0, nk = key_range(t)
        keys = k_ref[k0:k0 + nk, hk * 2 * HEAD_DIM:(hk + 1) * 2 * HEAD_DIM]
        s = lax.dot_general(keys, qs, _NT, preferred_element_type=_F32)
        s = s + bias_sc[hk, span - nk:, :]
        s_sc[slot + dyn0, :nk, :] = s
        return jnp.maximum(jnp.max(s, axis=0, keepdims=True), sinks[hk])

    def weighted_values(hk, t, slot, m):
        k0, nk = key_range(t)
        pb = jnp.exp2(s_sc[slot + dyn0, :nk, :] - m).astype(_BF16)
        before = k0 - (t // 2 - 1) * w
        after = 2 * w - before - nk
        pad = lambda rows: [jnp.zeros((rows, ncol), _BF16)] if rows else []
        p_full = jnp.concatenate(pad(before) + [pb] + pad(after), axis=0)
        ot = jnp.dot(vaug_sc[hk, t // 2], p_full, preferred_element_type=_F32)
        den = ot[HEAD_DIM:HEAD_DIM + 1] + jnp.exp2(sinks[hk] - m)
        return ot[:HEAD_DIM] / den

    def store(hk, j, even, odd):
        even, odd = even.astype(o_ref.dtype), odd.astype(o_ref.dtype)
        for pr in range(B_GROUP // 2):
            lanes = slice(pr * 2 * HEAD_DIM, (pr + 1) * 2 * HEAD_DIM)
            e, o = even[:, lanes], odd[:, lanes]
            xa = jnp.where(first, e, pltpu.roll(o, sq, 1))
            xb = jnp.where(first, pltpu.roll(e, sq, 1), o)
            x = jnp.concatenate([xa, xb], axis=0)
            lane0 = hk * ncol + pr * 2 * HEAD_DIM
            o_ref[j * w:(j + 1) * w, lane0:lane0 + 2 * HEAD_DIM] = x.T

    pending = {}

    def item(hk, t):
        def finish(slot, m):
            out = weighted_values(hk, t, slot, m)
            if t % 2 == 0:
                pending[hk] = out
            else:
                store(hk, t // 2, pending.pop(hk), out)

        return (lambda slot: scores(hk, t, slot)), finish

    return [item(hk, t) for hk in range(q_ref.shape[1] // ncol) for t in range(ntile)]


def _attention_kernel(slopes_a_ref, slopes_b_ref, sinks_ref, qa_ref, ka_ref, vat_ref, km_ref, qb_ref, kb_ref,
                      vbt_ref, oa_ref, ob_ref, kaug_sc, vaug_a_sc, causal_sc, sa_sc, bias_sc, vaug_b_sc, sb_sc):
    dyn0 = _dynamic_zero()
    moba = _moba_items(slopes_a_ref, qa_ref, ka_ref, vat_ref, km_ref, oa_ref, kaug_sc, vaug_a_sc, causal_sc,
                       sa_sc, dyn0)
    swa = _swa_items(slopes_b_ref, sinks_ref, qb_ref, kb_ref, vbt_ref, ob_ref, bias_sc, vaug_b_sc, sb_sc, dyn0)
    items = []
    for kind, (group, nslot) in enumerate(((moba, sa_sc.shape[0]), (swa, sb_sc.shape[0]))):
        items += [((k + 0.5) / len(group), kind, k % nslot, fns) for k, fns in enumerate(group)]
    items.sort(key=lambda it: it[:2])
    state = {}
    for u in range(len(items) + ATTN_LOOKAHEAD):
        if u < len(items):
            _, _, slot, (scores, _) = items[u]
            state[u] = scores(slot)
        done = u - ATTN_LOOKAHEAD
        if done >= 0:
            _, _, slot, (_, finish) = items[done]
            finish(slot, state.pop(done))


def _attention(slopes_a, slopes_b, sinks, qa, ka, vat, km, qb, kb, vbt, batch, seq):
    nb = seq // MOBA_BLOCK
    nblk = seq // SWA_WINDOW
    grp = B_GROUP * HEAD_DIM
    span = SWA_WINDOW + SWA_WINDOW // 2
    smem = pl.BlockSpec(memory_space=pltpu.SMEM)
    rows = lambda width: pl.BlockSpec((seq, width), lambda b: (b, 0))
    slots = ATTN_LOOKAHEAD // 2 + 2
    return pl.pallas_call(
        _attention_kernel, grid=(batch,),
        in_specs=[smem, smem, smem, rows(W_A), rows(W_A),
                  pl.BlockSpec((nb, W_A, MOBA_BLOCK), lambda b: (b, 0, 0)),
                  pl.BlockSpec((1, nb, W_A), lambda b: (b, 0, 0)),
                  rows(W_QB), rows(W_KB_DUP),
                  pl.BlockSpec((nblk, W_KB, SWA_WINDOW), lambda b: (b, 0, 0))],
        out_specs=(rows(W_A), rows(W_QB)),
        out_shape=(jax.ShapeDtypeStruct(qa.shape, _BF16), jax.ShapeDtypeStruct(qb.shape, _BF16)),
        scratch_shapes=[pltpu.VMEM((A_HEADS, seq, 2 * HEAD_DIM), _BF16),
                        pltpu.VMEM((A_HEADS, nb, HEAD_DIM + ONES_ROWS, MOBA_BLOCK), _BF16),
                        pltpu.VMEM((MOBA_BLOCK, MOBA_BLOCK), _F32),
                        pltpu.VMEM((slots, nb, MOBA_BLOCK, MOBA_BLOCK), _F32),
                        pltpu.VMEM((B_KV_HEADS, span, grp), _F32),
                        pltpu.VMEM((B_KV_HEADS, nblk, HEAD_DIM + ONES_ROWS, 2 * SWA_WINDOW), _BF16),
                        pltpu.VMEM((slots, span, grp), _F32)],
        compiler_params=pltpu.CompilerParams(dimension_semantics=("parallel",),
                                             vmem_limit_bytes=ATTN_VMEM_LIMIT),
        name="attention",
    )(slopes_a, slopes_b, sinks, qa, ka, vat, km, qb, kb, vbt)


def _merge_mlp_kernel(x_ref, oa_ref, ob_ref, g_ref, wa_ref, wb_ref, wo_ref, gm_ref, wup_ref, wdn_ref, o_ref):
    a = jnp.dot(oa_ref[...], wa_ref[...], preferred_element_type=_F32)
    b = jnp.dot(ob_ref[...], wb_ref[...], preferred_element_type=_F32)
    ga = g_ref[:, :D_MODEL].astype(_F32)
    gb = g_ref[:, D_MODEL:].astype(_F32)
    mixed = jax.nn.sigmoid(ga) * a + jax.nn.sigmoid(gb) * b
    x1 = x_ref[...] + jnp.dot(mixed.astype(_BF16), wo_ref[...], preferred_element_type=_F32)
    ms = jnp.mean(x1 * x1, axis=-1, keepdims=True)
    h2 = ((x1 * lax.rsqrt(ms + EPS)) * gm_ref[...]).astype(_BF16)
    acc = x1
    for c in range(D_FF // FF_CHUNK):
        u = jnp.dot(h2, wup_ref[:, c * FF_CHUNK:(c + 1) * FF_CHUNK], preferred_element_type=_F32)
        u = jnp.square(jnp.maximum(u, 0.0)).astype(_BF16)
        acc = acc + jnp.dot(u, wdn_ref[c * FF_CHUNK:(c + 1) * FF_CHUNK, :], preferred_element_type=_F32)
    o_ref[...] = acc


def _merge_mlp(x2, oa, ob, g, wa, wb, wo, gm, wup, wdn):
    n = x2.shape[0]
    tm = MERGE_TILE
    row = lambda w: pl.BlockSpec((tm, w), lambda i: (i, 0))
    return pl.pallas_call(
        _merge_mlp_kernel, grid=(n // tm,),
        in_specs=[row(D_MODEL), row(W_A), row(W_QB), row(W_GATES), _const_spec(wa.shape), _const_spec(wb.shape),
                  _const_spec(wo.shape), _const_spec(gm.shape), _const_spec(wup.shape), _const_spec(wdn.shape)],
        out_specs=row(D_MODEL),
        out_shape=jax.ShapeDtypeStruct(x2.shape, x2.dtype),
        compiler_params=pltpu.CompilerParams(dimension_semantics=("parallel",),
                                             vmem_limit_bytes=VMEM_LIMIT),
        name="merge_mlp",
    )(x2, oa, ob, g, wa, wb, wo, gm, wup, wdn)


def _alibi_slopes(n):
    return jnp.exp2(-(8.0 / n) * jnp.arange(1, n + 1, dtype=_F32))


def kernel(x, norm_attn, w_in, q_norm_a, k_norm_a, q_norm_b, k_norm_b, sinks_b, w_branch_a, w_branch_b, w_out,
           norm_mlp, w_up, w_down):
    batch, seq, d = x.shape
    assert d == D_MODEL and (batch * seq) % INPROJ_TILE == 0 and (batch * seq) % MERGE_TILE == 0
    assert seq % MOBA_BLOCK == 0 and INPROJ_TILE % MOBA_BLOCK == 0
    slopes = _alibi_slopes(N_ATTN_HEADS)
    slopes_b, slopes_a = slopes[:B_HEADS], slopes[B_HEADS:]
    x2 = x.reshape(batch * seq, d)
    for l in range(norm_attn.shape[0]):
        tile_gain = lambda g, reps: jnp.tile(g, reps)[None, :]
        (qa, ka, qb, kb, g, vat, vbt, km), (wa, wb, wo, wup, wdn) = _inproj(
            x2, norm_attn[l][None, :], w_in[l].astype(_BF16),
            tile_gain(q_norm_a[l], A_HEADS), tile_gain(k_norm_a[l], A_HEADS),
            tile_gain(q_norm_b[l], B_HEADS), tile_gain(k_norm_b[l], B_KV_HEADS),
            (w_branch_a[l], w_branch_b[l], w_out[l], w_up[l], w_down[l]))
        km = km.reshape(batch, seq // MOBA_BLOCK, W_A)
        oa, ob = _attention(slopes_a * LOG2E, slopes_b * LOG2E, sinks_b[l] * LOG2E, qa, ka, vat, km, qb, kb, vbt,
                            batch, seq)
        x2 = _merge_mlp(x2, oa, ob, g, wa, wb, wo, norm_mlp[l][None, :], wup, wdn)
    return x2.reshape(batch, seq, d)
```

```python
import jax
import jax.numpy as jnp
from jax import lax
from jax.experimental import pallas as pl
from jax.experimental.pallas import tpu as pltpu

D_MODEL = 1024
HEAD_DIM = 64
A_HEADS = 8
B_HEADS = 8
B_KV_HEADS = 2
B_GROUP = B_HEADS // B_KV_HEADS
N_ATTN_HEADS = A_HEADS + B_HEADS
MOBA_BLOCK = 256
MOBA_TOPK = 3
SWA_WINDOW = 128
D_FF = 4 * D_MODEL
EPS = 1e-6
NEG = -1e30
SCALE = HEAD_DIM ** -0.5
LOG2E = 1.4426950408889634

W_A = A_HEADS * HEAD_DIM
W_QB = B_HEADS * HEAD_DIM
W_KB = B_KV_HEADS * HEAD_DIM
W_KB_DUP = 2 * W_KB
W_GATES = 2 * D_MODEL

C_QA = 0
C_KA = C_QA + W_A
C_VA = C_KA + W_A
C_QB = C_VA + W_A
C_KB = C_QB + W_QB
C_VB = C_KB + W_KB
C_G = C_VB + W_KB
C_END = C_G + W_GATES

N_INPROJ_OUT = 8
INPROJ_TILE = 1024
MERGE_TILE = 512
FF_CHUNK = 1024
ONES_ROWS = 16
ATTN_LOOKAHEAD = 4
VMEM_LIMIT = 48 * 1024 * 1024
ATTN_VMEM_LIMIT = 56 * 1024 * 1024

_NT = (((1,), (1,)), ((), ()))
_BF16 = jnp.bfloat16
_F32 = jnp.float32


def _const_spec(shape):
    return pl.BlockSpec(shape, lambda *_: (0,) * len(shape), pipeline_mode=pl.Buffered(1))


def _dynamic_zero():
    return jnp.minimum(pl.program_id(0), 0)


def _inproj_kernel(x_ref, gn_ref, w_ref, gqa_ref, gka_ref, gqb_ref, gkb_ref, *refs):
    n_cast = (len(refs) - N_INPROJ_OUT) // 2
    qa_ref, ka_ref, qb_ref, kb_ref, g_ref, vat_ref, vbt_ref, km_ref = refs[n_cast:n_cast + N_INPROJ_OUT]
    for src, dst in zip(refs[:n_cast], refs[n_cast + N_INPROJ_OUT:]):
        dst[...] = src[...].astype(_BF16)

    sub = MOBA_BLOCK
    n_sub = x_ref.shape[0] // sub
    lane = lax.broadcasted_iota(jnp.int32, (sub, 2 * HEAD_DIM), 1)
    first = lane < HEAD_DIM

    def normed(r):
        x = x_ref[r * sub:(r + 1) * sub, :]
        ms = jnp.mean(x * x, axis=-1, keepdims=True)
        return ((x * lax.rsqrt(ms + EPS)) * gn_ref[...]).astype(_BF16)

    def head_norm(y, gain_ref):
        parts = []
        for c in range(0, y.shape[-1], 2 * HEAD_DIM):
            yc = y[:, c:c + 2 * HEAD_DIM]
            sq = yc * yc
            s0 = jnp.sum(jnp.where(first, sq, 0.0), axis=-1, keepdims=True)
            s1 = jnp.sum(jnp.where(first, 0.0, sq), axis=-1, keepdims=True)
            msq = jnp.where(first, s0, s1) * (1.0 / HEAD_DIM)
            parts.append(yc * lax.rsqrt(msq + EPS))
        return jnp.concatenate(parts, axis=1) * gain_ref[...]

    def project(r, h):
        rows = slice(r * sub, (r + 1) * sub)

        def proj(lo, hi):
            return jnp.dot(h, w_ref[:, lo:hi], preferred_element_type=_F32)

        qa_ref[rows, :] = (head_norm(proj(C_QA, C_KA), gqa_ref) * (SCALE * LOG2E)).astype(_BF16)
        kn = head_norm(proj(C_KA, C_VA), gka_ref)
        ka_ref[rows, :] = kn.astype(_BF16)
        km_ref[0, r:r + 1, :] = jnp.sum(kn, axis=0, keepdims=True) * (1.0 / MOBA_BLOCK)
        qb_ref[rows, :] = (head_norm(proj(C_QB, C_KB), gqb_ref) * (SCALE * LOG2E)).astype(_BF16)
        g_ref[rows, :] = proj(C_G, C_END).astype(_BF16)

        kv = proj(C_KB, C_G)
        kb = head_norm(kv[:, :W_KB], gkb_ref)
        swapped = pltpu.roll(kb, HEAD_DIM, 1)
        kb_ref[rows, :] = jnp.concatenate([jnp.where(first, kb, swapped), jnp.where(first, swapped, kb)],
                                          axis=1).astype(_BF16)

        vat_ref[r] = proj(C_VA, C_QB).T.astype(_BF16)
        vbt = kv[:, W_KB:].T
        per = sub // SWA_WINDOW
        for c in range(per):
            vbt_ref[r * per + c] = vbt[:, c * SWA_WINDOW:(c + 1) * SWA_WINDOW].astype(_BF16)

    hs = [normed(r) for r in range(n_sub)]
    for r in range(n_sub):
        project(r, hs[r])


def _inproj(x2, gn, w_in, gqa, gka, gqb, gkb, later_weights):
    n = x2.shape[0]
    tm = INPROJ_TILE
    steps = n // tm
    row = lambda w: pl.BlockSpec((tm, w), lambda i: (i, 0))
    slabs = [w.reshape(steps, w.shape[0] // steps, w.shape[1]) for w in later_weights]
    slab_specs = [pl.BlockSpec((1,) + s.shape[1:], lambda i: (i, 0, 0)) for s in slabs]
    out_shape = (
        jax.ShapeDtypeStruct((n, W_A), _BF16),
        jax.ShapeDtypeStruct((n, W_A), _BF16),
        jax.ShapeDtypeStruct((n, W_QB), _BF16),
        jax.ShapeDtypeStruct((n, W_KB_DUP), _BF16),
        jax.ShapeDtypeStruct((n, W_GATES), _BF16),
        jax.ShapeDtypeStruct((n // MOBA_BLOCK, W_A, MOBA_BLOCK), _BF16),
        jax.ShapeDtypeStruct((n // SWA_WINDOW, W_KB, SWA_WINDOW), _BF16),
        jax.ShapeDtypeStruct((n // tm, tm // MOBA_BLOCK, W_A), _F32),
    )
    out_specs = (
        row(W_A), row(W_A), row(W_QB), row(W_KB_DUP), row(W_GATES),
        pl.BlockSpec((tm // MOBA_BLOCK, W_A, MOBA_BLOCK), lambda i: (i, 0, 0)),
        pl.BlockSpec((tm // SWA_WINDOW, W_KB, SWA_WINDOW), lambda i: (i, 0, 0)),
        pl.BlockSpec((1, tm // MOBA_BLOCK, W_A), lambda i: (i, 0, 0)),
    )
    in_specs = [row(D_MODEL), _const_spec(gn.shape), _const_spec(w_in.shape),
                _const_spec(gqa.shape), _const_spec(gka.shape),
                _const_spec(gqb.shape), _const_spec(gkb.shape)]
    outs = pl.pallas_call(
        _inproj_kernel, grid=(steps,), in_specs=in_specs + slab_specs,
        out_specs=out_specs + tuple(slab_specs),
        out_shape=out_shape + tuple(jax.ShapeDtypeStruct(s.shape, _BF16) for s in slabs),
        compiler_params=pltpu.CompilerParams(dimension_semantics=("parallel",),
                                             vmem_limit_bytes=VMEM_LIMIT),
        name="inproj",
    )(x2, gn, w_in, gqa, gka, gqb, gkb, *slabs)
    assert len(out_shape) == N_INPROJ_OUT
    return outs[:N_INPROJ_OUT], [o.reshape(w.shape) for o, w in zip(outs[N_INPROJ_OUT:], later_weights)]


def _moba_items(slopes_ref, q_ref, k_ref, vt_ref, km_ref, o_ref, kaug_sc, vaug_sc, causal_sc, s_sc, dyn0):
    blk = MOBA_BLOCK
    nb = q_ref.shape[0] // blk
    pair = 2 * HEAD_DIM
    npair = q_ref.shape[1] // pair
    kp = lax.broadcasted_iota(jnp.int32, (blk, blk), 0)
    qp = lax.broadcasted_iota(jnp.int32, (blk, blk), 1)
    causal_sc[...] = jnp.where(kp <= qp, 0.0, NEG)
    lane = lax.broadcasted_iota(jnp.int32, (blk, pair), 1)
    prow = lax.broadcasted_iota(jnp.int32, (blk, pair), 0).astype(_F32)
    ridx = lax.broadcasted_iota(jnp.int32, (nb, blk), 0)
    km_lane = lax.broadcasted_iota(jnp.int32, (nb, pair), 1)
    in_head = [(lane >= e * HEAD_DIM) & (lane < (e + 1) * HEAD_DIM) for e in range(2)]

    def prepare(lp):
        lanes = slice(lp * pair, (lp + 1) * pair)
        heads = []
        for e in range(2):
            h = 2 * lp + e
            slope = slopes_ref[h]
            a = (1 - e) * HEAD_DIM
            sv = jnp.full((blk, pair), slope, _F32)
            hi = sv.astype(_BF16).astype(_F32)
            mid = (sv - hi).astype(_BF16).astype(_F32)
            lo = sv - hi - mid
            pieces = jnp.where(lane == a, hi, jnp.where(lane == a + 1, mid, jnp.where(lane == a + 2, lo, 0.0)))
            k_aug = jnp.where((lane >= a) & (lane < a + 3), prow, 0.0).astype(_BF16)
            for n in range(nb):
                rows = slice(n * blk, (n + 1) * blk)
                kaug_sc[h, rows, :] = jnp.where(in_head[e], k_ref[rows, lanes], k_aug)
                vaug_sc[h, n, :HEAD_DIM, :] = vt_ref[n, h * HEAD_DIM:(h + 1) * HEAD_DIM, :]
                vaug_sc[h, n, HEAD_DIM:, :] = jnp.ones((vaug_sc.shape[2] - HEAD_DIM, blk), _BF16)
            km_e = (km_lane >= e * HEAD_DIM) & (km_lane < (e + 1) * HEAD_DIM)
            km_head = jnp.where(km_e, km_ref[0, :, lanes], 0.0).astype(_BF16)
            heads.append((slope, pieces.astype(_BF16), km_head))
        return heads

    prepared = {}

    def scores(lp, i, e, slot):
        if lp not in prepared:
            prepared[lp] = prepare(lp)
        slope, q_aug, km_head = prepared[lp][e]
        h = 2 * lp + e
        qm = jnp.where(in_head[e], q_ref[i * blk:(i + 1) * blk, lp * pair:(lp + 1) * pair], q_aug)
        gs = lax.dot_general(km_head, qm, _NT, preferred_element_type=_F32)
        radj = []
        for n in range(i):
            row = gs[n:n + 1, :]
            ahead = ((gs > row) | ((gs == row) & (ridx < n))) & (ridx < i)
            rank = jnp.sum(ahead.astype(_F32), axis=0, keepdims=True)
            radj.append(jnp.where(rank < MOBA_TOPK, 0.0, NEG) - slope * float(blk * (i - n)))
        m = None
        for n in range(i + 1):
            t = lax.dot_general(kaug_sc[h, n * blk:(n + 1) * blk, :], qm, _NT, preferred_element_type=_F32)
            if n == i:
                t = t + causal_sc[...]
            s_sc[slot, n + dyn0] = t
            bm = jnp.max(t, axis=0, keepdims=True)
            if n < i:
                bm = bm + radj[n]
            m = bm if m is None else jnp.maximum(m, bm)
        return [m - radj[n] if n < i else m for n in range(i + 1)]

    def weighted_values(lp, i, e, slot, shifts):
        acc = None
        for n in range(i + 1):
            p = jnp.exp2(s_sc[slot, n + dyn0] - shifts[n]).astype(_BF16)
            pv = jnp.dot(vaug_sc[2 * lp + e, n], p, preferred_element_type=_F32)
            acc = pv if acc is None else acc + pv
        return acc[:HEAD_DIM] / acc[HEAD_DIM:HEAD_DIM + 1]

    outs = {}

    def item(lp, i, e):
        def finish(slot, shifts):
            outs[e] = weighted_values(lp, i, e, slot, shifts)
            if e == 1:
                o = jnp.concatenate([outs.pop(0), outs.pop(1)], axis=0)
                o_ref[i * blk:(i + 1) * blk, lp * pair:(lp + 1) * pair] = o.astype(o_ref.dtype).T

        return (lambda slot: scores(lp, i, e, slot)), finish

    return [item(lp, i, e) for lp in range(npair) for i in reversed(range(nb)) for e in range(2)]


def _swa_items(slopes_ref, sinks_ref, q_ref, k_ref, vt_ref, o_ref, bias_sc, vaug_sc, s_sc, dyn0):
    w = SWA_WINDOW
    sq = w // 2
    span = w + sq
    ncol = B_GROUP * sq
    ntile = q_ref.shape[0] // sq
    nblk = q_ref.shape[0] // w

    def per_head(hk, col, ref):
        out = ref[hk * B_GROUP]
        for h in range(1, B_GROUP):
            out = jnp.where(col >= h * sq, ref[hk * B_GROUP + h], out)
        return out

    kp = lax.broadcasted_iota(jnp.int32, (span, ncol), 0)
    col = lax.broadcasted_iota(jnp.int32, (span, ncol), 1)
    dist = (col & (sq - 1)) + w - kp
    lane = lax.broadcasted_iota(jnp.int32, (sq, 2 * HEAD_DIM), 1)
    first = lane < HEAD_DIM
    ones = jnp.ones((vaug_sc.shape[2] - HEAD_DIM, 2 * w), _BF16)

    def prepare(hk):
        slope = per_head(hk, col, slopes_ref)
        bias_sc[hk] = jnp.where((dist >= 0) & (dist < w), -slope * dist.astype(_F32), NEG)
        v_rows = slice(hk * HEAD_DIM, (hk + 1) * HEAD_DIM)
        for j in range(nblk):
            prev = vt_ref[j - 1, v_rows, :] if j > 0 else jnp.zeros((HEAD_DIM, w), _BF16)
            vaug_sc[hk, j, :HEAD_DIM, :] = jnp.concatenate([prev, vt_ref[j, v_rows, :]], axis=1)
            vaug_sc[hk, j, HEAD_DIM:, :] = ones
        return per_head(hk, lax.broadcasted_iota(jnp.int32, (1, ncol), 1), sinks_ref)

    sinks = {}

    def key_range(t):
        k0 = max(t - 2, 0) * sq
        return k0, (t + 1) * sq - k0

    def scores(hk, t, slot):
        if hk not in sinks:
            sinks[hk] = prepare(hk)
        q_t = q_ref[t * sq:(t + 1) * sq, hk * ncol:(hk + 1) * ncol]
        zero = jnp.zeros((sq, 2 * HEAD_DIM), q_t.dtype)
        stacked = []
        for pr in range(B_GROUP // 2):
            pair = q_t[:, pr * 2 * HEAD_DIM:(pr + 1) * 2 * HEAD_DIM]
            stacked += [jnp.where(first, pair, zero), jnp.where(first, zero, pair)]
        qs = jnp.concatenate(stacked, axis=0)
        k0, nk = key_range(t)
        keys = k_ref[k0:k0 + nk, hk * 2 * HEAD_DIM:(hk + 1) * 2 * HEAD_DIM]
        s = lax.dot_general(keys, qs, _NT, preferred_element_type=_F32)
        s = s + bias_sc[hk, span - nk:, :]
        s_sc[slot + dyn0, :nk, :] = s
        return jnp.maximum(jnp.max(s, axis=0, keepdims=True), sinks[hk])

    def weighted_values(hk, t, slot, m):
        k0, nk = key_range(t)
        pb = jnp.exp2(s_sc[slot + dyn0, :nk, :] - m).astype(_BF16)
        before = k0 - (t // 2 - 1) * w
        after = 2 * w - before - nk
        pad = lambda rows: [jnp.zeros((rows, ncol), _BF16)] if rows else []
        p_full = jnp.concatenate(pad(before) + [pb] + pad(after), axis=0)
        ot = jnp.dot(vaug_sc[hk, t // 2], p_full, preferred_element_type=_F32)
        den = ot[HEAD_DIM:HEAD_DIM + 1] + jnp.exp2(sinks[hk] - m)
        return ot[:HEAD_DIM] / den

    def store(hk, j, even, odd):
        even, odd = even.astype(o_ref.dtype), odd.astype(o_ref.dtype)
        for pr in range(B_GROUP // 2):
            lanes = slice(pr * 2 * HEAD_DIM, (pr + 1) * 2 * HEAD_DIM)
            e, o = even[:, lanes], odd[:, lanes]
            xa = jnp.where(first, e, pltpu.roll(o, sq, 1))
            xb = jnp.where(first, pltpu.roll(e, sq, 1), o)
            x = jnp.concatenate([xa, xb], axis=0)
            lane0 = hk * ncol + pr * 2 * HEAD_DIM
            o_ref[j * w:(j + 1) * w, lane0:lane0 + 2 * HEAD_DIM] = x.T

    pending = {}

    def item(hk, t):
        def finish(slot, m):
            out = weighted_values(hk, t, slot, m)
            if t % 2 == 0:
                pending[hk] = out
            else:
                store(hk, t // 2, pending.pop(hk), out)

        return (lambda slot: scores(hk, t, slot)), finish

    return [item(hk, t) for hk in range(q_ref.shape[1] // ncol) for t in range(ntile)]


def _attention_kernel(slopes_a_ref, slopes_b_ref, sinks_ref, qa_ref, ka_ref, vat_ref, km_ref, qb_ref, kb_ref,
                      vbt_ref, oa_ref, ob_ref, kaug_sc, vaug_a_sc, causal_sc, sa_sc, bias_sc, vaug_b_sc, sb_sc):
    dyn0 = _dynamic_zero()
    moba = _moba_items(slopes_a_ref, qa_ref, ka_ref, vat_ref, km_ref, oa_ref, kaug_sc, vaug_a_sc, causal_sc,
                       sa_sc, dyn0)
    swa = _swa_items(slopes_b_ref, sinks_ref, qb_ref, kb_ref, vbt_ref, ob_ref, bias_sc, vaug_b_sc, sb_sc, dyn0)
    items = []
    for kind, (group, nslot) in enumerate(((moba, sa_sc.shape[0]), (swa, sb_sc.shape[0]))):
        items += [((k + 0.5) / len(group), kind, k % nslot, fns) for k, fns in enumerate(group)]
    items.sort(key=lambda it: it[:2])
    state = {}
    for u in range(len(items) + ATTN_LOOKAHEAD):
        if u < len(items):
            _, _, slot, (scores, _) = items[u]
            state[u] = scores(slot)
        done = u - ATTN_LOOKAHEAD
        if done >= 0:
            _, _, slot, (_, finish) = items[done]
            finish(slot, state.pop(done))


def _attention(slopes_a, slopes_b, sinks, qa, ka, vat, km, qb, kb, vbt, batch, seq):
    nb = seq // MOBA_BLOCK
    nblk = seq // SWA_WINDOW
    grp = B_GROUP * HEAD_DIM
    span = SWA_WINDOW + SWA_WINDOW // 2
    smem = pl.BlockSpec(memory_space=pltpu.SMEM)
    rows = lambda width: pl.BlockSpec((seq, width), lambda b: (b, 0))
    slots = ATTN_LOOKAHEAD // 2 + 2
    return pl.pallas_call(
        _attention_kernel, grid=(batch,),
        in_specs=[smem, smem, smem, rows(W_A), rows(W_A),
                  pl.BlockSpec((nb, W_A, MOBA_BLOCK), lambda b: (b, 0, 0)),
                  pl.BlockSpec((1, nb, W_A), lambda b: (b, 0, 0)),
                  rows(W_QB), rows(W_KB_DUP),
                  pl.BlockSpec((nblk, W_KB, SWA_WINDOW), lambda b: (b, 0, 0))],
        out_specs=(rows(W_A), rows(W_QB)),
        out_shape=(jax.ShapeDtypeStruct(qa.shape, _BF16), jax.ShapeDtypeStruct(qb.shape, _BF16)),
        scratch_shapes=[pltpu.VMEM((A_HEADS, seq, 2 * HEAD_DIM), _BF16),
                        pltpu.VMEM((A_HEADS, nb, HEAD_DIM + ONES_ROWS, MOBA_BLOCK), _BF16),
                        pltpu.VMEM((MOBA_BLOCK, MOBA_BLOCK), _F32),
                        pltpu.VMEM((slots, nb, MOBA_BLOCK, MOBA_BLOCK), _F32),
                        pltpu.VMEM((B_KV_HEADS, span, grp), _F32),
                        pltpu.VMEM((B_KV_HEADS, nblk, HEAD_DIM + ONES_ROWS, 2 * SWA_WINDOW), _BF16),
                        pltpu.VMEM((slots, span, grp), _F32)],
        compiler_params=pltpu.CompilerParams(dimension_semantics=("parallel",),
                                             vmem_limit_bytes=ATTN_VMEM_LIMIT),
        name="attention",
    )(slopes_a, slopes_b, sinks, qa, ka, vat, km, qb, kb, vbt)


def _merge_mlp_kernel(x_ref, oa_ref, ob_ref, g_ref, wa_ref, wb_ref, wo_ref, gm_ref, wup_ref, wdn_ref, o_ref):
    a = jnp.dot(oa_ref[...], wa_ref[...], preferred_element_type=_F32)
    b = jnp.dot(ob_ref[...], wb_ref[...], preferred_element_type=_F32)
    ga = g_ref[:, :D_MODEL].astype(_F32)
    gb = g_ref[:, D_MODEL:].astype(_F32)
    mixed = jax.nn.sigmoid(ga) * a + jax.nn.sigmoid(gb) * b
    x1 = x_ref[...] + jnp.dot(mixed.astype(_BF16), wo_ref[...], preferred_element_type=_F32)
    ms = jnp.mean(x1 * x1, axis=-1, keepdims=True)
    h2 = ((x1 * lax.rsqrt(ms + EPS)) * gm_ref[...]).astype(_BF16)
    acc = x1
    for c in range(D_FF // FF_CHUNK):
        u = jnp.dot(h2, wup_ref[:, c * FF_CHUNK:(c + 1) * FF_CHUNK], preferred_element_type=_F32)
        u = jnp.square(jnp.maximum(u, 0.0)).astype(_BF16)
        acc = acc + jnp.dot(u, wdn_ref[c * FF_CHUNK:(c + 1) * FF_CHUNK, :], preferred_element_type=_F32)
    o_ref[...] = acc


def _merge_mlp(x2, oa, ob, g, wa, wb, wo, gm, wup, wdn):
    n = x2.shape[0]
    tm = MERGE_TILE
    row = lambda w: pl.BlockSpec((tm, w), lambda i: (i, 0))
    return pl.pallas_call(
        _merge_mlp_kernel, grid=(n // tm,),
        in_specs=[row(D_MODEL), row(W_A), row(W_QB), row(W_GATES), _const_spec(wa.shape), _const_spec(wb.shape),
                  _const_spec(wo.shape), _const_spec(gm.shape), _const_spec(wup.shape), _const_spec(wdn.shape)],
        out_specs=row(D_MODEL),
        out_shape=jax.ShapeDtypeStruct(x2.shape, x2.dtype),
        compiler_params=pltpu.CompilerParams(dimension_semantics=("parallel",),
                                             vmem_limit_bytes=VMEM_LIMIT),
        name="merge_mlp",
    )(x2, oa, ob, g, wa, wb, wo, gm, wup, wdn)


def _alibi_slopes(n):
    return jnp.exp2(-(8.0 / n) * jnp.arange(1, n + 1, dtype=_F32))


def kernel(x, norm_attn, w_in, q_norm_a, k_norm_a, q_norm_b, k_norm_b, sinks_b, w_branch_a, w_branch_b, w_out,
           norm_mlp, w_up, w_down):
    batch, seq, d = x.shape
    assert d == D_MODEL and (batch * seq) % INPROJ_TILE == 0 and (batch * seq) % MERGE_TILE == 0
    assert seq % MOBA_BLOCK == 0 and INPROJ_TILE % MOBA_BLOCK == 0
    slopes = _alibi_slopes(N_ATTN_HEADS)
    slopes_b, slopes_a = slopes[:B_HEADS], slopes[B_HEADS:]
    x2 = x.reshape(batch * seq, d)
    for l in range(norm_attn.shape[0]):
        tile_gain = lambda g, reps: jnp.tile(g, reps)[None, :]
        (qa, ka, qb, kb, g, vat, vbt, km), (wa, wb, wo, wup, wdn) = _inproj(
            x2, norm_attn[l][None, :], w_in[l].astype(_BF16),
            tile_gain(q_norm_a[l], A_HEADS), tile_gain(k_norm_a[l], A_HEADS),
            tile_gain(q_norm_b[l], B_HEADS), tile_gain(k_norm_b[l], B_KV_HEADS),
            (w_branch_a[l], w_branch_b[l], w_out[l], w_up[l], w_down[l]))
        km = km.reshape(batch, seq // MOBA_BLOCK, W_A)
        oa, ob = _attention(slopes_a * LOG2E, slopes_b * LOG2E, sinks_b[l] * LOG2E, qa, ka, vat, km, qb, kb, vbt,
                            batch, seq)
        x2 = _merge_mlp(x2, oa, ob, g, wa, wb, wo, norm_mlp[l][None, :], wup, wdn)
    return x2.reshape(batch, seq, d)
```

```python
import jax
import jax.numpy as jnp
from jax import lax
from jax.experimental import pallas as pl
from jax.experimental.pallas import tpu as pltpu

D_MODEL = 1024
HEAD_DIM = 64
A_HEADS = 8
B_HEADS = 8
B_KV_HEADS = 2
B_GROUP = B_HEADS // B_KV_HEADS
N_ATTN_HEADS = A_HEADS + B_HEADS
MOBA_BLOCK = 256
MOBA_TOPK = 3
SWA_WINDOW = 128
D_FF = 4 * D_MODEL
EPS = 1e-6
NEG = -1e30
SCALE = HEAD_DIM ** -0.5
LOG2E = 1.4426950408889634

W_A = A_HEADS * HEAD_DIM
W_QB = B_HEADS * HEAD_DIM
W_KB = B_KV_HEADS * HEAD_DIM
W_KB_DUP = 2 * W_KB
W_GATES = 2 * D_MODEL

C_QA = 0
C_KA = C_QA + W_A
C_VA = C_KA + W_A
C_QB = C_VA + W_A
C_KB = C_QB + W_QB
C_VB = C_KB + W_KB
C_G = C_VB + W_KB
C_END = C_G + W_GATES

N_INPROJ_OUT = 8
INPROJ_TILE = 1024
MERGE_TILE = 512
FF_CHUNK = 1024
ONES_ROWS = 16
ATTN_LOOKAHEAD = 4
VMEM_LIMIT = 48 * 1024 * 1024
ATTN_VMEM_LIMIT = 56 * 1024 * 1024

_NT = (((1,), (1,)), ((), ()))
_BF16 = jnp.bfloat16
_F32 = jnp.float32


def _const_spec(shape):
    return pl.BlockSpec(shape, lambda *_: (0,) * len(shape), pipeline_mode=pl.Buffered(1))


def _dynamic_zero():
    return jnp.minimum(pl.program_id(0), 0)


def _inproj_kernel(x_ref, gn_ref, w_ref, gqa_ref, gka_ref, gqb_ref, gkb_ref, *refs):
    n_cast = (len(refs) - N_INPROJ_OUT) // 2
    qa_ref, ka_ref, qb_ref, kb_ref, g_ref, vat_ref, vbt_ref, km_ref = refs[n_cast:n_cast + N_INPROJ_OUT]
    for src, dst in zip(refs[:n_cast], refs[n_cast + N_INPROJ_OUT:]):
        dst[...] = src[...].astype(_BF16)

    sub = MOBA_BLOCK
    n_sub = x_ref.shape[0] // sub
    lane = lax.broadcasted_iota(jnp.int32, (sub, 2 * HEAD_DIM), 1)
    first = lane < HEAD_DIM

    def normed(r):
        x = x_ref[r * sub:(r + 1) * sub, :]
        ms = jnp.mean(x * x, axis=-1, keepdims=True)
        return ((x * lax.rsqrt(ms + EPS)) * gn_ref[...]).astype(_BF16)

    def head_norm(y, gain_ref):
        parts = []
        for c in range(0, y.shape[-1], 2 * HEAD_DIM):
            yc = y[:, c:c + 2 * HEAD_DIM]
            sq = yc * yc
            s0 = jnp.sum(jnp.where(first, sq, 0.0), axis=-1, keepdims=True)
            s1 = jnp.sum(jnp.where(first, 0.0, sq), axis=-1, keepdims=True)
            msq = jnp.where(first, s0, s1) * (1.0 / HEAD_DIM)
            parts.append(yc * lax.rsqrt(msq + EPS))
        return jnp.concatenate(parts, axis=1) * gain_ref[...]

    def project(r, h):
        rows = slice(r * sub, (r + 1) * sub)

        def proj(lo, hi):
            return jnp.dot(h, w_ref[:, lo:hi], preferred_element_type=_F32)

        qa_ref[rows, :] = (head_norm(proj(C_QA, C_KA), gqa_ref) * (SCALE * LOG2E)).astype(_BF16)
        kn = head_norm(proj(C_KA, C_VA), gka_ref)
        ka_ref[rows, :] = kn.astype(_BF16)
        km_ref[0, r:r + 1, :] = jnp.sum(kn, axis=0, keepdims=True) * (1.0 / MOBA_BLOCK)
        qb_ref[rows, :] = (head_norm(proj(C_QB, C_KB), gqb_ref) * (SCALE * LOG2E)).astype(_BF16)
        g_ref[rows, :] = proj(C_G, C_END).astype(_BF16)

        kv = proj(C_KB, C_G)
        kb = head_norm(kv[:, :W_KB], gkb_ref)
        swapped = pltpu.roll(kb, HEAD_DIM, 1)
        kb_ref[rows, :] = jnp.concatenate([jnp.where(first, kb, swapped), jnp.where(first, swapped, kb)],
                                          axis=1).astype(_BF16)

        vat_ref[r] = proj(C_VA, C_QB).T.astype(_BF16)
        vbt = kv[:, W_KB:].T
        per = sub // SWA_WINDOW
        for c in range(per):
            vbt_ref[r * per + c] = vbt[:, c * SWA_WINDOW:(c + 1) * SWA_WINDOW].astype(_BF16)

    hs = [normed(r) for r in range(n_sub)]
    for r in range(n_sub):
        project(r, hs[r])


def _inproj(x2, gn, w_in, gqa, gka, gqb, gkb, later_weights):
    n = x2.shape[0]
    tm = INPROJ_TILE
    steps = n // tm
    row = lambda w: pl.BlockSpec((tm, w), lambda i: (i, 0))
    slabs = [w.reshape(steps, w.shape[0] // steps, w.shape[1]) for w in later_weights]
    slab_specs = [pl.BlockSpec((1,) + s.shape[1:], lambda i: (i, 0, 0)) for s in slabs]
    out_shape = (
        jax.ShapeDtypeStruct((n, W_A), _BF16),
        jax.ShapeDtypeStruct((n, W_A), _BF16),
        jax.ShapeDtypeStruct((n, W_QB), _BF16),
        jax.ShapeDtypeStruct((n, W_KB_DUP), _BF16),
        jax.ShapeDtypeStruct((n, W_GATES), _BF16),
        jax.ShapeDtypeStruct((n // MOBA_BLOCK, W_A, MOBA_BLOCK), _BF16),
        jax.ShapeDtypeStruct((n // SWA_WINDOW, W_KB, SWA_WINDOW), _BF16),
        jax.ShapeDtypeStruct((n // tm, tm // MOBA_BLOCK, W_A), _F32),
    )
    out_specs = (
        row(W_A), row(W_A), row(W_QB), row(W_KB_DUP), row(W_GATES),
        pl.BlockSpec((tm // MOBA_BLOCK, W_A, MOBA_BLOCK), lambda i: (i, 0, 0)),
        pl.BlockSpec((tm // SWA_WINDOW, W_KB, SWA_WINDOW), lambda i: (i, 0, 0)),
        pl.BlockSpec((1, tm // MOBA_BLOCK, W_A), lambda i: (i, 0, 0)),
    )
    in_specs = [row(D_MODEL), _const_spec(gn.shape), _const_spec(w_in.shape),
                _const_spec(gqa.shape), _const_spec(gka.shape),
                _const_spec(gqb.shape), _const_spec(gkb.shape)]
    outs = pl.pallas_call(
        _inproj_kernel, grid=(steps,), in_specs=in_specs + slab_specs,
        out_specs=out_specs + tuple(slab_specs),
        out_shape=out_shape + tuple(jax.ShapeDtypeStruct(s.shape, _BF16) for s in slabs),
        compiler_params=pltpu.CompilerParams(dimension_semantics=("parallel",),
                                             vmem_limit_bytes=VMEM_LIMIT),
        name="inproj",
    )(x2, gn, w_in, gqa, gka, gqb, gkb, *slabs)
    assert len(out_shape) == N_INPROJ_OUT
    return outs[:N_INPROJ_OUT], [o.reshape(w.shape) for o, w in zip(outs[N_INPROJ_OUT:], later_weights)]


def _moba_items(slopes_ref, q_ref, k_ref, vt_ref, km_ref, o_ref, kaug_sc, vaug_sc, causal_sc, s_sc, dyn0):
    blk = MOBA_BLOCK
    nb = q_ref.shape[0] // blk
    pair = 2 * HEAD_DIM
    npair = q_ref.shape[1] // pair
    kp = lax.broadcasted_iota(jnp.int32, (blk, blk), 0)
    qp = lax.broadcasted_iota(jnp.int32, (blk, blk), 1)
    causal_sc[...] = jnp.where(kp <= qp, 0.0, NEG)
    lane = lax.broadcasted_iota(jnp.int32, (blk, pair), 1)
    prow = lax.broadcasted_iota(jnp.int32, (blk, pair), 0).astype(_F32)
    ridx = lax.broadcasted_iota(jnp.int32, (nb, blk), 0)
    km_lane = lax.broadcasted_iota(jnp.int32, (nb, pair), 1)
    in_head = [(lane >= e * HEAD_DIM) & (lane < (e + 1) * HEAD_DIM) for e in range(2)]

    def prepare(lp):
        lanes = slice(lp * pair, (lp + 1) * pair)
        heads = []
        for e in range(2):
            h = 2 * lp + e
            slope = slopes_ref[h]
            a = (1 - e) * HEAD_DIM
            sv = jnp.full((blk, pair), slope, _F32)
            hi = sv.astype(_BF16).astype(_F32)
            mid = (sv - hi).astype(_BF16).astype(_F32)
            lo = sv - hi - mid
            pieces = jnp.where(lane == a, hi, jnp.where(lane == a + 1, mid, jnp.where(lane == a + 2, lo, 0.0)))
            k_aug = jnp.where((lane >= a) & (lane < a + 3), prow, 0.0).astype(_BF16)
            for n in range(nb):
                rows = slice(n * blk, (n + 1) * blk)
                kaug_sc[h, rows, :] = jnp.where(in_head[e], k_ref[rows, lanes], k_aug)
                vaug_sc[h, n, :HEAD_DIM, :] = vt_ref[n, h * HEAD_DIM:(h + 1) * HEAD_DIM, :]
                vaug_sc[h, n, HEAD_DIM:, :] = jnp.ones((vaug_sc.shape[2] - HEAD_DIM, blk), _BF16)
            km_e = (km_lane >= e * HEAD_DIM) & (km_lane < (e + 1) * HEAD_DIM)
            km_head = jnp.where(km_e, km_ref[0, :, lanes], 0.0).astype(_BF16)
            heads.append((slope, pieces.astype(_BF16), km_head))
        return heads

    prepared = {}

    def scores(lp, i, e, slot):
        if lp not in prepared:
            prepared[lp] = prepare(lp)
        slope, q_aug, km_head = prepared[lp][e]
        h = 2 * lp + e
        qm = jnp.where(in_head[e], q_ref[i * blk:(i + 1) * blk, lp * pair:(lp + 1) * pair], q_aug)
        gs = lax.dot_general(km_head, qm, _NT, preferred_element_type=_F32)
        radj = []
        for n in range(i):
            row = gs[n:n + 1, :]
            ahead = ((gs > row) | ((gs == row) & (ridx < n))) & (ridx < i)
            rank = jnp.sum(ahead.astype(_F32), axis=0, keepdims=True)
            radj.append(jnp.where(rank < MOBA_TOPK, 0.0, NEG) - slope * float(blk * (i - n)))
        m = None
        for n in range(i + 1):
            t = lax.dot_general(kaug_sc[h, n * blk:(n + 1) * blk, :], qm, _NT, preferred_element_type=_F32)
            if n == i:
                t = t + causal_sc[...]
            s_sc[slot, n + dyn0] = t
            bm = jnp.max(t, axis=0, keepdims=True)
            if n < i:
                bm = bm + radj[n]
            m = bm if m is None else jnp.maximum(m, bm)
        return [m - radj[n] if n < i else m for n in range(i + 1)]

    def weighted_values(lp, i, e, slot, shifts):
        acc = None
        for n in range(i + 1):
            p = jnp.exp2(s_sc[slot, n + dyn0] - shifts[n]).astype(_BF16)
            pv = jnp.dot(vaug_sc[2 * lp + e, n], p, preferred_element_type=_F32)
            acc = pv if acc is None else acc + pv
        return acc[:HEAD_DIM] / acc[HEAD_DIM:HEAD_DIM + 1]

    outs = {}

    def item(lp, i, e):
        def finish(slot, shifts):
            outs[e] = weighted_values(lp, i, e, slot, shifts)
            if e == 1:
                o = jnp.concatenate([outs.pop(0), outs.pop(1)], axis=0)
                o_ref[i * blk:(i + 1) * blk, lp * pair:(lp + 1) * pair] = o.astype(o_ref.dtype).T

        return (lambda slot: scores(lp, i, e, slot)), finish, i + 1

    return [item(lp, i, e) for lp in range(npair) for i in reversed(range(nb)) for e in range(2)]


def _swa_items(slopes_ref, sinks_ref, q_ref, k_ref, vt_ref, o_ref, bias_sc, vaug_sc, s_sc, dyn0):
    w = SWA_WINDOW
    sq = w // 2
    span = w + sq
    ncol = B_GROUP * sq
    ntile = q_ref.shape[0] // sq
    nblk = q_ref.shape[0] // w

    def per_head(hk, col, ref):
        out = ref[hk * B_GROUP]
        for h in range(1, B_GROUP):
            out = jnp.where(col >= h * sq, ref[hk * B_GROUP + h], out)
        return out

    kp = lax.broadcasted_iota(jnp.int32, (span, ncol), 0)
    col = lax.broadcasted_iota(jnp.int32, (span, ncol), 1)
    dist = (col & (sq - 1)) + w - kp
    lane = lax.broadcasted_iota(jnp.int32, (sq, 2 * HEAD_DIM), 1)
    first = lane < HEAD_DIM
    ones = jnp.ones((vaug_sc.shape[2] - HEAD_DIM, 2 * w), _BF16)

    def prepare(hk):
        slope = per_head(hk, col, slopes_ref)
        bias_sc[hk] = jnp.where((dist >= 0) & (dist < w), -slope * dist.astype(_F32), NEG)
        v_rows = slice(hk * HEAD_DIM, (hk + 1) * HEAD_DIM)
        for j in range(nblk):
            prev = vt_ref[j - 1, v_rows, :] if j > 0 else jnp.zeros((HEAD_DIM, w), _BF16)
            vaug_sc[hk, j, :HEAD_DIM, :] = jnp.concatenate([prev, vt_ref[j, v_rows, :]], axis=1)
            vaug_sc[hk, j, HEAD_DIM:, :] = ones
        return per_head(hk, lax.broadcasted_iota(jnp.int32, (1, ncol), 1), sinks_ref)

    sinks = {}

    def key_range(t):
        k0 = max(t - 2, 0) * sq
        return k0, (t + 1) * sq - k0

    def scores(hk, t, slot):
        if hk not in sinks:
            sinks[hk] = prepare(hk)
        q_t = q_ref[t * sq:(t + 1) * sq, hk * ncol:(hk + 1) * ncol]
        zero = jnp.zeros((sq, 2 * HEAD_DIM), q_t.dtype)
        stacked = []
        for pr in range(B_GROUP // 2):
            pair = q_t[:, pr * 2 * HEAD_DIM:(pr + 1) * 2 * HEAD_DIM]
            stacked += [jnp.where(first, pair, zero), jnp.where(first, zero, pair)]
        qs = jnp.concatenate(stacked, axis=0)
        k0, nk = key_range(t)
        keys = k_ref[k0:k0 + nk, hk * 2 * HEAD_DIM:(hk + 1) * 2 * HEAD_DIM]
        s = lax.dot_general(keys, qs, _NT, preferred_element_type=_F32)
        s = s + bias_sc[hk, span - nk:, :]
        s_sc[slot + dyn0, :nk, :] = s
        return jnp.maximum(jnp.max(s, axis=0, keepdims=True), sinks[hk])

    def weighted_values(hk, t, slot, m):
        k0, nk = key_range(t)
        pb = jnp.exp2(s_sc[slot + dyn0, :nk, :] - m).astype(_BF16)
        before = k0 - (t // 2 - 1) * w
        after = 2 * w - before - nk
        pad = lambda rows: [jnp.zeros((rows, ncol), _BF16)] if rows else []
        p_full = jnp.concatenate(pad(before) + [pb] + pad(after), axis=0)
        ot = jnp.dot(vaug_sc[hk, t // 2], p_full, preferred_element_type=_F32)
        den = ot[HEAD_DIM:HEAD_DIM + 1] + jnp.exp2(sinks[hk] - m)
        return ot[:HEAD_DIM] / den

    def store(hk, j, even, odd):
        even, odd = even.astype(o_ref.dtype), odd.astype(o_ref.dtype)
        for pr in range(B_GROUP // 2):
            lanes = slice(pr * 2 * HEAD_DIM, (pr + 1) * 2 * HEAD_DIM)
            e, o = even[:, lanes], odd[:, lanes]
            xa = jnp.where(first, e, pltpu.roll(o, sq, 1))
            xb = jnp.where(first, pltpu.roll(e, sq, 1), o)
            x = jnp.concatenate([xa, xb], axis=0)
            lane0 = hk * ncol + pr * 2 * HEAD_DIM
            o_ref[j * w:(j + 1) * w, lane0:lane0 + 2 * HEAD_DIM] = x.T

    pending = {}

    def item(hk, t):
        def finish(slot, m):
            out = weighted_values(hk, t, slot, m)
            if t % 2 == 0:
                pending[hk] = out
            else:
                store(hk, t // 2, pending.pop(hk), out)

        return (lambda slot: scores(hk, t, slot)), finish, 1

    return [item(hk, t) for hk in range(q_ref.shape[1] // ncol) for t in range(ntile)]


def _attention_kernel(slopes_a_ref, slopes_b_ref, sinks_ref, qa_ref, ka_ref, vat_ref, km_ref, qb_ref, kb_ref,
                      vbt_ref, oa_ref, ob_ref, kaug_sc, vaug_a_sc, causal_sc, sa_sc, bias_sc, vaug_b_sc, sb_sc):
    dyn0 = _dynamic_zero()
    moba = _moba_items(slopes_a_ref, qa_ref, ka_ref, vat_ref, km_ref, oa_ref, kaug_sc, vaug_a_sc, causal_sc,
                       sa_sc, dyn0)
    swa = _swa_items(slopes_b_ref, sinks_ref, qb_ref, kb_ref, vbt_ref, ob_ref, bias_sc, vaug_b_sc, sb_sc, dyn0)
    items = []
    for kind, (group, nslot) in enumerate(((moba, sa_sc.shape[0]), (swa, sb_sc.shape[0]))):
        total = sum(work for _, _, work in group)
        before = 0
        for k, (scores, finish, work) in enumerate(group):
            items.append(((before + work / 2) / total, kind, k % nslot, scores, finish))
            before += work
    items.sort(key=lambda it: it[:2])
    state = {}
    for u in range(len(items) + ATTN_LOOKAHEAD):
        if u < len(items):
            _, _, slot, scores, _ = items[u]
            state[u] = scores(slot)
        done = u - ATTN_LOOKAHEAD
        if done >= 0:
            _, _, slot, _, finish = items[done]
            finish(slot, state.pop(done))


def _attention(slopes_a, slopes_b, sinks, qa, ka, vat, km, qb, kb, vbt, batch, seq):
    nb = seq // MOBA_BLOCK
    nblk = seq // SWA_WINDOW
    grp = B_GROUP * HEAD_DIM
    span = SWA_WINDOW + SWA_WINDOW // 2
    smem = pl.BlockSpec(memory_space=pltpu.SMEM)
    rows = lambda width: pl.BlockSpec((seq, width), lambda b: (b, 0))
    slots = ATTN_LOOKAHEAD + 1
    return pl.pallas_call(
        _attention_kernel, grid=(batch,),
        in_specs=[smem, smem, smem, rows(W_A), rows(W_A),
                  pl.BlockSpec((nb, W_A, MOBA_BLOCK), lambda b: (b, 0, 0)),
                  pl.BlockSpec((1, nb, W_A), lambda b: (b, 0, 0)),
                  rows(W_QB), rows(W_KB_DUP),
                  pl.BlockSpec((nblk, W_KB, SWA_WINDOW), lambda b: (b, 0, 0))],
        out_specs=(rows(W_A), rows(W_QB)),
        out_shape=(jax.ShapeDtypeStruct(qa.shape, _BF16), jax.ShapeDtypeStruct(qb.shape, _BF16)),
        scratch_shapes=[pltpu.VMEM((A_HEADS, seq, 2 * HEAD_DIM), _BF16),
                        pltpu.VMEM((A_HEADS, nb, HEAD_DIM + ONES_ROWS, MOBA_BLOCK), _BF16),
                        pltpu.VMEM((MOBA_BLOCK, MOBA_BLOCK), _F32),
                        pltpu.VMEM((slots, nb, MOBA_BLOCK, MOBA_BLOCK), _F32),
                        pltpu.VMEM((B_KV_HEADS, span, grp), _F32),
                        pltpu.VMEM((B_KV_HEADS, nblk, HEAD_DIM + ONES_ROWS, 2 * SWA_WINDOW), _BF16),
                        pltpu.VMEM((slots, span, grp), _F32)],
        compiler_params=pltpu.CompilerParams(dimension_semantics=("parallel",),
                                             vmem_limit_bytes=ATTN_VMEM_LIMIT),
        name="attention",
    )(slopes_a, slopes_b, sinks, qa, ka, vat, km, qb, kb, vbt)


def _merge_mlp_kernel(x_ref, oa_ref, ob_ref, g_ref, wa_ref, wb_ref, wo_ref, gm_ref, wup_ref, wdn_ref, o_ref):
    a = jnp.dot(oa_ref[...], wa_ref[...], preferred_element_type=_F32)
    b = jnp.dot(ob_ref[...], wb_ref[...], preferred_element_type=_F32)
    ga = g_ref[:, :D_MODEL].astype(_F32)
    gb = g_ref[:, D_MODEL:].astype(_F32)
    mixed = jax.nn.sigmoid(ga) * a + jax.nn.sigmoid(gb) * b
    x1 = x_ref[...] + jnp.dot(mixed.astype(_BF16), wo_ref[...], preferred_element_type=_F32)
    ms = jnp.mean(x1 * x1, axis=-1, keepdims=True)
    h2 = ((x1 * lax.rsqrt(ms + EPS)) * gm_ref[...]).astype(_BF16)
    acc = x1
    for c in range(D_FF // FF_CHUNK):
        u = jnp.dot(h2, wup_ref[:, c * FF_CHUNK:(c + 1) * FF_CHUNK], preferred_element_type=_F32)
        u = jnp.square(jnp.maximum(u, 0.0)).astype(_BF16)
        acc = acc + jnp.dot(u, wdn_ref[c * FF_CHUNK:(c + 1) * FF_CHUNK, :], preferred_element_type=_F32)
    o_ref[...] = acc


def _merge_mlp(x2, oa, ob, g, wa, wb, wo, gm, wup, wdn):
    n = x2.shape[0]
    tm = MERGE_TILE
    row = lambda w: pl.BlockSpec((tm, w), lambda i: (i, 0))
    return pl.pallas_call(
        _merge_mlp_kernel, grid=(n // tm,),
        in_specs=[row(D_MODEL), row(W_A), row(W_QB), row(W_GATES), _const_spec(wa.shape), _const_spec(wb.shape),
                  _const_spec(wo.shape), _const_spec(gm.shape), _const_spec(wup.shape), _const_spec(wdn.shape)],
        out_specs=row(D_MODEL),
        out_shape=jax.ShapeDtypeStruct(x2.shape, x2.dtype),
        compiler_params=pltpu.CompilerParams(dimension_semantics=("parallel",),
                                             vmem_limit_bytes=VMEM_LIMIT),
        name="merge_mlp",
    )(x2, oa, ob, g, wa, wb, wo, gm, wup, wdn)


def _alibi_slopes(n):
    return jnp.exp2(-(8.0 / n) * jnp.arange(1, n + 1, dtype=_F32))


def kernel(x, norm_attn, w_in, q_norm_a, k_norm_a, q_norm_b, k_norm_b, sinks_b, w_branch_a, w_branch_b, w_out,
           norm_mlp, w_up, w_down):
    batch, seq, d = x.shape
    assert d == D_MODEL and (batch * seq) % INPROJ_TILE == 0 and (batch * seq) % MERGE_TILE == 0
    assert seq % MOBA_BLOCK == 0 and INPROJ_TILE % MOBA_BLOCK == 0
    slopes = _alibi_slopes(N_ATTN_HEADS)
    slopes_b, slopes_a = slopes[:B_HEADS], slopes[B_HEADS:]
    x2 = x.reshape(batch * seq, d)
    for l in range(norm_attn.shape[0]):
        tile_gain = lambda g, reps: jnp.tile(g, reps)[None, :]
        (qa, ka, qb, kb, g, vat, vbt, km), (wa, wb, wo, wup, wdn) = _inproj(
            x2, norm_attn[l][None, :], w_in[l].astype(_BF16),
            tile_gain(q_norm_a[l], A_HEADS), tile_gain(k_norm_a[l], A_HEADS),
            tile_gain(q_norm_b[l], B_HEADS), tile_gain(k_norm_b[l], B_KV_HEADS),
            (w_branch_a[l], w_branch_b[l], w_out[l], w_up[l], w_down[l]))
        km = km.reshape(batch, seq // MOBA_BLOCK, W_A)
        oa, ob = _attention(slopes_a * LOG2E, slopes_b * LOG2E, sinks_b[l] * LOG2E, qa, ka, vat, km, qb, kb, vbt,
                            batch, seq)
        x2 = _merge_mlp(x2, oa, ob, g, wa, wb, wo, norm_mlp[l][None, :], wup, wdn)
    return x2.reshape(batch, seq, d)
```

```python
import jax
import jax.numpy as jnp
from jax import lax
from jax.experimental import pallas as pl
from jax.experimental.pallas import tpu as pltpu

D_MODEL = 1024
HEAD_DIM = 64
A_HEADS = 8
B_HEADS = 8
B_KV_HEADS = 2
B_GROUP = B_HEADS // B_KV_HEADS
N_ATTN_HEADS = A_HEADS + B_HEADS
MOBA_BLOCK = 256
MOBA_TOPK = 3
SWA_WINDOW = 128
D_FF = 4 * D_MODEL
EPS = 1e-6
NEG = -1e30
SCALE = HEAD_DIM ** -0.5
LOG2E = 1.4426950408889634

W_A = A_HEADS * HEAD_DIM
W_QB = B_HEADS * HEAD_DIM
W_KB = B_KV_HEADS * HEAD_DIM
W_KB_DUP = 2 * W_KB
W_GATES = 2 * D_MODEL

C_QA = 0
C_KA = C_QA + W_A
C_VA = C_KA + W_A
C_QB = C_VA + W_A
C_KB = C_QB + W_QB
C_VB = C_KB + W_KB
C_G = C_VB + W_KB
C_END = C_G + W_GATES

N_INPROJ_OUT = 8
INPROJ_TILE = 1024
MERGE_TILE = 512
FF_CHUNK = 1024
ONES_ROWS = 16
ATTN_LOOKAHEAD = 4
SWA_SPAN = 0.9
VMEM_LIMIT = 48 * 1024 * 1024
ATTN_VMEM_LIMIT = 56 * 1024 * 1024

_NT = (((1,), (1,)), ((), ()))
_BF16 = jnp.bfloat16
_F32 = jnp.float32


def _const_spec(shape):
    return pl.BlockSpec(shape, lambda *_: (0,) * len(shape), pipeline_mode=pl.Buffered(1))


def _dynamic_zero():
    return jnp.minimum(pl.program_id(0), 0)


def _inproj_kernel(x_ref, gn_ref, w_ref, gqa_ref, gka_ref, gqb_ref, gkb_ref, *refs):
    n_cast = (len(refs) - N_INPROJ_OUT) // 2
    qa_ref, ka_ref, qb_ref, kb_ref, g_ref, vat_ref, vbt_ref, km_ref = refs[n_cast:n_cast + N_INPROJ_OUT]
    for src, dst in zip(refs[:n_cast], refs[n_cast + N_INPROJ_OUT:]):
        dst[...] = src[...].astype(_BF16)

    sub = MOBA_BLOCK
    n_sub = x_ref.shape[0] // sub
    lane = lax.broadcasted_iota(jnp.int32, (sub, 2 * HEAD_DIM), 1)
    first = lane < HEAD_DIM

    def normed(r):
        x = x_ref[r * sub:(r + 1) * sub, :]
        ms = jnp.mean(x * x, axis=-1, keepdims=True)
        return ((x * lax.rsqrt(ms + EPS)) * gn_ref[...]).astype(_BF16)

    def head_norm(y, gain_ref):
        parts = []
        for c in range(0, y.shape[-1], 2 * HEAD_DIM):
            yc = y[:, c:c + 2 * HEAD_DIM]
            sq = yc * yc
            s0 = jnp.sum(jnp.where(first, sq, 0.0), axis=-1, keepdims=True)
            s1 = jnp.sum(jnp.where(first, 0.0, sq), axis=-1, keepdims=True)
            msq = jnp.where(first, s0, s1) * (1.0 / HEAD_DIM)
            parts.append(yc * lax.rsqrt(msq + EPS))
        return jnp.concatenate(parts, axis=1) * gain_ref[...]

    def project(r, h):
        rows = slice(r * sub, (r + 1) * sub)

        def proj(lo, hi):
            return jnp.dot(h, w_ref[:, lo:hi], preferred_element_type=_F32)

        qa_ref[rows, :] = (head_norm(proj(C_QA, C_KA), gqa_ref) * (SCALE * LOG2E)).astype(_BF16)
        kn = head_norm(proj(C_KA, C_VA), gka_ref)
        ka_ref[rows, :] = kn.astype(_BF16)
        km_ref[0, r:r + 1, :] = jnp.sum(kn, axis=0, keepdims=True) * (1.0 / MOBA_BLOCK)
        qb_ref[rows, :] = (head_norm(proj(C_QB, C_KB), gqb_ref) * (SCALE * LOG2E)).astype(_BF16)
        g_ref[rows, :] = proj(C_G, C_END).astype(_BF16)

        kv = proj(C_KB, C_G)
        kb = head_norm(kv[:, :W_KB], gkb_ref)
        swapped = pltpu.roll(kb, HEAD_DIM, 1)
        kb_ref[rows, :] = jnp.concatenate([jnp.where(first, kb, swapped), jnp.where(first, swapped, kb)],
                                          axis=1).astype(_BF16)

        vat_ref[r] = proj(C_VA, C_QB).T.astype(_BF16)
        vbt = kv[:, W_KB:].T
        per = sub // SWA_WINDOW
        for c in range(per):
            vbt_ref[r * per + c] = vbt[:, c * SWA_WINDOW:(c + 1) * SWA_WINDOW].astype(_BF16)

    hs = [normed(r) for r in range(n_sub)]
    for r in range(n_sub):
        project(r, hs[r])


def _inproj(x2, gn, w_in, gqa, gka, gqb, gkb, later_weights):
    n = x2.shape[0]
    tm = INPROJ_TILE
    steps = n // tm
    row = lambda w: pl.BlockSpec((tm, w), lambda i: (i, 0))
    slabs = [w.reshape(steps, w.shape[0] // steps, w.shape[1]) for w in later_weights]
    slab_specs = [pl.BlockSpec((1,) + s.shape[1:], lambda i: (i, 0, 0)) for s in slabs]
    out_shape = (
        jax.ShapeDtypeStruct((n, W_A), _BF16),
        jax.ShapeDtypeStruct((n, W_A), _BF16),
        jax.ShapeDtypeStruct((n, W_QB), _BF16),
        jax.ShapeDtypeStruct((n, W_KB_DUP), _BF16),
        jax.ShapeDtypeStruct((n, W_GATES), _BF16),
        jax.ShapeDtypeStruct((n // MOBA_BLOCK, W_A, MOBA_BLOCK), _BF16),
        jax.ShapeDtypeStruct((n // SWA_WINDOW, W_KB, SWA_WINDOW), _BF16),
        jax.ShapeDtypeStruct((n // tm, tm // MOBA_BLOCK, W_A), _F32),
    )
    out_specs = (
        row(W_A), row(W_A), row(W_QB), row(W_KB_DUP), row(W_GATES),
        pl.BlockSpec((tm // MOBA_BLOCK, W_A, MOBA_BLOCK), lambda i: (i, 0, 0)),
        pl.BlockSpec((tm // SWA_WINDOW, W_KB, SWA_WINDOW), lambda i: (i, 0, 0)),
        pl.BlockSpec((1, tm // MOBA_BLOCK, W_A), lambda i: (i, 0, 0)),
    )
    in_specs = [row(D_MODEL), _const_spec(gn.shape), _const_spec(w_in.shape),
                _const_spec(gqa.shape), _const_spec(gka.shape),
                _const_spec(gqb.shape), _const_spec(gkb.shape)]
    outs = pl.pallas_call(
        _inproj_kernel, grid=(steps,), in_specs=in_specs + slab_specs,
        out_specs=out_specs + tuple(slab_specs),
        out_shape=out_shape + tuple(jax.ShapeDtypeStruct(s.shape, _BF16) for s in slabs),
        compiler_params=pltpu.CompilerParams(dimension_semantics=("parallel",),
                                             vmem_limit_bytes=VMEM_LIMIT),
        name="inproj",
    )(x2, gn, w_in, gqa, gka, gqb, gkb, *slabs)
    assert len(out_shape) == N_INPROJ_OUT
    return outs[:N_INPROJ_OUT], [o.reshape(w.shape) for o, w in zip(outs[N_INPROJ_OUT:], later_weights)]


def _moba_items(slopes_ref, q_ref, k_ref, vt_ref, km_ref, o_ref, kaug_sc, vaug_sc, causal_sc, s_sc, dyn0):
    blk = MOBA_BLOCK
    nb = q_ref.shape[0] // blk
    pair = 2 * HEAD_DIM
    npair = q_ref.shape[1] // pair
    kp = lax.broadcasted_iota(jnp.int32, (blk, blk), 0)
    qp = lax.broadcasted_iota(jnp.int32, (blk, blk), 1)
    causal_sc[...] = jnp.where(kp <= qp, 0.0, NEG)
    lane = lax.broadcasted_iota(jnp.int32, (blk, pair), 1)
    prow = lax.broadcasted_iota(jnp.int32, (blk, pair), 0).astype(_F32)
    ridx = lax.broadcasted_iota(jnp.int32, (nb, blk), 0)
    km_lane = lax.broadcasted_iota(jnp.int32, (nb, pair), 1)
    in_head = [(lane >= e * HEAD_DIM) & (lane < (e + 1) * HEAD_DIM) for e in range(2)]

    def prepare(lp):
        lanes = slice(lp * pair, (lp + 1) * pair)
        heads = []
        for e in range(2):
            h = 2 * lp + e
            slope = slopes_ref[h]
            a = (1 - e) * HEAD_DIM
            sv = jnp.full((blk, pair), slope, _F32)
            hi = sv.astype(_BF16).astype(_F32)
            mid = (sv - hi).astype(_BF16).astype(_F32)
            lo = sv - hi - mid
            pieces = jnp.where(lane == a, hi, jnp.where(lane == a + 1, mid, jnp.where(lane == a + 2, lo, 0.0)))
            k_aug = jnp.where((lane >= a) & (lane < a + 3), prow, 0.0).astype(_BF16)
            for n in range(nb):
                rows = slice(n * blk, (n + 1) * blk)
                kaug_sc[h, rows, :] = jnp.where(in_head[e], k_ref[rows, lanes], k_aug)
                vaug_sc[h, n, :HEAD_DIM, :] = vt_ref[n, h * HEAD_DIM:(h + 1) * HEAD_DIM, :]
                vaug_sc[h, n, HEAD_DIM:, :] = jnp.ones((vaug_sc.shape[2] - HEAD_DIM, blk), _BF16)
            km_e = (km_lane >= e * HEAD_DIM) & (km_lane < (e + 1) * HEAD_DIM)
            km_head = jnp.where(km_e, km_ref[0, :, lanes], 0.0).astype(_BF16)
            heads.append((slope, pieces.astype(_BF16), km_head))
        return heads

    prepared = {}

    def scores(lp, i, e, slot):
        if lp not in prepared:
            prepared[lp] = prepare(lp)
        slope, q_aug, km_head = prepared[lp][e]
        h = 2 * lp + e
        qm = jnp.where(in_head[e], q_ref[i * blk:(i + 1) * blk, lp * pair:(lp + 1) * pair], q_aug)
        gs = lax.dot_general(km_head, qm, _NT, preferred_element_type=_F32)
        radj = []
        for n in range(i):
            row = gs[n:n + 1, :]
            ahead = ((gs > row) | ((gs == row) & (ridx < n))) & (ridx < i)
            rank = jnp.sum(ahead.astype(_F32), axis=0, keepdims=True)
            radj.append(jnp.where(rank < MOBA_TOPK, 0.0, NEG) - slope * float(blk * (i - n)))
        m = None
        for n in range(i + 1):
            t = lax.dot_general(kaug_sc[h, n * blk:(n + 1) * blk, :], qm, _NT, preferred_element_type=_F32)
            if n == i:
                t = t + causal_sc[...]
            s_sc[slot, n + dyn0] = t
            bm = jnp.max(t, axis=0, keepdims=True)
            if n < i:
                bm = bm + radj[n]
            m = bm if m is None else jnp.maximum(m, bm)
        return [m - radj[n] if n < i else m for n in range(i + 1)]

    def weighted_values(lp, i, e, slot, shifts):
        acc = None
        for n in range(i + 1):
            p = jnp.exp2(s_sc[slot, n + dyn0] - shifts[n]).astype(_BF16)
            pv = jnp.dot(vaug_sc[2 * lp + e, n], p, preferred_element_type=_F32)
            acc = pv if acc is None else acc + pv
        return acc[:HEAD_DIM] / acc[HEAD_DIM:HEAD_DIM + 1]

    outs = {}

    def item(lp, i, e):
        def finish(slot, shifts):
            outs[e] = weighted_values(lp, i, e, slot, shifts)
            if e == 1:
                o = jnp.concatenate([outs.pop(0), outs.pop(1)], axis=0)
                o_ref[i * blk:(i + 1) * blk, lp * pair:(lp + 1) * pair] = o.astype(o_ref.dtype).T

        return (lambda slot: scores(lp, i, e, slot)), finish, i + 1

    return [item(lp, i, e) for lp in range(npair) for i in reversed(range(nb)) for e in range(2)]


def _swa_items(slopes_ref, sinks_ref, q_ref, k_ref, vt_ref, o_ref, bias_sc, vaug_sc, s_sc, dyn0):
    w = SWA_WINDOW
    sq = w // 2
    span = w + sq
    ncol = B_GROUP * sq
    ntile = q_ref.shape[0] // sq
    nblk = q_ref.shape[0] // w

    def per_head(hk, col, ref):
        out = ref[hk * B_GROUP]
        for h in range(1, B_GROUP):
            out = jnp.where(col >= h * sq, ref[hk * B_GROUP + h], out)
        return out

    kp = lax.broadcasted_iota(jnp.int32, (span, ncol), 0)
    col = lax.broadcasted_iota(jnp.int32, (span, ncol), 1)
    dist = (col & (sq - 1)) + w - kp
    lane = lax.broadcasted_iota(jnp.int32, (sq, 2 * HEAD_DIM), 1)
    first = lane < HEAD_DIM
    ones = jnp.ones((vaug_sc.shape[2] - HEAD_DIM, 2 * w), _BF16)

    def prepare(hk):
        slope = per_head(hk, col, slopes_ref)
        bias_sc[hk] = jnp.where((dist >= 0) & (dist < w), -slope * dist.astype(_F32), NEG)
        v_rows = slice(hk * HEAD_DIM, (hk + 1) * HEAD_DIM)
        for j in range(nblk):
            prev = vt_ref[j - 1, v_rows, :] if j > 0 else jnp.zeros((HEAD_DIM, w), _BF16)
            vaug_sc[hk, j, :HEAD_DIM, :] = jnp.concatenate([prev, vt_ref[j, v_rows, :]], axis=1)
            vaug_sc[hk, j, HEAD_DIM:, :] = ones
        return per_head(hk, lax.broadcasted_iota(jnp.int32, (1, ncol), 1), sinks_ref)

    sinks = {}

    def key_range(t):
        k0 = max(t - 2, 0) * sq
        return k0, (t + 1) * sq - k0

    def scores(hk, t, slot):
        if hk not in sinks:
            sinks[hk] = prepare(hk)
        q_t = q_ref[t * sq:(t + 1) * sq, hk * ncol:(hk + 1) * ncol]
        zero = jnp.zeros((sq, 2 * HEAD_DIM), q_t.dtype)
        stacked = []
        for pr in range(B_GROUP // 2):
            pair = q_t[:, pr * 2 * HEAD_DIM:(pr + 1) * 2 * HEAD_DIM]
            stacked += [jnp.where(first, pair, zero), jnp.where(first, zero, pair)]
        qs = jnp.concatenate(stacked, axis=0)
        k0, nk = key_range(t)
        keys = k_ref[k0:k0 + nk, hk * 2 * HEAD_DIM:(hk + 1) * 2 * HEAD_DIM]
        s = lax.dot_general(keys, qs, _NT, preferred_element_type=_F32)
        s = s + bias_sc[hk, span - nk:, :]
        s_sc[slot + dyn0, :nk, :] = s
        return jnp.maximum(jnp.max(s, axis=0, keepdims=True), sinks[hk])

    def weighted_values(hk, t, slot, m):
        k0, nk = key_range(t)
        pb = jnp.exp2(s_sc[slot + dyn0, :nk, :] - m).astype(_BF16)
        before = k0 - (t // 2 - 1) * w
        after = 2 * w - before - nk
        pad = lambda rows: [jnp.zeros((rows, ncol), _BF16)] if rows else []
        p_full = jnp.concatenate(pad(before) + [pb] + pad(after), axis=0)
        ot = jnp.dot(vaug_sc[hk, t // 2], p_full, preferred_element_type=_F32)
        den = ot[HEAD_DIM:HEAD_DIM + 1] + jnp.exp2(sinks[hk] - m)
        return ot[:HEAD_DIM] / den

    def store(hk, j, even, odd):
        even, odd = even.astype(o_ref.dtype), odd.astype(o_ref.dtype)
        for pr in range(B_GROUP // 2):
            lanes = slice(pr * 2 * HEAD_DIM, (pr + 1) * 2 * HEAD_DIM)
            e, o = even[:, lanes], odd[:, lanes]
            xa = jnp.where(first, e, pltpu.roll(o, sq, 1))
            xb = jnp.where(first, pltpu.roll(e, sq, 1), o)
            x = jnp.concatenate([xa, xb], axis=0)
            lane0 = hk * ncol + pr * 2 * HEAD_DIM
            o_ref[j * w:(j + 1) * w, lane0:lane0 + 2 * HEAD_DIM] = x.T

    pending = {}

    def item(hk, t):
        def finish(slot, m):
            out = weighted_values(hk, t, slot, m)
            if t % 2 == 0:
                pending[hk] = out
            else:
                store(hk, t // 2, pending.pop(hk), out)

        return (lambda slot: scores(hk, t, slot)), finish, 1

    return [item(hk, t) for hk in range(q_ref.shape[1] // ncol) for t in range(ntile)]


def _attention_kernel(slopes_a_ref, slopes_b_ref, sinks_ref, qa_ref, ka_ref, vat_ref, km_ref, qb_ref, kb_ref,
                      vbt_ref, oa_ref, ob_ref, kaug_sc, vaug_a_sc, causal_sc, sa_sc, bias_sc, vaug_b_sc, sb_sc):
    dyn0 = _dynamic_zero()
    moba = _moba_items(slopes_a_ref, qa_ref, ka_ref, vat_ref, km_ref, oa_ref, kaug_sc, vaug_a_sc, causal_sc,
                       sa_sc, dyn0)
    swa = _swa_items(slopes_b_ref, sinks_ref, qb_ref, kb_ref, vbt_ref, ob_ref, bias_sc, vaug_b_sc, sb_sc, dyn0)
    items = []
    for kind, (group, nslot, span) in enumerate(((moba, sa_sc.shape[0], 1.0), (swa, sb_sc.shape[0], SWA_SPAN))):
        total = sum(work for _, _, work in group)
        before = 0
        for k, (scores, finish, work) in enumerate(group):
            items.append((span * (before + work / 2) / total, kind, k % nslot, scores, finish))
            before += work
    items.sort(key=lambda it: it[:2])
    state = {}
    for u in range(len(items) + ATTN_LOOKAHEAD):
        if u < len(items):
            _, _, slot, scores, _ = items[u]
            state[u] = scores(slot)
        done = u - ATTN_LOOKAHEAD
        if done >= 0:
            _, _, slot, _, finish = items[done]
            finish(slot, state.pop(done))


def _attention(slopes_a, slopes_b, sinks, qa, ka, vat, km, qb, kb, vbt, batch, seq):
    nb = seq // MOBA_BLOCK
    nblk = seq // SWA_WINDOW
    grp = B_GROUP * HEAD_DIM
    span = SWA_WINDOW + SWA_WINDOW // 2
    smem = pl.BlockSpec(memory_space=pltpu.SMEM)
    rows = lambda width: pl.BlockSpec((seq, width), lambda b: (b, 0))
    slots = ATTN_LOOKAHEAD + 1
    return pl.pallas_call(
        _attention_kernel, grid=(batch,),
        in_specs=[smem, smem, smem, rows(W_A), rows(W_A),
                  pl.BlockSpec((nb, W_A, MOBA_BLOCK), lambda b: (b, 0, 0)),
                  pl.BlockSpec((1, nb, W_A), lambda b: (b, 0, 0)),
                  rows(W_QB), rows(W_KB_DUP),
                  pl.BlockSpec((nblk, W_KB, SWA_WINDOW), lambda b: (b, 0, 0))],
        out_specs=(rows(W_A), rows(W_QB)),
        out_shape=(jax.ShapeDtypeStruct(qa.shape, _BF16), jax.ShapeDtypeStruct(qb.shape, _BF16)),
        scratch_shapes=[pltpu.VMEM((A_HEADS, seq, 2 * HEAD_DIM), _BF16),
                        pltpu.VMEM((A_HEADS, nb, HEAD_DIM + ONES_ROWS, MOBA_BLOCK), _BF16),
                        pltpu.VMEM((MOBA_BLOCK, MOBA_BLOCK), _F32),
                        pltpu.VMEM((slots, nb, MOBA_BLOCK, MOBA_BLOCK), _F32),
                        pltpu.VMEM((B_KV_HEADS, span, grp), _F32),
                        pltpu.VMEM((B_KV_HEADS, nblk, HEAD_DIM + ONES_ROWS, 2 * SWA_WINDOW), _BF16),
                        pltpu.VMEM((slots, span, grp), _F32)],
        compiler_params=pltpu.CompilerParams(dimension_semantics=("parallel",),
                                             vmem_limit_bytes=ATTN_VMEM_LIMIT),
        name="attention",
    )(slopes_a, slopes_b, sinks, qa, ka, vat, km, qb, kb, vbt)


def _merge_mlp_kernel(x_ref, oa_ref, ob_ref, g_ref, wa_ref, wb_ref, wo_ref, gm_ref, wup_ref, wdn_ref, o_ref):
    a = jnp.dot(oa_ref[...], wa_ref[...], preferred_element_type=_F32)
    b = jnp.dot(ob_ref[...], wb_ref[...], preferred_element_type=_F32)
    ga = g_ref[:, :D_MODEL].astype(_F32)
    gb = g_ref[:, D_MODEL:].astype(_F32)
    mixed = jax.nn.sigmoid(ga) * a + jax.nn.sigmoid(gb) * b
    x1 = x_ref[...] + jnp.dot(mixed.astype(_BF16), wo_ref[...], preferred_element_type=_F32)
    ms = jnp.mean(x1 * x1, axis=-1, keepdims=True)
    h2 = ((x1 * lax.rsqrt(ms + EPS)) * gm_ref[...]).astype(_BF16)
    acc = x1
    for c in range(D_FF // FF_CHUNK):
        u = jnp.dot(h2, wup_ref[:, c * FF_CHUNK:(c + 1) * FF_CHUNK], preferred_element_type=_F32)
        u = jnp.square(jnp.maximum(u, 0.0)).astype(_BF16)
        acc = acc + jnp.dot(u, wdn_ref[c * FF_CHUNK:(c + 1) * FF_CHUNK, :], preferred_element_type=_F32)
    o_ref[...] = acc


def _merge_mlp(x2, oa, ob, g, wa, wb, wo, gm, wup, wdn):
    n = x2.shape[0]
    tm = MERGE_TILE
    row = lambda w: pl.BlockSpec((tm, w), lambda i: (i, 0))
    return pl.pallas_call(
        _merge_mlp_kernel, grid=(n // tm,),
        in_specs=[row(D_MODEL), row(W_A), row(W_QB), row(W_GATES), _const_spec(wa.shape), _const_spec(wb.shape),
                  _const_spec(wo.shape), _const_spec(gm.shape), _const_spec(wup.shape), _const_spec(wdn.shape)],
        out_specs=row(D_MODEL),
        out_shape=jax.ShapeDtypeStruct(x2.shape, x2.dtype),
        compiler_params=pltpu.CompilerParams(dimension_semantics=("parallel",),
                                             vmem_limit_bytes=VMEM_LIMIT),
        name="merge_mlp",
    )(x2, oa, ob, g, wa, wb, wo, gm, wup, wdn)


def _alibi_slopes(n):
    return jnp.exp2(-(8.0 / n) * jnp.arange(1, n + 1, dtype=_F32))


def kernel(x, norm_attn, w_in, q_norm_a, k_norm_a, q_norm_b, k_norm_b, sinks_b, w_branch_a, w_branch_b, w_out,
           norm_mlp, w_up, w_down):
    batch, seq, d = x.shape
    assert d == D_MODEL and (batch * seq) % INPROJ_TILE == 0 and (batch * seq) % MERGE_TILE == 0
    assert seq % MOBA_BLOCK == 0 and INPROJ_TILE % MOBA_BLOCK == 0
    slopes = _alibi_slopes(N_ATTN_HEADS)
    slopes_b, slopes_a = slopes[:B_HEADS], slopes[B_HEADS:]
    x2 = x.reshape(batch * seq, d)
    for l in range(norm_attn.shape[0]):
        tile_gain = lambda g, reps: jnp.tile(g, reps)[None, :]
        (qa, ka, qb, kb, g, vat, vbt, km), (wa, wb, wo, wup, wdn) = _inproj(
            x2, norm_attn[l][None, :], w_in[l].astype(_BF16),
            tile_gain(q_norm_a[l], A_HEADS), tile_gain(k_norm_a[l], A_HEADS),
            tile_gain(q_norm_b[l], B_HEADS), tile_gain(k_norm_b[l], B_KV_HEADS),
            (w_branch_a[l], w_branch_b[l], w_out[l], w_up[l], w_down[l]))
        km = km.reshape(batch, seq // MOBA_BLOCK, W_A)
        oa, ob = _attention(slopes_a * LOG2E, slopes_b * LOG2E, sinks_b[l] * LOG2E, qa, ka, vat, km, qb, kb, vbt,
                            batch, seq)
        x2 = _merge_mlp(x2, oa, ob, g, wa, wb, wo, norm_mlp[l][None, :], wup, wdn)
    return x2.reshape(batch, seq, d)
```

```python
import jax
import jax.numpy as jnp
from jax import lax
from jax.experimental import pallas as pl
from jax.experimental.pallas import tpu as pltpu

D_MODEL = 1024
HEAD_DIM = 64
A_HEADS = 8
B_HEADS = 8
B_KV_HEADS = 2
B_GROUP = B_HEADS // B_KV_HEADS
N_ATTN_HEADS = A_HEADS + B_HEADS
MOBA_BLOCK = 256
MOBA_TOPK = 3
SWA_WINDOW = 128
D_FF = 4 * D_MODEL
EPS = 1e-6
NEG = -1e30
SCALE = HEAD_DIM ** -0.5
LOG2E = 1.4426950408889634

W_A = A_HEADS * HEAD_DIM
W_QB = B_HEADS * HEAD_DIM
W_KB = B_KV_HEADS * HEAD_DIM
W_KB_DUP = 2 * W_KB
W_GATES = 2 * D_MODEL

C_QA = 0
C_KA = C_QA + W_A
C_VA = C_KA + W_A
C_QB = C_VA + W_A
C_KB = C_QB + W_QB
C_VB = C_KB + W_KB
C_G = C_VB + W_KB
C_END = C_G + W_GATES

N_INPROJ_OUT = 8
INPROJ_TILE = 1024
MERGE_TILE = 512
FF_CHUNK = 1024
ONES_ROWS = 16
ATTN_LOOKAHEAD = 4
VMEM_LIMIT = 48 * 1024 * 1024
ATTN_VMEM_LIMIT = 56 * 1024 * 1024

_NT = (((1,), (1,)), ((), ()))
_BF16 = jnp.bfloat16
_F32 = jnp.float32


def _const_spec(shape):
    return pl.BlockSpec(shape, lambda *_: (0,) * len(shape), pipeline_mode=pl.Buffered(1))


def _dynamic_zero():
    return jnp.minimum(pl.program_id(0), 0)


def _inproj_kernel(x_ref, gn_ref, w_ref, gqa_ref, gka_ref, gqb_ref, gkb_ref, *refs):
    n_cast = (len(refs) - N_INPROJ_OUT) // 2
    qa_ref, ka_ref, qb_ref, kb_ref, g_ref, vat_ref, vbt_ref, km_ref = refs[n_cast:n_cast + N_INPROJ_OUT]
    for src, dst in zip(refs[:n_cast], refs[n_cast + N_INPROJ_OUT:]):
        dst[...] = src[...].astype(_BF16)

    sub = MOBA_BLOCK
    n_sub = x_ref.shape[0] // sub
    lane = lax.broadcasted_iota(jnp.int32, (sub, 2 * HEAD_DIM), 1)
    first = lane < HEAD_DIM

    def normed(r):
        x = x_ref[r * sub:(r + 1) * sub, :]
        ms = jnp.mean(x * x, axis=-1, keepdims=True)
        return ((x * lax.rsqrt(ms + EPS)) * gn_ref[...]).astype(_BF16)

    def head_norm(y, gain_ref):
        parts = []
        for c in range(0, y.shape[-1], 2 * HEAD_DIM):
            yc = y[:, c:c + 2 * HEAD_DIM]
            sq = yc * yc
            s0 = jnp.sum(jnp.where(first, sq, 0.0), axis=-1, keepdims=True)
            s1 = jnp.sum(jnp.where(first, 0.0, sq), axis=-1, keepdims=True)
            msq = jnp.where(first, s0, s1) * (1.0 / HEAD_DIM)
            parts.append(yc * lax.rsqrt(msq + EPS))
        return jnp.concatenate(parts, axis=1) * gain_ref[...]

    def project(r, h):
        rows = slice(r * sub, (r + 1) * sub)

        def proj(lo, hi):
            return jnp.dot(h, w_ref[:, lo:hi], preferred_element_type=_F32)

        qa_ref[rows, :] = (head_norm(proj(C_QA, C_KA), gqa_ref) * (SCALE * LOG2E)).astype(_BF16)
        kn = head_norm(proj(C_KA, C_VA), gka_ref)
        ka_ref[rows, :] = kn.astype(_BF16)
        km_ref[0, r:r + 1, :] = jnp.sum(kn, axis=0, keepdims=True) * (1.0 / MOBA_BLOCK)
        qb_ref[rows, :] = (head_norm(proj(C_QB, C_KB), gqb_ref) * (SCALE * LOG2E)).astype(_BF16)

        kv = proj(C_KB, C_G)
        kb = head_norm(kv[:, :W_KB], gkb_ref)
        swapped = pltpu.roll(kb, HEAD_DIM, 1)
        kb_ref[rows, :] = jnp.concatenate([jnp.where(first, kb, swapped), jnp.where(first, swapped, kb)],
                                          axis=1).astype(_BF16)

        vat_ref[r] = proj(C_VA, C_QB).T.astype(_BF16)
        vbt = kv[:, W_KB:].T
        per = sub // SWA_WINDOW
        for c in range(per):
            vbt_ref[r * per + c] = vbt[:, c * SWA_WINDOW:(c + 1) * SWA_WINDOW].astype(_BF16)

        g_ref[rows, :] = proj(C_G, C_END).astype(_BF16)

    hs = [normed(r) for r in range(n_sub)]
    for r in range(n_sub):
        project(r, hs[r])


def _inproj(x2, gn, w_in, gqa, gka, gqb, gkb, later_weights):
    n = x2.shape[0]
    tm = INPROJ_TILE
    steps = n // tm
    row = lambda w: pl.BlockSpec((tm, w), lambda i: (i, 0))
    slabs = [w.reshape(steps, w.shape[0] // steps, w.shape[1]) for w in later_weights]
    slab_specs = [pl.BlockSpec((1,) + s.shape[1:], lambda i: (i, 0, 0)) for s in slabs]
    out_shape = (
        jax.ShapeDtypeStruct((n, W_A), _BF16),
        jax.ShapeDtypeStruct((n, W_A), _BF16),
        jax.ShapeDtypeStruct((n, W_QB), _BF16),
        jax.ShapeDtypeStruct((n, W_KB_DUP), _BF16),
        jax.ShapeDtypeStruct((n, W_GATES), _BF16),
        jax.ShapeDtypeStruct((n // MOBA_BLOCK, W_A, MOBA_BLOCK), _BF16),
        jax.ShapeDtypeStruct((n // SWA_WINDOW, W_KB, SWA_WINDOW), _BF16),
        jax.ShapeDtypeStruct((n // tm, tm // MOBA_BLOCK, W_A), _F32),
    )
    out_specs = (
        row(W_A), row(W_A), row(W_QB), row(W_KB_DUP), row(W_GATES),
        pl.BlockSpec((tm // MOBA_BLOCK, W_A, MOBA_BLOCK), lambda i: (i, 0, 0)),
        pl.BlockSpec((tm // SWA_WINDOW, W_KB, SWA_WINDOW), lambda i: (i, 0, 0)),
        pl.BlockSpec((1, tm // MOBA_BLOCK, W_A), lambda i: (i, 0, 0)),
    )
    in_specs = [row(D_MODEL), _const_spec(gn.shape), _const_spec(w_in.shape),
                _const_spec(gqa.shape), _const_spec(gka.shape),
                _const_spec(gqb.shape), _const_spec(gkb.shape)]
    outs = pl.pallas_call(
        _inproj_kernel, grid=(steps,), in_specs=in_specs + slab_specs,
        out_specs=out_specs + tuple(slab_specs),
        out_shape=out_shape + tuple(jax.ShapeDtypeStruct(s.shape, _BF16) for s in slabs),
        compiler_params=pltpu.CompilerParams(dimension_semantics=("parallel",),
                                             vmem_limit_bytes=VMEM_LIMIT),
        name="inproj",
    )(x2, gn, w_in, gqa, gka, gqb, gkb, *slabs)
    assert len(out_shape) == N_INPROJ_OUT
    return outs[:N_INPROJ_OUT], [o.reshape(w.shape) for o, w in zip(outs[N_INPROJ_OUT:], later_weights)]


def _moba_items(slopes_ref, q_ref, k_ref, vt_ref, km_ref, o_ref, kaug_sc, vaug_sc, causal_sc, s_sc, dyn0):
    blk = MOBA_BLOCK
    nb = q_ref.shape[0] // blk
    pair = 2 * HEAD_DIM
    npair = q_ref.shape[1] // pair
    kp = lax.broadcasted_iota(jnp.int32, (blk, blk), 0)
    qp = lax.broadcasted_iota(jnp.int32, (blk, blk), 1)
    causal_sc[...] = jnp.where(kp <= qp, 0.0, NEG)
    lane = lax.broadcasted_iota(jnp.int32, (blk, pair), 1)
    prow = lax.broadcasted_iota(jnp.int32, (blk, pair), 0).astype(_F32)
    ridx = lax.broadcasted_iota(jnp.int32, (nb, blk), 0)
    km_lane = lax.broadcasted_iota(jnp.int32, (nb, pair), 1)
    in_head = [(lane >= e * HEAD_DIM) & (lane < (e + 1) * HEAD_DIM) for e in range(2)]

    def prepare(lp):
        lanes = slice(lp * pair, (lp + 1) * pair)
        heads = []
        for e in range(2):
            h = 2 * lp + e
            slope = slopes_ref[h]
            a = (1 - e) * HEAD_DIM
            sv = jnp.full((blk, pair), slope, _F32)
            hi = sv.astype(_BF16).astype(_F32)
            mid = (sv - hi).astype(_BF16).astype(_F32)
            lo = sv - hi - mid
            pieces = jnp.where(lane == a, hi, jnp.where(lane == a + 1, mid, jnp.where(lane == a + 2, lo, 0.0)))
            k_aug = jnp.where((lane >= a) & (lane < a + 3), prow, 0.0).astype(_BF16)
            for n in range(nb):
                rows = slice(n * blk, (n + 1) * blk)
                kaug_sc[h, rows, :] = jnp.where(in_head[e], k_ref[rows, lanes], k_aug)
                vaug_sc[h, n, :HEAD_DIM, :] = vt_ref[n, h * HEAD_DIM:(h + 1) * HEAD_DIM, :]
                vaug_sc[h, n, HEAD_DIM:, :] = jnp.ones((vaug_sc.shape[2] - HEAD_DIM, blk), _BF16)
            km_e = (km_lane >= e * HEAD_DIM) & (km_lane < (e + 1) * HEAD_DIM)
            km_head = jnp.where(km_e, km_ref[0, :, lanes], 0.0).astype(_BF16)
            heads.append((slope, pieces.astype(_BF16), km_head))
        return heads

    prepared = {}

    def scores(lp, i, e, slot):
        if lp not in prepared:
            prepared[lp] = prepare(lp)
        slope, q_aug, km_head = prepared[lp][e]
        h = 2 * lp + e
        qm = jnp.where(in_head[e], q_ref[i * blk:(i + 1) * blk, lp * pair:(lp + 1) * pair], q_aug)
        gs = lax.dot_general(km_head, qm, _NT, preferred_element_type=_F32)
        radj = []
        for n in range(i):
            row = gs[n:n + 1, :]
            ahead = ((gs > row) | ((gs == row) & (ridx < n))) & (ridx < i)
            rank = jnp.sum(ahead.astype(_F32), axis=0, keepdims=True)
            radj.append(jnp.where(rank < MOBA_TOPK, 0.0, NEG) - slope * float(blk * (i - n)))
        m = None
        for n in range(i + 1):
            t = lax.dot_general(kaug_sc[h, n * blk:(n + 1) * blk, :], qm, _NT, preferred_element_type=_F32)
            if n == i:
                t = t + causal_sc[...]
            s_sc[slot, n + dyn0] = t
            bm = jnp.max(t, axis=0, keepdims=True)
            if n < i:
                bm = bm + radj[n]
            m = bm if m is None else jnp.maximum(m, bm)
        return [m - radj[n] if n < i else m for n in range(i + 1)]

    def weighted_values(lp, i, e, slot, shifts):
        acc = None
        for n in range(i + 1):
            p = jnp.exp2(s_sc[slot, n + dyn0] - shifts[n]).astype(_BF16)
            pv = jnp.dot(vaug_sc[2 * lp + e, n], p, preferred_element_type=_F32)
            acc = pv if acc is None else acc + pv
        return acc[:HEAD_DIM] / acc[HEAD_DIM:HEAD_DIM + 1]

    outs = {}

    def item(lp, i, e):
        def finish(slot, shifts):
            outs[e] = weighted_values(lp, i, e, slot, shifts)
            if e == 1:
                o = jnp.concatenate([outs.pop(0), outs.pop(1)], axis=0)
                o_ref[i * blk:(i + 1) * blk, lp * pair:(lp + 1) * pair] = o.astype(o_ref.dtype).T

        return (lambda slot: scores(lp, i, e, slot)), finish, i + 1

    return [item(lp, i, e) for lp in range(npair) for i in reversed(range(nb)) for e in range(2)]


def _swa_items(slopes_ref, sinks_ref, q_ref, k_ref, vt_ref, o_ref, bias_sc, vaug_sc, s_sc, dyn0):
    w = SWA_WINDOW
    sq = w // 2
    span = w + sq
    ncol = B_GROUP * sq
    ntile = q_ref.shape[0] // sq
    nblk = q_ref.shape[0] // w

    def per_head(hk, col, ref):
        out = ref[hk * B_GROUP]
        for h in range(1, B_GROUP):
            out = jnp.where(col >= h * sq, ref[hk * B_GROUP + h], out)
        return out

    kp = lax.broadcasted_iota(jnp.int32, (span, ncol), 0)
    col = lax.broadcasted_iota(jnp.int32, (span, ncol), 1)
    dist = (col & (sq - 1)) + w - kp
    lane = lax.broadcasted_iota(jnp.int32, (sq, 2 * HEAD_DIM), 1)
    first = lane < HEAD_DIM
    ones = jnp.ones((vaug_sc.shape[2] - HEAD_DIM, 2 * w), _BF16)

    def prepare(hk):
        slope = per_head(hk, col, slopes_ref)
        bias_sc[hk] = jnp.where((dist >= 0) & (dist < w), -slope * dist.astype(_F32), NEG)
        v_rows = slice(hk * HEAD_DIM, (hk + 1) * HEAD_DIM)
        for j in range(nblk):
            prev = vt_ref[j - 1, v_rows, :] if j > 0 else jnp.zeros((HEAD_DIM, w), _BF16)
            vaug_sc[hk, j, :HEAD_DIM, :] = jnp.concatenate([prev, vt_ref[j, v_rows, :]], axis=1)
            vaug_sc[hk, j, HEAD_DIM:, :] = ones
        return per_head(hk, lax.broadcasted_iota(jnp.int32, (1, ncol), 1), sinks_ref)

    sinks = {}

    def key_range(t):
        k0 = max(t - 2, 0) * sq
        return k0, (t + 1) * sq - k0

    def scores(hk, t, slot):
        if hk not in sinks:
            sinks[hk] = prepare(hk)
        q_t = q_ref[t * sq:(t + 1) * sq, hk * ncol:(hk + 1) * ncol]
        zero = jnp.zeros((sq, 2 * HEAD_DIM), q_t.dtype)
        stacked = []
        for pr in range(B_GROUP // 2):
            pair = q_t[:, pr * 2 * HEAD_DIM:(pr + 1) * 2 * HEAD_DIM]
            stacked += [jnp.where(first, pair, zero), jnp.where(first, zero, pair)]
        qs = jnp.concatenate(stacked, axis=0)
        k0, nk = key_range(t)
        keys = k_ref[k0:k0 + nk, hk * 2 * HEAD_DIM:(hk + 1) * 2 * HEAD_DIM]
        s = lax.dot_general(keys, qs, _NT, preferred_element_type=_F32)
        s = s + bias_sc[hk, span - nk:, :]
        s_sc[slot + dyn0, :nk, :] = s
        return jnp.maximum(jnp.max(s, axis=0, keepdims=True), sinks[hk])

    def weighted_values(hk, t, slot, m):
        k0, nk = key_range(t)
        pb = jnp.exp2(s_sc[slot + dyn0, :nk, :] - m).astype(_BF16)
        before = k0 - (t // 2 - 1) * w
        after = 2 * w - before - nk
        pad = lambda rows: [jnp.zeros((rows, ncol), _BF16)] if rows else []
        p_full = jnp.concatenate(pad(before) + [pb] + pad(after), axis=0)
        ot = jnp.dot(vaug_sc[hk, t // 2], p_full, preferred_element_type=_F32)
        den = ot[HEAD_DIM:HEAD_DIM + 1] + jnp.exp2(sinks[hk] - m)
        return ot[:HEAD_DIM] / den

    def store(hk, j, even, odd):
        even, odd = even.astype(o_ref.dtype), odd.astype(o_ref.dtype)
        for pr in range(B_GROUP // 2):
            lanes = slice(pr * 2 * HEAD_DIM, (pr + 1) * 2 * HEAD_DIM)
            e, o = even[:, lanes], odd[:, lanes]
            xa = jnp.where(first, e, pltpu.roll(o, sq, 1))
            xb = jnp.where(first, pltpu.roll(e, sq, 1), o)
            x = jnp.concatenate([xa, xb], axis=0)
            lane0 = hk * ncol + pr * 2 * HEAD_DIM
            o_ref[j * w:(j + 1) * w, lane0:lane0 + 2 * HEAD_DIM] = x.T

    pending = {}

    def item(hk, t):
        def finish(slot, m):
            out = weighted_values(hk, t, slot, m)
            if t % 2 == 0:
                pending[hk] = out
            else:
                store(hk, t // 2, pending.pop(hk), out)

        return (lambda slot: scores(hk, t, slot)), finish, 1

    return [item(hk, t) for hk in range(q_ref.shape[1] // ncol) for t in range(ntile)]


def _attention_kernel(slopes_a_ref, slopes_b_ref, sinks_ref, qa_ref, ka_ref, vat_ref, km_ref, qb_ref, kb_ref,
                      vbt_ref, oa_ref, ob_ref, kaug_sc, vaug_a_sc, causal_sc, sa_sc, bias_sc, vaug_b_sc, sb_sc):
    dyn0 = _dynamic_zero()
    moba = _moba_items(slopes_a_ref, qa_ref, ka_ref, vat_ref, km_ref, oa_ref, kaug_sc, vaug_a_sc, causal_sc,
                       sa_sc, dyn0)
    swa = _swa_items(slopes_b_ref, sinks_ref, qb_ref, kb_ref, vbt_ref, ob_ref, bias_sc, vaug_b_sc, sb_sc, dyn0)
    items = []
    for kind, (group, nslot) in enumerate(((moba, sa_sc.shape[0]), (swa, sb_sc.shape[0]))):
        total = sum(work for _, _, work in group)
        before = 0
        for k, (scores, finish, work) in enumerate(group):
            items.append(((before + work / 2) / total, kind, k % nslot, scores, finish))
            before += work
    items.sort(key=lambda it: it[:2])
    state = {}
    for u in range(len(items) + ATTN_LOOKAHEAD):
        if u < len(items):
            _, _, slot, scores, _ = items[u]
            state[u] = scores(slot)
        done = u - ATTN_LOOKAHEAD
        if done >= 0:
            _, _, slot, _, finish = items[done]
            finish(slot, state.pop(done))


def _attention(slopes_a, slopes_b, sinks, qa, ka, vat, km, qb, kb, vbt, batch, seq):
    nb = seq // MOBA_BLOCK
    nblk = seq // SWA_WINDOW
    grp = B_GROUP * HEAD_DIM
    span = SWA_WINDOW + SWA_WINDOW // 2
    smem = pl.BlockSpec(memory_space=pltpu.SMEM)
    rows = lambda width: pl.BlockSpec((seq, width), lambda b: (b, 0))
    slots = ATTN_LOOKAHEAD + 1
    return pl.pallas_call(
        _attention_kernel, grid=(batch,),
        in_specs=[smem, smem, smem, rows(W_A), rows(W_A),
                  pl.BlockSpec((nb, W_A, MOBA_BLOCK), lambda b: (b, 0, 0)),
                  pl.BlockSpec((1, nb, W_A), lambda b: (b, 0, 0)),
                  rows(W_QB), rows(W_KB_DUP),
                  pl.BlockSpec((nblk, W_KB, SWA_WINDOW), lambda b: (b, 0, 0))],
        out_specs=(rows(W_A), rows(W_QB)),
        out_shape=(jax.ShapeDtypeStruct(qa.shape, _BF16), jax.ShapeDtypeStruct(qb.shape, _BF16)),
        scratch_shapes=[pltpu.VMEM((A_HEADS, seq, 2 * HEAD_DIM), _BF16),
                        pltpu.VMEM((A_HEADS, nb, HEAD_DIM + ONES_ROWS, MOBA_BLOCK), _BF16),
                        pltpu.VMEM((MOBA_BLOCK, MOBA_BLOCK), _F32),
                        pltpu.VMEM((slots, nb, MOBA_BLOCK, MOBA_BLOCK), _F32),
                        pltpu.VMEM((B_KV_HEADS, span, grp), _F32),
                        pltpu.VMEM((B_KV_HEADS, nblk, HEAD_DIM + ONES_ROWS, 2 * SWA_WINDOW), _BF16),
                        pltpu.VMEM((slots, span, grp), _F32)],
        compiler_params=pltpu.CompilerParams(dimension_semantics=("parallel",),
                                             vmem_limit_bytes=ATTN_VMEM_LIMIT),
        name="attention",
    )(slopes_a, slopes_b, sinks, qa, ka, vat, km, qb, kb, vbt)


def _merge_mlp_kernel(x_ref, oa_ref, ob_ref, g_ref, wa_ref, wb_ref, wo_ref, gm_ref, wup_ref, wdn_ref, o_ref):
    a = jnp.dot(oa_ref[...], wa_ref[...], preferred_element_type=_F32)
    b = jnp.dot(ob_ref[...], wb_ref[...], preferred_element_type=_F32)
    ga = g_ref[:, :D_MODEL].astype(_F32)
    gb = g_ref[:, D_MODEL:].astype(_F32)
    mixed = jax.nn.sigmoid(ga) * a + jax.nn.sigmoid(gb) * b
    x1 = x_ref[...] + jnp.dot(mixed.astype(_BF16), wo_ref[...], preferred_element_type=_F32)
    ms = jnp.mean(x1 * x1, axis=-1, keepdims=True)
    h2 = ((x1 * lax.rsqrt(ms + EPS)) * gm_ref[...]).astype(_BF16)
    acc = x1
    for c in range(D_FF // FF_CHUNK):
        u = jnp.dot(h2, wup_ref[:, c * FF_CHUNK:(c + 1) * FF_CHUNK], preferred_element_type=_F32)
        u = jnp.square(jnp.maximum(u, 0.0)).astype(_BF16)
        acc = acc + jnp.dot(u, wdn_ref[c * FF_CHUNK:(c + 1) * FF_CHUNK, :], preferred_element_type=_F32)
    o_ref[...] = acc


def _merge_mlp(x2, oa, ob, g, wa, wb, wo, gm, wup, wdn):
    n = x2.shape[0]
    tm = MERGE_TILE
    row = lambda w: pl.BlockSpec((tm, w), lambda i: (i, 0))
    return pl.pallas_call(
        _merge_mlp_kernel, grid=(n // tm,),
        in_specs=[row(D_MODEL), row(W_A), row(W_QB), row(W_GATES), _const_spec(wa.shape), _const_spec(wb.shape),
                  _const_spec(wo.shape), _const_spec(gm.shape), _const_spec(wup.shape), _const_spec(wdn.shape)],
        out_specs=row(D_MODEL),
        out_shape=jax.ShapeDtypeStruct(x2.shape, x2.dtype),
        compiler_params=pltpu.CompilerParams(dimension_semantics=("parallel",),
                                             vmem_limit_bytes=VMEM_LIMIT),
        name="merge_mlp",
    )(x2, oa, ob, g, wa, wb, wo, gm, wup, wdn)


def _alibi_slopes(n):
    return jnp.exp2(-(8.0 / n) * jnp.arange(1, n + 1, dtype=_F32))


def kernel(x, norm_attn, w_in, q_norm_a, k_norm_a, q_norm_b, k_norm_b, sinks_b, w_branch_a, w_branch_b, w_out,
           norm_mlp, w_up, w_down):
    batch, seq, d = x.shape
    assert d == D_MODEL and (batch * seq) % INPROJ_TILE == 0 and (batch * seq) % MERGE_TILE == 0
    assert seq % MOBA_BLOCK == 0 and INPROJ_TILE % MOBA_BLOCK == 0
    slopes = _alibi_slopes(N_ATTN_HEADS)
    slopes_b, slopes_a = slopes[:B_HEADS], slopes[B_HEADS:]
    x2 = x.reshape(batch * seq, d)
    for l in range(norm_attn.shape[0]):
        tile_gain = lambda g, reps: jnp.tile(g, reps)[None, :]
        (qa, ka, qb, kb, g, vat, vbt, km), (wa, wb, wo, wup, wdn) = _inproj(
            x2, norm_attn[l][None, :], w_in[l].astype(_BF16),
            tile_gain(q_norm_a[l], A_HEADS), tile_gain(k_norm_a[l], A_HEADS),
            tile_gain(q_norm_b[l], B_HEADS), tile_gain(k_norm_b[l], B_KV_HEADS),
            (w_branch_a[l], w_branch_b[l], w_out[l], w_up[l], w_down[l]))
        km = km.reshape(batch, seq // MOBA_BLOCK, W_A)
        oa, ob = _attention(slopes_a * LOG2E, slopes_b * LOG2E, sinks_b[l] * LOG2E, qa, ka, vat, km, qb, kb, vbt,
                            batch, seq)
        x2 = _merge_mlp(x2, oa, ob, g, wa, wb, wo, norm_mlp[l][None, :], wup, wdn)
    return x2.reshape(batch, seq, d)
```

```python
import jax
import jax.numpy as jnp
from jax import lax
from jax.experimental import pallas as pl
from jax.experimental.pallas import tpu as pltpu

D_MODEL = 1024
HEAD_DIM = 64
A_HEADS = 8
B_HEADS = 8
B_KV_HEADS = 2
B_GROUP = B_HEADS // B_KV_HEADS
N_ATTN_HEADS = A_HEADS + B_HEADS
MOBA_BLOCK = 256
MOBA_TOPK = 3
SWA_WINDOW = 128
D_FF = 4 * D_MODEL
EPS = 1e-6
NEG = -1e30
SCALE = HEAD_DIM ** -0.5
LOG2E = 1.4426950408889634

W_A = A_HEADS * HEAD_DIM
W_QB = B_HEADS * HEAD_DIM
W_KB = B_KV_HEADS * HEAD_DIM
W_KB_DUP = 2 * W_KB
W_GATES = 2 * D_MODEL

C_QA = 0
C_KA = C_QA + W_A
C_VA = C_KA + W_A
C_QB = C_VA + W_A
C_KB = C_QB + W_QB
C_VB = C_KB + W_KB
C_G = C_VB + W_KB
C_END = C_G + W_GATES

N_INPROJ_OUT = 8
INPROJ_TILE = 1024
MERGE_TILE = 512
FF_CHUNK = 1024
ONES_ROWS = 16
ATTN_LOOKAHEAD = 4
VMEM_LIMIT = 48 * 1024 * 1024
ATTN_VMEM_LIMIT = 56 * 1024 * 1024

_NT = (((1,), (1,)), ((), ()))
_BF16 = jnp.bfloat16
_F32 = jnp.float32


def _const_spec(shape):
    return pl.BlockSpec(shape, lambda *_: (0,) * len(shape), pipeline_mode=pl.Buffered(1))


def _dynamic_zero():
    return jnp.minimum(pl.program_id(0), 0)


def _inproj_kernel(x_ref, gn_ref, w_ref, gqa_ref, gka_ref, gqb_ref, gkb_ref, *refs):
    n_cast = (len(refs) - N_INPROJ_OUT) // 2
    qa_ref, ka_ref, qb_ref, kb_ref, g_ref, vat_ref, vbt_ref, km_ref = refs[n_cast:n_cast + N_INPROJ_OUT]
    for src, dst in zip(refs[:n_cast], refs[n_cast + N_INPROJ_OUT:]):
        dst[...] = src[...].astype(_BF16)

    sub = MOBA_BLOCK
    n_sub = x_ref.shape[0] // sub
    lane = lax.broadcasted_iota(jnp.int32, (sub, 2 * HEAD_DIM), 1)
    first = lane < HEAD_DIM

    def normed(r):
        x = x_ref[r * sub:(r + 1) * sub, :]
        ms = jnp.mean(x * x, axis=-1, keepdims=True)
        return ((x * lax.rsqrt(ms + EPS)) * gn_ref[...]).astype(_BF16)

    def head_norm(y, gain_ref):
        parts = []
        for c in range(0, y.shape[-1], 2 * HEAD_DIM):
            yc = y[:, c:c + 2 * HEAD_DIM]
            sq = yc * yc
            s0 = jnp.sum(jnp.where(first, sq, 0.0), axis=-1, keepdims=True)
            s1 = jnp.sum(jnp.where(first, 0.0, sq), axis=-1, keepdims=True)
            msq = jnp.where(first, s0, s1) * (1.0 / HEAD_DIM)
            parts.append(yc * lax.rsqrt(msq + EPS))
        return jnp.concatenate(parts, axis=1) * gain_ref[...]

    def project(r, h):
        rows = slice(r * sub, (r + 1) * sub)

        def proj(lo, hi):
            return jnp.dot(h, w_ref[:, lo:hi], preferred_element_type=_F32)

        qa_ref[rows, :] = (head_norm(proj(C_QA, C_KA), gqa_ref) * (SCALE * LOG2E)).astype(_BF16)
        kn = head_norm(proj(C_KA, C_VA), gka_ref)
        ka_ref[rows, :] = kn.astype(_BF16)
        km_ref[0, r:r + 1, :] = jnp.sum(kn, axis=0, keepdims=True) * (1.0 / MOBA_BLOCK)
        qb_ref[rows, :] = (head_norm(proj(C_QB, C_KB), gqb_ref) * (SCALE * LOG2E)).astype(_BF16)
        g_ref[rows, :] = proj(C_G, C_END).astype(_BF16)

        kv = proj(C_KB, C_G)
        kb = head_norm(kv[:, :W_KB], gkb_ref)
        swapped = pltpu.roll(kb, HEAD_DIM, 1)
        kb_ref[rows, :] = jnp.concatenate([jnp.where(first, kb, swapped), jnp.where(first, swapped, kb)],
                                          axis=1).astype(_BF16)

        vat_ref[r] = proj(C_VA, C_QB).T.astype(_BF16)
        vbt = kv[:, W_KB:].T
        per = sub // SWA_WINDOW
        for c in range(per):
            vbt_ref[r * per + c] = vbt[:, c * SWA_WINDOW:(c + 1) * SWA_WINDOW].astype(_BF16)

    hs = [normed(r) for r in range(n_sub)]
    for r in range(n_sub):
        project(r, hs[r])


def _inproj(x2, gn, w_in, gqa, gka, gqb, gkb, later_weights):
    n = x2.shape[0]
    tm = INPROJ_TILE
    steps = n // tm
    row = lambda w: pl.BlockSpec((tm, w), lambda i: (i, 0))
    slabs = [w.reshape(steps, w.shape[0] // steps, w.shape[1]) for w in later_weights]
    slab_specs = [pl.BlockSpec((1,) + s.shape[1:], lambda i: (i, 0, 0)) for s in slabs]
    out_shape = (
        jax.ShapeDtypeStruct((n, W_A), _BF16),
        jax.ShapeDtypeStruct((n, W_A), _BF16),
        jax.ShapeDtypeStruct((n, W_QB), _BF16),
        jax.ShapeDtypeStruct((n, W_KB_DUP), _BF16),
        jax.ShapeDtypeStruct((n, W_GATES), _BF16),
        jax.ShapeDtypeStruct((n // MOBA_BLOCK, W_A, MOBA_BLOCK), _BF16),
        jax.ShapeDtypeStruct((n // SWA_WINDOW, W_KB, SWA_WINDOW), _BF16),
        jax.ShapeDtypeStruct((n // tm, tm // MOBA_BLOCK, W_A), _F32),
    )
    out_specs = (
        row(W_A), row(W_A), row(W_QB), row(W_KB_DUP), row(W_GATES),
        pl.BlockSpec((tm // MOBA_BLOCK, W_A, MOBA_BLOCK), lambda i: (i, 0, 0)),
        pl.BlockSpec((tm // SWA_WINDOW, W_KB, SWA_WINDOW), lambda i: (i, 0, 0)),
        pl.BlockSpec((1, tm // MOBA_BLOCK, W_A), lambda i: (i, 0, 0)),
    )
    in_specs = [row(D_MODEL), _const_spec(gn.shape), _const_spec(w_in.shape),
                _const_spec(gqa.shape), _const_spec(gka.shape),
                _const_spec(gqb.shape), _const_spec(gkb.shape)]
    outs = pl.pallas_call(
        _inproj_kernel, grid=(steps,), in_specs=in_specs + slab_specs,
        out_specs=out_specs + tuple(slab_specs),
        out_shape=out_shape + tuple(jax.ShapeDtypeStruct(s.shape, _BF16) for s in slabs),
        compiler_params=pltpu.CompilerParams(dimension_semantics=("parallel",),
                                             vmem_limit_bytes=VMEM_LIMIT, shape_invariant_numerics=False),
        name="inproj",
    )(x2, gn, w_in, gqa, gka, gqb, gkb, *slabs)
    assert len(out_shape) == N_INPROJ_OUT
    return outs[:N_INPROJ_OUT], [o.reshape(w.shape) for o, w in zip(outs[N_INPROJ_OUT:], later_weights)]


def _moba_items(slopes_ref, q_ref, k_ref, vt_ref, km_ref, o_ref, kaug_sc, vaug_sc, causal_sc, s_sc, dyn0):
    blk = MOBA_BLOCK
    nb = q_ref.shape[0] // blk
    pair = 2 * HEAD_DIM
    npair = q_ref.shape[1] // pair
    kp = lax.broadcasted_iota(jnp.int32, (blk, blk), 0)
    qp = lax.broadcasted_iota(jnp.int32, (blk, blk), 1)
    causal_sc[...] = jnp.where(kp <= qp, 0.0, NEG)
    lane = lax.broadcasted_iota(jnp.int32, (blk, pair), 1)
    prow = lax.broadcasted_iota(jnp.int32, (blk, pair), 0).astype(_F32)
    ridx = lax.broadcasted_iota(jnp.int32, (nb, blk), 0)
    km_lane = lax.broadcasted_iota(jnp.int32, (nb, pair), 1)
    in_head = [(lane >= e * HEAD_DIM) & (lane < (e + 1) * HEAD_DIM) for e in range(2)]

    def prepare(lp):
        lanes = slice(lp * pair, (lp + 1) * pair)
        heads = []
        for e in range(2):
            h = 2 * lp + e
            slope = slopes_ref[h]
            a = (1 - e) * HEAD_DIM
            sv = jnp.full((blk, pair), slope, _F32)
            hi = sv.astype(_BF16).astype(_F32)
            mid = (sv - hi).astype(_BF16).astype(_F32)
            lo = sv - hi - mid
            pieces = jnp.where(lane == a, hi, jnp.where(lane == a + 1, mid, jnp.where(lane == a + 2, lo, 0.0)))
            k_aug = jnp.where((lane >= a) & (lane < a + 3), prow, 0.0).astype(_BF16)
            for n in range(nb):
                rows = slice(n * blk, (n + 1) * blk)
                kaug_sc[h, rows, :] = jnp.where(in_head[e], k_ref[rows, lanes], k_aug)
                vaug_sc[h, n, :HEAD_DIM, :] = vt_ref[n, h * HEAD_DIM:(h + 1) * HEAD_DIM, :]
                vaug_sc[h, n, HEAD_DIM:, :] = jnp.ones((vaug_sc.shape[2] - HEAD_DIM, blk), _BF16)
            km_e = (km_lane >= e * HEAD_DIM) & (km_lane < (e + 1) * HEAD_DIM)
            km_head = jnp.where(km_e, km_ref[0, :, lanes], 0.0).astype(_BF16)
            heads.append((slope, pieces.astype(_BF16), km_head))
        return heads

    prepared = {}

    def scores(lp, i, e, slot):
        if lp not in prepared:
            prepared[lp] = prepare(lp)
        slope, q_aug, km_head = prepared[lp][e]
        h = 2 * lp + e
        qm = jnp.where(in_head[e], q_ref[i * blk:(i + 1) * blk, lp * pair:(lp + 1) * pair], q_aug)
        gs = lax.dot_general(km_head, qm, _NT, preferred_element_type=_F32)
        radj = []
        for n in range(i):
            row = gs[n:n + 1, :]
            ahead = ((gs > row) | ((gs == row) & (ridx < n))) & (ridx < i)
            rank = jnp.sum(ahead.astype(_F32), axis=0, keepdims=True)
            radj.append(jnp.where(rank < MOBA_TOPK, 0.0, NEG) - slope * float(blk * (i - n)))
        m = None
        for n in range(i + 1):
            t = lax.dot_general(kaug_sc[h, n * blk:(n + 1) * blk, :], qm, _NT, preferred_element_type=_F32)
            if n == i:
                t = t + causal_sc[...]
            s_sc[slot, n + dyn0] = t
            bm = jnp.max(t, axis=0, keepdims=True)
            if n < i:
                bm = bm + radj[n]
            m = bm if m is None else jnp.maximum(m, bm)
        return [m - radj[n] if n < i else m for n in range(i + 1)]

    def weighted_values(lp, i, e, slot, shifts):
        acc = None
        for n in range(i + 1):
            p = jnp.exp2(s_sc[slot, n + dyn0] - shifts[n]).astype(_BF16)
            pv = jnp.dot(vaug_sc[2 * lp + e, n], p, preferred_element_type=_F32)
            acc = pv if acc is None else acc + pv
        return acc[:HEAD_DIM] / acc[HEAD_DIM:HEAD_DIM + 1]

    outs = {}

    def item(lp, i, e):
        def finish(slot, shifts):
            outs[e] = weighted_values(lp, i, e, slot, shifts)
            if e == 1:
                o = jnp.concatenate([outs.pop(0), outs.pop(1)], axis=0)
                o_ref[i * blk:(i + 1) * blk, lp * pair:(lp + 1) * pair] = o.astype(o_ref.dtype).T

        return (lambda slot: scores(lp, i, e, slot)), finish, i + 1

    return [item(lp, i, e) for lp in range(npair) for i in reversed(range(nb)) for e in range(2)]


def _swa_items(slopes_ref, sinks_ref, q_ref, k_ref, vt_ref, o_ref, bias_sc, vaug_sc, s_sc, dyn0):
    w = SWA_WINDOW
    sq = w // 2
    span = w + sq
    ncol = B_GROUP * sq
    ntile = q_ref.shape[0] // sq
    nblk = q_ref.shape[0] // w

    def per_head(hk, col, ref):
        out = ref[hk * B_GROUP]
        for h in range(1, B_GROUP):
            out = jnp.where(col >= h * sq, ref[hk * B_GROUP + h], out)
        return out

    kp = lax.broadcasted_iota(jnp.int32, (span, ncol), 0)
    col = lax.broadcasted_iota(jnp.int32, (span, ncol), 1)
    dist = (col & (sq - 1)) + w - kp
    lane = lax.broadcasted_iota(jnp.int32, (sq, 2 * HEAD_DIM), 1)
    first = lane < HEAD_DIM
    ones = jnp.ones((vaug_sc.shape[2] - HEAD_DIM, 2 * w), _BF16)

    def prepare(hk):
        slope = per_head(hk, col, slopes_ref)
        bias_sc[hk] = jnp.where((dist >= 0) & (dist < w), -slope * dist.astype(_F32), NEG)
        v_rows = slice(hk * HEAD_DIM, (hk + 1) * HEAD_DIM)
        for j in range(nblk):
            prev = vt_ref[j - 1, v_rows, :] if j > 0 else jnp.zeros((HEAD_DIM, w), _BF16)
            vaug_sc[hk, j, :HEAD_DIM, :] = jnp.concatenate([prev, vt_ref[j, v_rows, :]], axis=1)
            vaug_sc[hk, j, HEAD_DIM:, :] = ones
        return per_head(hk, lax.broadcasted_iota(jnp.int32, (1, ncol), 1), sinks_ref)

    sinks = {}

    def key_range(t):
        k0 = max(t - 2, 0) * sq
        return k0, (t + 1) * sq - k0

    def scores(hk, t, slot):
        if hk not in sinks:
            sinks[hk] = prepare(hk)
        q_t = q_ref[t * sq:(t + 1) * sq, hk * ncol:(hk + 1) * ncol]
        zero = jnp.zeros((sq, 2 * HEAD_DIM), q_t.dtype)
        stacked = []
        for pr in range(B_GROUP // 2):
            pair = q_t[:, pr * 2 * HEAD_DIM:(pr + 1) * 2 * HEAD_DIM]
            stacked += [jnp.where(first, pair, zero), jnp.where(first, zero, pair)]
        qs = jnp.concatenate(stacked, axis=0)
        k0, nk = key_range(t)
        keys = k_ref[k0:k0 + nk, hk * 2 * HEAD_DIM:(hk + 1) * 2 * HEAD_DIM]
        s = lax.dot_general(keys, qs, _NT, preferred_element_type=_F32)
        s = s + bias_sc[hk, span - nk:, :]
        s_sc[slot + dyn0, :nk, :] = s
        return jnp.maximum(jnp.max(s, axis=0, keepdims=True), sinks[hk])

    def weighted_values(hk, t, slot, m):
        k0, nk = key_range(t)
        pb = jnp.exp2(s_sc[slot + dyn0, :nk, :] - m).astype(_BF16)
        before = k0 - (t // 2 - 1) * w
        after = 2 * w - before - nk
        pad = lambda rows: [jnp.zeros((rows, ncol), _BF16)] if rows else []
        p_full = jnp.concatenate(pad(before) + [pb] + pad(after), axis=0)
        ot = jnp.dot(vaug_sc[hk, t // 2], p_full, preferred_element_type=_F32)
        den = ot[HEAD_DIM:HEAD_DIM + 1] + jnp.exp2(sinks[hk] - m)
        return ot[:HEAD_DIM] / den

    def store(hk, j, even, odd):
        even, odd = even.astype(o_ref.dtype), odd.astype(o_ref.dtype)
        for pr in range(B_GROUP // 2):
            lanes = slice(pr * 2 * HEAD_DIM, (pr + 1) * 2 * HEAD_DIM)
            e, o = even[:, lanes], odd[:, lanes]
            xa = jnp.where(first, e, pltpu.roll(o, sq, 1))
            xb = jnp.where(first, pltpu.roll(e, sq, 1), o)
            x = jnp.concatenate([xa, xb], axis=0)
            lane0 = hk * ncol + pr * 2 * HEAD_DIM
            o_ref[j * w:(j + 1) * w, lane0:lane0 + 2 * HEAD_DIM] = x.T

    pending = {}

    def item(hk, t):
        def finish(slot, m):
            out = weighted_values(hk, t, slot, m)
            if t % 2 == 0:
                pending[hk] = out
            else:
                store(hk, t // 2, pending.pop(hk), out)

        return (lambda slot: scores(hk, t, slot)), finish, 1

    return [item(hk, t) for hk in range(q_ref.shape[1] // ncol) for t in range(ntile)]


def _attention_kernel(slopes_a_ref, slopes_b_ref, sinks_ref, qa_ref, ka_ref, vat_ref, km_ref, qb_ref, kb_ref,
                      vbt_ref, oa_ref, ob_ref, kaug_sc, vaug_a_sc, causal_sc, sa_sc, bias_sc, vaug_b_sc, sb_sc):
    dyn0 = _dynamic_zero()
    moba = _moba_items(slopes_a_ref, qa_ref, ka_ref, vat_ref, km_ref, oa_ref, kaug_sc, vaug_a_sc, causal_sc,
                       sa_sc, dyn0)
    swa = _swa_items(slopes_b_ref, sinks_ref, qb_ref, kb_ref, vbt_ref, ob_ref, bias_sc, vaug_b_sc, sb_sc, dyn0)
    items = []
    for kind, (group, nslot) in enumerate(((moba, sa_sc.shape[0]), (swa, sb_sc.shape[0]))):
        total = sum(work for _, _, work in group)
        before = 0
        for k, (scores, finish, work) in enumerate(group):
            items.append(((before + work / 2) / total, kind, k % nslot, scores, finish))
            before += work
    items.sort(key=lambda it: it[:2])
    state = {}
    for u in range(len(items) + ATTN_LOOKAHEAD):
        if u < len(items):
            _, _, slot, scores, _ = items[u]
            state[u] = scores(slot)
        done = u - ATTN_LOOKAHEAD
        if done >= 0:
            _, _, slot, _, finish = items[done]
            finish(slot, state.pop(done))


def _attention(slopes_a, slopes_b, sinks, qa, ka, vat, km, qb, kb, vbt, batch, seq):
    nb = seq // MOBA_BLOCK
    nblk = seq // SWA_WINDOW
    grp = B_GROUP * HEAD_DIM
    span = SWA_WINDOW + SWA_WINDOW // 2
    smem = pl.BlockSpec(memory_space=pltpu.SMEM)
    rows = lambda width: pl.BlockSpec((seq, width), lambda b: (b, 0))
    slots = ATTN_LOOKAHEAD + 1
    return pl.pallas_call(
        _attention_kernel, grid=(batch,),
        in_specs=[smem, smem, smem, rows(W_A), rows(W_A),
                  pl.BlockSpec((nb, W_A, MOBA_BLOCK), lambda b: (b, 0, 0)),
                  pl.BlockSpec((1, nb, W_A), lambda b: (b, 0, 0)),
                  rows(W_QB), rows(W_KB_DUP),
                  pl.BlockSpec((nblk, W_KB, SWA_WINDOW), lambda b: (b, 0, 0))],
        out_specs=(rows(W_A), rows(W_QB)),
        out_shape=(jax.ShapeDtypeStruct(qa.shape, _BF16), jax.ShapeDtypeStruct(qb.shape, _BF16)),
        scratch_shapes=[pltpu.VMEM((A_HEADS, seq, 2 * HEAD_DIM), _BF16),
                        pltpu.VMEM((A_HEADS, nb, HEAD_DIM + ONES_ROWS, MOBA_BLOCK), _BF16),
                        pltpu.VMEM((MOBA_BLOCK, MOBA_BLOCK), _F32),
                        pltpu.VMEM((slots, nb, MOBA_BLOCK, MOBA_BLOCK), _F32),
                        pltpu.VMEM((B_KV_HEADS, span, grp), _F32),
                        pltpu.VMEM((B_KV_HEADS, nblk, HEAD_DIM + ONES_ROWS, 2 * SWA_WINDOW), _BF16),
                        pltpu.VMEM((slots, span, grp), _F32)],
        compiler_params=pltpu.CompilerParams(dimension_semantics=("parallel",),
                                             vmem_limit_bytes=ATTN_VMEM_LIMIT),
        name="attention",
    )(slopes_a, slopes_b, sinks, qa, ka, vat, km, qb, kb, vbt)


def _merge_mlp_kernel(x_ref, oa_ref, ob_ref, g_ref, wa_ref, wb_ref, wo_ref, gm_ref, wup_ref, wdn_ref, o_ref):
    a = jnp.dot(oa_ref[...], wa_ref[...], preferred_element_type=_F32)
    b = jnp.dot(ob_ref[...], wb_ref[...], preferred_element_type=_F32)
    ga = g_ref[:, :D_MODEL].astype(_F32)
    gb = g_ref[:, D_MODEL:].astype(_F32)
    mixed = jax.nn.sigmoid(ga) * a + jax.nn.sigmoid(gb) * b
    x1 = x_ref[...] + jnp.dot(mixed.astype(_BF16), wo_ref[...], preferred_element_type=_F32)
    ms = jnp.mean(x1 * x1, axis=-1, keepdims=True)
    h2 = ((x1 * lax.rsqrt(ms + EPS)) * gm_ref[...]).astype(_BF16)
    acc = x1
    for c in range(D_FF // FF_CHUNK):
        u = jnp.dot(h2, wup_ref[:, c * FF_CHUNK:(c + 1) * FF_CHUNK], preferred_element_type=_F32)
        u = jnp.square(jnp.maximum(u, 0.0)).astype(_BF16)
        acc = acc + jnp.dot(u, wdn_ref[c * FF_CHUNK:(c + 1) * FF_CHUNK, :], preferred_element_type=_F32)
    o_ref[...] = acc


def _merge_mlp(x2, oa, ob, g, wa, wb, wo, gm, wup, wdn):
    n = x2.shape[0]
    tm = MERGE_TILE
    row = lambda w: pl.BlockSpec((tm, w), lambda i: (i, 0))
    return pl.pallas_call(
        _merge_mlp_kernel, grid=(n // tm,),
        in_specs=[row(D_MODEL), row(W_A), row(W_QB), row(W_GATES), _const_spec(wa.shape), _const_spec(wb.shape),
                  _const_spec(wo.shape), _const_spec(gm.shape), _const_spec(wup.shape), _const_spec(wdn.shape)],
        out_specs=row(D_MODEL),
        out_shape=jax.ShapeDtypeStruct(x2.shape, x2.dtype),
        compiler_params=pltpu.CompilerParams(dimension_semantics=("parallel",),
                                             vmem_limit_bytes=VMEM_LIMIT, shape_invariant_numerics=False),
        name="merge_mlp",
    )(x2, oa, ob, g, wa, wb, wo, gm, wup, wdn)


def _alibi_slopes(n):
    return jnp.exp2(-(8.0 / n) * jnp.arange(1, n + 1, dtype=_F32))


def kernel(x, norm_attn, w_in, q_norm_a, k_norm_a, q_norm_b, k_norm_b, sinks_b, w_branch_a, w_branch_b, w_out,
           norm_mlp, w_up, w_down):
    batch, seq, d = x.shape
    assert d == D_MODEL and (batch * seq) % INPROJ_TILE == 0 and (batch * seq) % MERGE_TILE == 0
    assert seq % MOBA_BLOCK == 0 and INPROJ_TILE % MOBA_BLOCK == 0
    slopes = _alibi_slopes(N_ATTN_HEADS)
    slopes_b, slopes_a = slopes[:B_HEADS], slopes[B_HEADS:]
    x2 = x.reshape(batch * seq, d)
    for l in range(norm_attn.shape[0]):
        tile_gain = lambda g, reps: jnp.tile(g, reps)[None, :]
        (qa, ka, qb, kb, g, vat, vbt, km), (wa, wb, wo, wup, wdn) = _inproj(
            x2, norm_attn[l][None, :], w_in[l].astype(_BF16),
            tile_gain(q_norm_a[l], A_HEADS), tile_gain(k_norm_a[l], A_HEADS),
            tile_gain(q_norm_b[l], B_HEADS), tile_gain(k_norm_b[l], B_KV_HEADS),
            (w_branch_a[l], w_branch_b[l], w_out[l], w_up[l], w_down[l]))
        km = km.reshape(batch, seq // MOBA_BLOCK, W_A)
        oa, ob = _attention(slopes_a * LOG2E, slopes_b * LOG2E, sinks_b[l] * LOG2E, qa, ka, vat, km, qb, kb, vbt,
                            batch, seq)
        x2 = _merge_mlp(x2, oa, ob, g, wa, wb, wo, norm_mlp[l][None, :], wup, wdn)
    return x2.reshape(batch, seq, d)
```

```python
import jax
import jax.numpy as jnp
from jax import lax
from jax.experimental import pallas as pl
from jax.experimental.pallas import tpu as pltpu

D_MODEL = 1024
HEAD_DIM = 64
A_HEADS = 8
B_HEADS = 8
B_KV_HEADS = 2
B_GROUP = B_HEADS // B_KV_HEADS
N_ATTN_HEADS = A_HEADS + B_HEADS
MOBA_BLOCK = 256
MOBA_TOPK = 3
SWA_WINDOW = 128
D_FF = 4 * D_MODEL
EPS = 1e-6
NEG = -1e30
SCALE = HEAD_DIM ** -0.5
LOG2E = 1.4426950408889634

W_A = A_HEADS * HEAD_DIM
W_QB = B_HEADS * HEAD_DIM
W_KB = B_KV_HEADS * HEAD_DIM
W_KB_DUP = 2 * W_KB
W_GATES = 2 * D_MODEL

C_QA = 0
C_KA = C_QA + W_A
C_VA = C_KA + W_A
C_QB = C_VA + W_A
C_KB = C_QB + W_QB
C_VB = C_KB + W_KB
C_G = C_VB + W_KB
C_END = C_G + W_GATES

N_INPROJ_OUT = 8
INPROJ_TILE = 1024
MERGE_TILE = 512
FF_CHUNK = 1024
ONES_ROWS = 16
ATTN_LOOKAHEAD = 4
VMEM_LIMIT = 48 * 1024 * 1024
ATTN_VMEM_LIMIT = 56 * 1024 * 1024

_NT = (((1,), (1,)), ((), ()))
_BF16 = jnp.bfloat16
_F32 = jnp.float32


def _const_spec(shape):
    return pl.BlockSpec(shape, lambda *_: (0,) * len(shape), pipeline_mode=pl.Buffered(1))


def _dynamic_zero():
    return jnp.minimum(pl.program_id(0), 0)


def _inproj_kernel(x_ref, gn_ref, w_ref, gqa_ref, gka_ref, gqb_ref, gkb_ref, *refs):
    n_cast = (len(refs) - N_INPROJ_OUT) // 2
    qa_ref, ka_ref, qb_ref, kb_ref, g_ref, vat_ref, vbt_ref, km_ref = refs[n_cast:n_cast + N_INPROJ_OUT]
    for src, dst in zip(refs[:n_cast], refs[n_cast + N_INPROJ_OUT:]):
        dst[...] = src[...].astype(_BF16)

    sub = MOBA_BLOCK
    n_sub = x_ref.shape[0] // sub
    lane = lax.broadcasted_iota(jnp.int32, (sub, 2 * HEAD_DIM), 1)
    first = lane < HEAD_DIM

    def normed(r):
        x = x_ref[r * sub:(r + 1) * sub, :]
        ms = jnp.mean(x * x, axis=-1, keepdims=True)
        return ((x * lax.rsqrt(ms + EPS)) * gn_ref[...]).astype(_BF16)

    def head_norm(y, gain_ref):
        parts = []
        for c in range(0, y.shape[-1], 2 * HEAD_DIM):
            yc = y[:, c:c + 2 * HEAD_DIM]
            sq = yc * yc
            s0 = jnp.sum(jnp.where(first, sq, 0.0), axis=-1, keepdims=True)
            s1 = jnp.sum(jnp.where(first, 0.0, sq), axis=-1, keepdims=True)
            msq = jnp.where(first, s0, s1) * (1.0 / HEAD_DIM)
            parts.append(yc * lax.rsqrt(msq + EPS))
        return jnp.concatenate(parts, axis=1) * gain_ref[...]

    def project(r, h):
        rows = slice(r * sub, (r + 1) * sub)

        def proj(lo, hi):
            return jnp.dot(h, w_ref[:, lo:hi], preferred_element_type=_F32)

        qa_ref[rows, :] = (head_norm(proj(C_QA, C_KA), gqa_ref) * (SCALE * LOG2E)).astype(_BF16)
        kn = head_norm(proj(C_KA, C_VA), gka_ref)
        ka_ref[rows, :] = kn.astype(_BF16)
        km_ref[0, r:r + 1, :] = jnp.sum(kn, axis=0, keepdims=True) * (1.0 / MOBA_BLOCK)
        qb_ref[rows, :] = (head_norm(proj(C_QB, C_KB), gqb_ref) * (SCALE * LOG2E)).astype(_BF16)
        g_ref[rows, :] = proj(C_G, C_END).astype(_BF16)

        kv = proj(C_KB, C_G)
        kb = head_norm(kv[:, :W_KB], gkb_ref)
        swapped = pltpu.roll(kb, HEAD_DIM, 1)
        kb_ref[rows, :] = jnp.concatenate([jnp.where(first, kb, swapped), jnp.where(first, swapped, kb)],
                                          axis=1).astype(_BF16)

        vat_ref[r] = proj(C_VA, C_QB).T.astype(_BF16)
        vbt = kv[:, W_KB:].T
        per = sub // SWA_WINDOW
        for c in range(per):
            vbt_ref[r * per + c] = vbt[:, c * SWA_WINDOW:(c + 1) * SWA_WINDOW].astype(_BF16)

    hs = [normed(r) for r in range(n_sub)]
    for r in range(n_sub):
        project(r, hs[r])


def _inproj(x2, gn, w_in, gqa, gka, gqb, gkb, later_weights):
    n = x2.shape[0]
    tm = INPROJ_TILE
    steps = n // tm
    row = lambda w: pl.BlockSpec((tm, w), lambda i: (i, 0))
    slabs = [w.reshape(steps, w.shape[0] // steps, w.shape[1]) for w in later_weights]
    slab_specs = [pl.BlockSpec((1,) + s.shape[1:], lambda i: (i, 0, 0)) for s in slabs]
    out_shape = (
        jax.ShapeDtypeStruct((n, W_A), _BF16),
        jax.ShapeDtypeStruct((n, W_A), _BF16),
        jax.ShapeDtypeStruct((n, W_QB), _BF16),
        jax.ShapeDtypeStruct((n, W_KB_DUP), _BF16),
        jax.ShapeDtypeStruct((n, W_GATES), _BF16),
        jax.ShapeDtypeStruct((n // MOBA_BLOCK, W_A, MOBA_BLOCK), _BF16),
        jax.ShapeDtypeStruct((n // SWA_WINDOW, W_KB, SWA_WINDOW), _BF16),
        jax.ShapeDtypeStruct((n // tm, tm // MOBA_BLOCK, W_A), _F32),
    )
    out_specs = (
        row(W_A), row(W_A), row(W_QB), row(W_KB_DUP), row(W_GATES),
        pl.BlockSpec((tm // MOBA_BLOCK, W_A, MOBA_BLOCK), lambda i: (i, 0, 0)),
        pl.BlockSpec((tm // SWA_WINDOW, W_KB, SWA_WINDOW), lambda i: (i, 0, 0)),
        pl.BlockSpec((1, tm // MOBA_BLOCK, W_A), lambda i: (i, 0, 0)),
    )
    in_specs = [row(D_MODEL), _const_spec(gn.shape), _const_spec(w_in.shape),
                _const_spec(gqa.shape), _const_spec(gka.shape),
                _const_spec(gqb.shape), _const_spec(gkb.shape)]
    outs = pl.pallas_call(
        _inproj_kernel, grid=(steps,), in_specs=in_specs + slab_specs,
        out_specs=out_specs + tuple(slab_specs),
        out_shape=out_shape + tuple(jax.ShapeDtypeStruct(s.shape, _BF16) for s in slabs),
        compiler_params=pltpu.CompilerParams(dimension_semantics=("parallel",),
                                             vmem_limit_bytes=VMEM_LIMIT),
        name="inproj",
    )(x2, gn, w_in, gqa, gka, gqb, gkb, *slabs)
    assert len(out_shape) == N_INPROJ_OUT
    return outs[:N_INPROJ_OUT], [o.reshape(w.shape) for o, w in zip(outs[N_INPROJ_OUT:], later_weights)]


def _moba_items(slopes_ref, q_ref, k_ref, vt_ref, km_ref, o_ref, kaug_sc, vaug_sc, causal_sc, s_sc, dyn0):
    blk = MOBA_BLOCK
    nb = q_ref.shape[0] // blk
    pair = 2 * HEAD_DIM
    npair = q_ref.shape[1] // pair
    kp = lax.broadcasted_iota(jnp.int32, (blk, blk), 0)
    qp = lax.broadcasted_iota(jnp.int32, (blk, blk), 1)
    causal_sc[...] = jnp.where(kp <= qp, 0.0, NEG)
    lane = lax.broadcasted_iota(jnp.int32, (blk, pair), 1)
    prow = lax.broadcasted_iota(jnp.int32, (blk, pair), 0).astype(_F32)
    ridx = lax.broadcasted_iota(jnp.int32, (nb, blk), 0)
    km_lane = lax.broadcasted_iota(jnp.int32, (nb, pair), 1)
    in_head = [(lane >= e * HEAD_DIM) & (lane < (e + 1) * HEAD_DIM) for e in range(2)]

    def prepare(lp, e):
        lanes = slice(lp * pair, (lp + 1) * pair)
        heads = []
        for _ in range(1):
            h = 2 * lp + e
            slope = slopes_ref[h]
            a = (1 - e) * HEAD_DIM
            sv = jnp.full((blk, pair), slope, _F32)
            hi = sv.astype(_BF16).astype(_F32)
            mid = (sv - hi).astype(_BF16).astype(_F32)
            lo = sv - hi - mid
            pieces = jnp.where(lane == a, hi, jnp.where(lane == a + 1, mid, jnp.where(lane == a + 2, lo, 0.0)))
            k_aug = jnp.where((lane >= a) & (lane < a + 3), prow, 0.0).astype(_BF16)
            for n in range(nb):
                rows = slice(n * blk, (n + 1) * blk)
                kaug_sc[h, rows, :] = jnp.where(in_head[e], k_ref[rows, lanes], k_aug)
                vaug_sc[h, n, :HEAD_DIM, :] = vt_ref[n, h * HEAD_DIM:(h + 1) * HEAD_DIM, :]
                vaug_sc[h, n, HEAD_DIM:, :] = jnp.ones((vaug_sc.shape[2] - HEAD_DIM, blk), _BF16)
            km_e = (km_lane >= e * HEAD_DIM) & (km_lane < (e + 1) * HEAD_DIM)
            km_head = jnp.where(km_e, km_ref[0, :, lanes], 0.0).astype(_BF16)
            heads.append((slope, pieces.astype(_BF16), km_head))
        return heads

    prepared = {}

    def scores(lp, i, e, slot):
        if (lp, e) not in prepared:
            prepared[lp, e] = prepare(lp, e)
        slope, q_aug, km_head = prepared[lp, e][0]
        h = 2 * lp + e
        qm = jnp.where(in_head[e], q_ref[i * blk:(i + 1) * blk, lp * pair:(lp + 1) * pair], q_aug)
        gs = lax.dot_general(km_head, qm, _NT, preferred_element_type=_F32)
        radj = []
        for n in range(i):
            row = gs[n:n + 1, :]
            ahead = ((gs > row) | ((gs == row) & (ridx < n))) & (ridx < i)
            rank = jnp.sum(ahead.astype(_F32), axis=0, keepdims=True)
            radj.append(jnp.where(rank < MOBA_TOPK, 0.0, NEG) - slope * float(blk * (i - n)))
        m = None
        for n in range(i + 1):
            t = lax.dot_general(kaug_sc[h, n * blk:(n + 1) * blk, :], qm, _NT, preferred_element_type=_F32)
            if n == i:
                t = t + causal_sc[...]
            s_sc[slot, n + dyn0] = t
            bm = jnp.max(t, axis=0, keepdims=True)
            if n < i:
                bm = bm + radj[n]
            m = bm if m is None else jnp.maximum(m, bm)
        return [m - radj[n] if n < i else m for n in range(i + 1)]

    def weighted_values(lp, i, e, slot, shifts):
        acc = None
        for n in range(i + 1):
            p = jnp.exp2(s_sc[slot, n + dyn0] - shifts[n]).astype(_BF16)
            pv = jnp.dot(vaug_sc[2 * lp + e, n], p, preferred_element_type=_F32)
            acc = pv if acc is None else acc + pv
        return acc[:HEAD_DIM] / acc[HEAD_DIM:HEAD_DIM + 1]

    outs = {}

    def item(lp, i, e):
        def finish(slot, shifts):
            outs[e] = weighted_values(lp, i, e, slot, shifts)
            if e == 1:
                o = jnp.concatenate([outs.pop(0), outs.pop(1)], axis=0)
                o_ref[i * blk:(i + 1) * blk, lp * pair:(lp + 1) * pair] = o.astype(o_ref.dtype).T

        return (lambda slot: scores(lp, i, e, slot)), finish, i + 1

    return [item(lp, i, e) for lp in range(npair) for i in reversed(range(nb)) for e in range(2)]


def _swa_items(slopes_ref, sinks_ref, q_ref, k_ref, vt_ref, o_ref, bias_sc, vaug_sc, s_sc, dyn0):
    w = SWA_WINDOW
    sq = w // 2
    span = w + sq
    ncol = B_GROUP * sq
    ntile = q_ref.shape[0] // sq
    nblk = q_ref.shape[0] // w

    def per_head(hk, col, ref):
        out = ref[hk * B_GROUP]
        for h in range(1, B_GROUP):
            out = jnp.where(col >= h * sq, ref[hk * B_GROUP + h], out)
        return out

    kp = lax.broadcasted_iota(jnp.int32, (span, ncol), 0)
    col = lax.broadcasted_iota(jnp.int32, (span, ncol), 1)
    dist = (col & (sq - 1)) + w - kp
    lane = lax.broadcasted_iota(jnp.int32, (sq, 2 * HEAD_DIM), 1)
    first = lane < HEAD_DIM
    ones = jnp.ones((vaug_sc.shape[2] - HEAD_DIM, 2 * w), _BF16)

    def prepare(hk):
        slope = per_head(hk, col, slopes_ref)
        bias_sc[hk] = jnp.where((dist >= 0) & (dist < w), -slope * dist.astype(_F32), NEG)
        v_rows = slice(hk * HEAD_DIM, (hk + 1) * HEAD_DIM)
        for j in range(nblk):
            prev = vt_ref[j - 1, v_rows, :] if j > 0 else jnp.zeros((HEAD_DIM, w), _BF16)
            vaug_sc[hk, j, :HEAD_DIM, :] = jnp.concatenate([prev, vt_ref[j, v_rows, :]], axis=1)
            vaug_sc[hk, j, HEAD_DIM:, :] = ones
        return per_head(hk, lax.broadcasted_iota(jnp.int32, (1, ncol), 1), sinks_ref)

    sinks = {}

    def key_range(t):
        k0 = max(t - 2, 0) * sq
        return k0, (t + 1) * sq - k0

    def scores(hk, t, slot):
        if hk not in sinks:
            sinks[hk] = prepare(hk)
        q_t = q_ref[t * sq:(t + 1) * sq, hk * ncol:(hk + 1) * ncol]
        zero = jnp.zeros((sq, 2 * HEAD_DIM), q_t.dtype)
        stacked = []
        for pr in range(B_GROUP // 2):
            pair = q_t[:, pr * 2 * HEAD_DIM:(pr + 1) * 2 * HEAD_DIM]
            stacked += [jnp.where(first, pair, zero), jnp.where(first, zero, pair)]
        qs = jnp.concatenate(stacked, axis=0)
        k0, nk = key_range(t)
        keys = k_ref[k0:k0 + nk, hk * 2 * HEAD_DIM:(hk + 1) * 2 * HEAD_DIM]
        s = lax.dot_general(keys, qs, _NT, preferred_element_type=_F32)
        s = s + bias_sc[hk, span - nk:, :]
        s_sc[slot + dyn0, :nk, :] = s
        return jnp.maximum(jnp.max(s, axis=0, keepdims=True), sinks[hk])

    def weighted_values(hk, t, slot, m):
        k0, nk = key_range(t)
        pb = jnp.exp2(s_sc[slot + dyn0, :nk, :] - m).astype(_BF16)
        before = k0 - (t // 2 - 1) * w
        after = 2 * w - before - nk
        pad = lambda rows: [jnp.zeros((rows, ncol), _BF16)] if rows else []
        p_full = jnp.concatenate(pad(before) + [pb] + pad(after), axis=0)
        ot = jnp.dot(vaug_sc[hk, t // 2], p_full, preferred_element_type=_F32)
        den = ot[HEAD_DIM:HEAD_DIM + 1] + jnp.exp2(sinks[hk] - m)
        return ot[:HEAD_DIM] / den

    def store(hk, j, even, odd):
        even, odd = even.astype(o_ref.dtype), odd.astype(o_ref.dtype)
        for pr in range(B_GROUP // 2):
            lanes = slice(pr * 2 * HEAD_DIM, (pr + 1) * 2 * HEAD_DIM)
            e, o = even[:, lanes], odd[:, lanes]
            xa = jnp.where(first, e, pltpu.roll(o, sq, 1))
            xb = jnp.where(first, pltpu.roll(e, sq, 1), o)
            x = jnp.concatenate([xa, xb], axis=0)
            lane0 = hk * ncol + pr * 2 * HEAD_DIM
            o_ref[j * w:(j + 1) * w, lane0:lane0 + 2 * HEAD_DIM] = x.T

    pending = {}

    def item(hk, t):
        def finish(slot, m):
            out = weighted_values(hk, t, slot, m)
            if t % 2 == 0:
                pending[hk] = out
            else:
                store(hk, t // 2, pending.pop(hk), out)

        return (lambda slot: scores(hk, t, slot)), finish, 1

    return [item(hk, t) for hk in range(q_ref.shape[1] // ncol) for t in range(ntile)]


def _attention_kernel(slopes_a_ref, slopes_b_ref, sinks_ref, qa_ref, ka_ref, vat_ref, km_ref, qb_ref, kb_ref,
                      vbt_ref, oa_ref, ob_ref, kaug_sc, vaug_a_sc, causal_sc, sa_sc, bias_sc, vaug_b_sc, sb_sc):
    dyn0 = _dynamic_zero()
    moba = _moba_items(slopes_a_ref, qa_ref, ka_ref, vat_ref, km_ref, oa_ref, kaug_sc, vaug_a_sc, causal_sc,
                       sa_sc, dyn0)
    swa = _swa_items(slopes_b_ref, sinks_ref, qb_ref, kb_ref, vbt_ref, ob_ref, bias_sc, vaug_b_sc, sb_sc, dyn0)
    items = []
    for kind, (group, nslot) in enumerate(((moba, sa_sc.shape[0]), (swa, sb_sc.shape[0]))):
        total = sum(work for _, _, work in group)
        before = 0
        for k, (scores, finish, work) in enumerate(group):
            items.append(((before + work / 2) / total, kind, k % nslot, scores, finish))
            before += work
    items.sort(key=lambda it: it[:2])
    state = {}
    for u in range(len(items) + ATTN_LOOKAHEAD):
        if u < len(items):
            _, _, slot, scores, _ = items[u]
            state[u] = scores(slot)
        done = u - ATTN_LOOKAHEAD
        if done >= 0:
            _, _, slot, _, finish = items[done]
            finish(slot, state.pop(done))


def _attention(slopes_a, slopes_b, sinks, qa, ka, vat, km, qb, kb, vbt, batch, seq):
    nb = seq // MOBA_BLOCK
    nblk = seq // SWA_WINDOW
    grp = B_GROUP * HEAD_DIM
    span = SWA_WINDOW + SWA_WINDOW // 2
    smem = pl.BlockSpec(memory_space=pltpu.SMEM)
    rows = lambda width: pl.BlockSpec((seq, width), lambda b: (b, 0))
    slots = ATTN_LOOKAHEAD + 1
    return pl.pallas_call(
        _attention_kernel, grid=(batch,),
        in_specs=[smem, smem, smem, rows(W_A), rows(W_A),
                  pl.BlockSpec((nb, W_A, MOBA_BLOCK), lambda b: (b, 0, 0)),
                  pl.BlockSpec((1, nb, W_A), lambda b: (b, 0, 0)),
                  rows(W_QB), rows(W_KB_DUP),
                  pl.BlockSpec((nblk, W_KB, SWA_WINDOW), lambda b: (b, 0, 0))],
        out_specs=(rows(W_A), rows(W_QB)),
        out_shape=(jax.ShapeDtypeStruct(qa.shape, _BF16), jax.ShapeDtypeStruct(qb.shape, _BF16)),
        scratch_shapes=[pltpu.VMEM((A_HEADS, seq, 2 * HEAD_DIM), _BF16),
                        pltpu.VMEM((A_HEADS, nb, HEAD_DIM + ONES_ROWS, MOBA_BLOCK), _BF16),
                        pltpu.VMEM((MOBA_BLOCK, MOBA_BLOCK), _F32),
                        pltpu.VMEM((slots, nb, MOBA_BLOCK, MOBA_BLOCK), _F32),
                        pltpu.VMEM((B_KV_HEADS, span, grp), _F32),
                        pltpu.VMEM((B_KV_HEADS, nblk, HEAD_DIM + ONES_ROWS, 2 * SWA_WINDOW), _BF16),
                        pltpu.VMEM((slots, span, grp), _F32)],
        compiler_params=pltpu.CompilerParams(dimension_semantics=("parallel",),
                                             vmem_limit_bytes=ATTN_VMEM_LIMIT),
        name="attention",
    )(slopes_a, slopes_b, sinks, qa, ka, vat, km, qb, kb, vbt)


def _merge_mlp_kernel(x_ref, oa_ref, ob_ref, g_ref, wa_ref, wb_ref, wo_ref, gm_ref, wup_ref, wdn_ref, o_ref):
    a = jnp.dot(oa_ref[...], wa_ref[...], preferred_element_type=_F32)
    b = jnp.dot(ob_ref[...], wb_ref[...], preferred_element_type=_F32)
    ga = g_ref[:, :D_MODEL].astype(_F32)
    gb = g_ref[:, D_MODEL:].astype(_F32)
    mixed = jax.nn.sigmoid(ga) * a + jax.nn.sigmoid(gb) * b
    x1 = x_ref[...] + jnp.dot(mixed.astype(_BF16), wo_ref[...], preferred_element_type=_F32)
    ms = jnp.mean(x1 * x1, axis=-1, keepdims=True)
    h2 = ((x1 * lax.rsqrt(ms + EPS)) * gm_ref[...]).astype(_BF16)
    acc = x1
    for c in range(D_FF // FF_CHUNK):
        u = jnp.dot(h2, wup_ref[:, c * FF_CHUNK:(c + 1) * FF_CHUNK], preferred_element_type=_F32)
        u = jnp.square(jnp.maximum(u, 0.0)).astype(_BF16)
        acc = acc + jnp.dot(u, wdn_ref[c * FF_CHUNK:(c + 1) * FF_CHUNK, :], preferred_element_type=_F32)
    o_ref[...] = acc


def _merge_mlp(x2, oa, ob, g, wa, wb, wo, gm, wup, wdn):
    n = x2.shape[0]
    tm = MERGE_TILE
    row = lambda w: pl.BlockSpec((tm, w), lambda i: (i, 0))
    return pl.pallas_call(
        _merge_mlp_kernel, grid=(n // tm,),
        in_specs=[row(D_MODEL), row(W_A), row(W_QB), row(W_GATES), _const_spec(wa.shape), _const_spec(wb.shape),
                  _const_spec(wo.shape), _const_spec(gm.shape), _const_spec(wup.shape), _const_spec(wdn.shape)],
        out_specs=row(D_MODEL),
        out_shape=jax.ShapeDtypeStruct(x2.shape, x2.dtype),
        compiler_params=pltpu.CompilerParams(dimension_semantics=("parallel",),
                                             vmem_limit_bytes=VMEM_LIMIT),
        name="merge_mlp",
    )(x2, oa, ob, g, wa, wb, wo, gm, wup, wdn)


def _alibi_slopes(n):
    return jnp.exp2(-(8.0 / n) * jnp.arange(1, n + 1, dtype=_F32))


def kernel(x, norm_attn, w_in, q_norm_a, k_norm_a, q_norm_b, k_norm_b, sinks_b, w_branch_a, w_branch_b, w_out,
           norm_mlp, w_up, w_down):
    batch, seq, d = x.shape
    assert d == D_MODEL and (batch * seq) % INPROJ_TILE == 0 and (batch * seq) % MERGE_TILE == 0
    assert seq % MOBA_BLOCK == 0 and INPROJ_TILE % MOBA_BLOCK == 0
    slopes = _alibi_slopes(N_ATTN_HEADS)
    slopes_b, slopes_a = slopes[:B_HEADS], slopes[B_HEADS:]
    x2 = x.reshape(batch * seq, d)
    for l in range(norm_attn.shape[0]):
        tile_gain = lambda g, reps: jnp.tile(g, reps)[None, :]
        (qa, ka, qb, kb, g, vat, vbt, km), (wa, wb, wo, wup, wdn) = _inproj(
            x2, norm_attn[l][None, :], w_in[l].astype(_BF16),
            tile_gain(q_norm_a[l], A_HEADS), tile_gain(k_norm_a[l], A_HEADS),
            tile_gain(q_norm_b[l], B_HEADS), tile_gain(k_norm_b[l], B_KV_HEADS),
            (w_branch_a[l], w_branch_b[l], w_out[l], w_up[l], w_down[l]))
        km = km.reshape(batch, seq // MOBA_BLOCK, W_A)
        oa, ob = _attention(slopes_a * LOG2E, slopes_b * LOG2E, sinks_b[l] * LOG2E, qa, ka, vat, km, qb, kb, vbt,
                            batch, seq)
        x2 = _merge_mlp(x2, oa, ob, g, wa, wb, wo, norm_mlp[l][None, :], wup, wdn)
    return x2.reshape(batch, seq, d)
```
